```python
import math
import jax, jax.numpy as jnp
from jax import lax
import numpy as np

D_MODEL = 1024
BATCH = 8
SEQ = 2048
DEPTH = 2
DEC_BATCH = 32
DEC_SEQ = 4
PAST_LEN = 16384
PAGE_SIZE = 128

ML_HEADS = 4
ML_DK = 128
ML_DV = 128
ML_CHUNK = 64
DF_HEADS = 4
DF_DK = 64
DF_DV = 2 * DF_DK
MLA_HEADS = 8
MLA_NOPE = 64
MLA_ROPE = 32
MLA_V = 64
MLA_Q_LORA = 384
MLA_KV_LORA = 256
ROPE_THETA = 10000.0
REL_BUCKETS = 32
REL_MAX_DIST = 128
N_BRANCH = 3
D_FF = ((8 * D_MODEL + 3 * 256 - 1) // (3 * 256)) * 256
Q_BLOCK = 128
EPS = 1e-6
NEG_INF = -1e30
IN_SIZES = (ML_HEADS * ML_DK, ML_HEADS * ML_DK, ML_HEADS * ML_DV, ML_HEADS * ML_DV, ML_HEADS, ML_HEADS,
            DF_HEADS * 2 * DF_DK, DF_HEADS * 2 * DF_DK, DF_HEADS * DF_DV,
            MLA_Q_LORA, MLA_KV_LORA, MLA_ROPE,
            N_BRANCH * D_MODEL)
IN_WIDTH = sum(IN_SIZES)

kernel_name = 'hybrid_mlstm_diffattn_mla_decoder_step'


def rmsnorm(x, g):
    xf = x.astype(jnp.float32)
    y = xf * lax.rsqrt(jnp.mean(xf * xf, axis=-1, keepdims=True) + EPS) * g.astype(jnp.float32)
    return y.astype(x.dtype)


def head_rmsnorm(x, g, n_heads):
    xs = x.reshape(x.shape[:-1] + (n_heads, -1))
    return rmsnorm(xs, g.reshape(n_heads, -1)).reshape(x.shape)


def rope(x, pos):
    half = x.shape[-1] // 2
    freqs = ROPE_THETA ** (-jnp.arange(half, dtype=jnp.float32) / half)
    ang = pos.astype(jnp.float32)[:, None] * freqs[None, :]
    shape = (1, x.shape[1]) + (1,) * (x.ndim - 3) + (half,)
    cos, sin = jnp.cos(ang).reshape(shape), jnp.sin(ang).reshape(shape)
    xf = x.astype(jnp.float32)
    x1, x2 = xf[..., :half], xf[..., half:]
    return jnp.concatenate([x1 * cos - x2 * sin, x2 * cos + x1 * sin], axis=-1).astype(x.dtype)


def t5_bucket(qpos, kpos):
    n = jnp.maximum(qpos[:, None] - kpos[None, :], 0)
    exact = REL_BUCKETS // 2
    nf = jnp.maximum(n, 1).astype(jnp.float32)
    large = exact + (jnp.log(nf / exact) / math.log(REL_MAX_DIST / exact) * (REL_BUCKETS - exact)).astype(jnp.int32)
    large = jnp.minimum(large, REL_BUCKETS - 1)
    return jnp.where(n < exact, n, large)


def over_query_blocks(fn, qs, qpos):
    t = qpos.shape[0]
    if t <= Q_BLOCK or t % Q_BLOCK:
        return fn(*qs, qpos)
    nb = t // Q_BLOCK

    def split(a):
        return jnp.moveaxis(a.reshape((a.shape[0], nb, Q_BLOCK) + a.shape[2:]), 1, 0)

    out = lax.map(lambda args: fn(*args[0], args[1]), (tuple(split(a) for a in qs), qpos.reshape(nb, Q_BLOCK)))
    return jnp.moveaxis(out, 0, 1).reshape((out.shape[1], t) + out.shape[3:])


def split_keys(p, segs):
    offsets = np.cumsum([s[2].shape[0] for s in segs])[:-1].tolist()
    return jnp.split(p, offsets, axis=-1)


def mlstm_chunked(q, k, v, li, lf, c0, n0, m0):
    b_, t, h_, dk = q.shape
    L = math.gcd(t, ML_CHUNK)
    nc = t // L

    def chunks(a):
        a = a.reshape((b_, nc, L, h_) + a.shape[3:])
        return jnp.moveaxis(a, (1, 3), (0, 2))

    causal = jnp.tril(jnp.ones((L, L), dtype=bool))

    def step(carry, inp):
        C, n, m = carry
        qc, kc, vc, ic, fc = inp
        b = jnp.cumsum(fc, axis=-1)
        D = jnp.where(causal, b[..., :, None] - b[..., None, :] + ic[..., None, :], -jnp.inf)
        inter = b + m[..., None]
        mt = jnp.maximum(inter, jnp.max(D, axis=-1))
        w = jnp.exp(D - mt[..., None])
        iw = jnp.exp(inter - mt)
        A = jnp.einsum('bhtd,bhsd->bhts', qc, kc) * w
        num = jnp.einsum('bhts,bhsv->bhtv', A, vc) + iw[..., None] * jnp.einsum('bhtd,bhdv->bhtv', qc, C)
        den = jnp.sum(A, axis=-1) + iw * jnp.einsum('bhtd,bhd->bht', qc, n)
        h = num / jnp.maximum(jnp.abs(den), jnp.exp(-mt))[..., None]
        m_new = mt[..., -1]
        wl = jnp.exp(b[..., -1:] - b + ic - m_new[..., None])
        dec = jnp.exp(b[..., -1] + m - m_new)
        C_new = dec[..., None, None] * C + jnp.einsum('bhs,bhsd,bhsv->bhdv', wl, kc, vc)
        n_new = dec[..., None] * n + jnp.einsum('bhs,bhsd->bhd', wl, kc)
        return (C_new, n_new, m_new), h

    (C, n, m), hs = lax.scan(step, (c0, n0, m0), tuple(chunks(a) for a in (q, k, v, li, lf)))
    hs = jnp.moveaxis(hs, (0, 2), (1, 3)).reshape(b_, t, h_, -1)
    return hs, C, n, m


def diff_attention(q, qpos, segs, lam, rel_bias):
    def block(qb, qp):
        scores = []
        for k, v, kp in segs:
            s = jnp.einsum('bqhmd,bkhmd->bhmqk', qb, k, preferred_element_type=jnp.float32) * DF_DK ** -0.5
            bias = jnp.transpose(rel_bias[t5_bucket(qp, kp)], (2, 0, 1)).astype(jnp.float32)
            scores.append(jnp.where(kp[None, :] <= qp[:, None], s + bias[None, :, None], NEG_INF))
        p = jax.nn.softmax(jnp.concatenate(scores, axis=-1), axis=-1)
        a = p[:, :, 0] - lam * p[:, :, 1]
        outs = [jnp.einsum('bhqk,bkhe->bqhe', a_s.astype(s_[1].dtype), s_[1])
                for a_s, s_ in zip(split_keys(a, segs), segs)]
        out = outs[0]
        for o in outs[1:]:
            out = out + o
        return out
    return over_query_blocks(block, (q,), qpos)


def mla_attention(q_lat, q_rope, qpos, segs):
    scale = (MLA_NOPE + MLA_ROPE) ** -0.5

    def block(ql, qr, qp):
        scores = []
        for ckv, kr, kp in segs:
            s = (jnp.einsum('bqhc,bkc->bhqk', ql, ckv, preferred_element_type=jnp.float32)
                 + jnp.einsum('bqhr,bkr->bhqk', qr, kr, preferred_element_type=jnp.float32)) * scale
            scores.append(jnp.where(kp[None, :] <= qp[:, None], s, NEG_INF))
        p = jax.nn.softmax(jnp.concatenate(scores, axis=-1), axis=-1)
        outs = [jnp.einsum('bhqk,bkc->bqhc', p_s.astype(s_[0].dtype), s_[0])
                for p_s, s_ in zip(split_keys(p, segs), segs)]
        out = outs[0]
        for o in outs[1:]:
            out = out + o
        return out
    return over_query_blocks(block, (q_lat, q_rope), qpos)


def decoder_layer(l, x, pos, past, w):
    B, T, _ = x.shape
    f32 = jnp.float32
    h = rmsnorm(x, w['norm'][0])
    z = h @ w['w_in']
    (mq, mk, mv, mo, mi, mf, dq, dk, dv, cq, ckv, kr, gl) = jnp.split(
        z, np.cumsum(IN_SIZES)[:-1].tolist(), axis=-1)

    q_m = mq.reshape(B, T, ML_HEADS, ML_DK).astype(f32)
    k_m = mk.reshape(B, T, ML_HEADS, ML_DK).astype(f32) * ML_DK ** -0.5
    v_m = mv.reshape(B, T, ML_HEADS, ML_DV).astype(f32)
    bg = w['b_ml'].astype(f32)
    li = mi.astype(f32) + bg[:ML_HEADS]
    lf = jax.nn.log_sigmoid(mf.astype(f32) + bg[ML_HEADS:])
    if past is None:
        c0 = jnp.zeros((B, ML_HEADS, ML_DK, ML_DV), f32)
        n0 = jnp.zeros((B, ML_HEADS, ML_DK), f32)
        m0 = jnp.zeros((B, ML_HEADS), f32)
    else:
        c0, n0, m0 = (a.astype(f32) for a in past[5:])
    h_m, C, n, m = mlstm_chunked(q_m, k_m, v_m, li, lf, c0, n0, m0)
    y_m = head_rmsnorm(h_m.reshape(B, T, -1).astype(x.dtype), w['ml_norm'], ML_HEADS) * jax.nn.sigmoid(mo)

    q_d = dq.reshape(B, T, DF_HEADS, 2, DF_DK)
    k_d = dk.reshape(B, T, DF_HEADS, 2, DF_DK)
    v_d = dv.reshape(B, T, DF_HEADS, DF_DV)
    lam_init = 0.8 - 0.6 * math.exp(-0.3 * l)
    lv = w['df_lambda'].astype(f32)
    lam = jnp.exp(jnp.sum(lv[0] * lv[1])) - jnp.exp(jnp.sum(lv[2] * lv[3])) + lam_init
    segs_d = [(k_d, v_d, pos)] if past is None else [(past[0], past[1], past[4]), (k_d, v_d, pos)]
    o_d = diff_attention(q_d, pos, segs_d, lam, w['rel_bias'])
    y_d = head_rmsnorm(o_d.reshape(B, T, -1), w['df_norm'], DF_HEADS) * (1.0 - lam_init)

    c_q = rmsnorm(cq, w['q_norm'])
    q_full = (c_q @ w['w_uq']).reshape(B, T, MLA_HEADS, MLA_NOPE + MLA_ROPE)
    q_nope = q_full[..., :MLA_NOPE]
    q_rope = rope(q_full[..., MLA_NOPE:], pos)
    c_kv = rmsnorm(ckv, w['kv_norm'])
    k_rope = rope(kr, pos)
    q_lat = jnp.einsum('bthn,chn->bthc', q_nope, w['w_uk'])
    segs_m = [(c_kv, k_rope, pos)] if past is None else [(past[2], past[3], past[4]), (c_kv, k_rope, pos)]
    o_lat = mla_attention(q_lat, q_rope, pos, segs_m)
    y_c = jnp.einsum('bthc,chv->bthv', o_lat, w['w_uv']).reshape(B, T, -1)

    g = jax.nn.sigmoid(gl).reshape(B, T, N_BRANCH, D_MODEL)
    u = (g[:, :, 0] * (y_m @ w['w_br_ml']) + g[:, :, 1] * (y_d @ w['w_br_df'])
         + g[:, :, 2] * (y_c @ w['w_br_mla']))
    x = x + rmsnorm(u @ w['w_out'], w['norm'][1])

    hf = rmsnorm(x, w['norm'][2])
    a, bb = jnp.split(hf @ w['w_ffn_in'], 2, axis=-1)
    x = x + rmsnorm((jax.nn.silu(a) * bb) @ w['w_ffn_out'], w['norm'][3])
    new = (k_d, v_d, c_kv, k_rope, C.astype(x.dtype), n.astype(x.dtype), m.astype(x.dtype))
    return x, new


def setup_inputs(seed: int = 0) -> dict:
    key = jax.random.key(seed)
    ks = iter(jax.random.split(key, 40))
    f32 = jnp.float32

    def nrm(shape, scale):
        return jax.random.normal(next(ks), shape, f32) * scale

    n_pages = PAST_LEN // PAGE_SIZE
    n_used = DEC_BATCH * n_pages
    n_pool = n_used + max(1, n_used // 4)
    page_table = jax.random.permutation(next(ks), n_pool)[:n_used].reshape(DEC_BATCH, n_pages).astype(jnp.int32)
    b_i = nrm((DEPTH, ML_HEADS), 0.1)
    b_f = jnp.linspace(3.0, 6.0, ML_HEADS, dtype=f32)[None, :] + nrm((DEPTH, ML_HEADS), 0.1)
    return {
        'x_prompt': nrm((BATCH, SEQ, D_MODEL), 1.0),
        'x_sample': nrm((DEC_BATCH, DEC_SEQ, D_MODEL), 1.0),
        'state_mlstm_C': nrm((DEPTH, DEC_BATCH, ML_HEADS, ML_DK, ML_DV), 0.1),
        'state_mlstm_n': nrm((DEPTH, DEC_BATCH, ML_HEADS, ML_DK), 0.5),
        'state_mlstm_m': nrm((DEPTH, DEC_BATCH, ML_HEADS), 1.0),
        'cache_diff_k': nrm((DEPTH, n_pool, PAGE_SIZE, DF_HEADS, 2, DF_DK), 1.0),
        'cache_diff_v': nrm((DEPTH, n_pool, PAGE_SIZE, DF_HEADS, DF_DV), 1.0),
        'cache_mla_ckv': nrm((DEPTH, n_pool, PAGE_SIZE, MLA_KV_LORA), 1.0),
        'cache_mla_krope': nrm((DEPTH, n_pool, PAGE_SIZE, MLA_ROPE), 1.0),
        'page_table': page_table,
        'norm_gains': 1.0 + nrm((DEPTH, 4, D_MODEL), 0.1),
        'w_in': nrm((DEPTH, D_MODEL, IN_WIDTH), D_MODEL ** -0.5),
        'b_ml_gates': jnp.concatenate([b_i, b_f], axis=-1),
        'ml_head_norm': 1.0 + nrm((DEPTH, ML_HEADS * ML_DV), 0.1),
        'df_lambda': nrm((DEPTH, 4, DF_DK), 0.1),
        'df_head_norm': 1.0 + nrm((DEPTH, DF_HEADS * DF_DV), 0.1),
        'rel_bias': nrm((REL_BUCKETS, DF_HEADS), 0.5),
        'mla_q_norm': 1.0 + nrm((DEPTH, MLA_Q_LORA), 0.1),
        'mla_kv_norm': 1.0 + nrm((DEPTH, MLA_KV_LORA), 0.1),
        'w_uq': nrm((DEPTH, MLA_Q_LORA, MLA_HEADS * (MLA_NOPE + MLA_ROPE)), MLA_Q_LORA ** -0.5),
        'w_uk': nrm((DEPTH, MLA_KV_LORA, MLA_HEADS, MLA_NOPE), MLA_KV_LORA ** -0.5),
        'w_uv': nrm((DEPTH, MLA_KV_LORA, MLA_HEADS, MLA_V), MLA_KV_LORA ** -0.5),
        'w_br_ml': nrm((DEPTH, ML_HEADS * ML_DV, D_MODEL), (ML_HEADS * ML_DV) ** -0.5),
        'w_br_df': nrm((DEPTH, DF_HEADS * DF_DV, D_MODEL), (DF_HEADS * DF_DV) ** -0.5),
        'w_br_mla': nrm((DEPTH, MLA_HEADS * MLA_V, D_MODEL), (MLA_HEADS * MLA_V) ** -0.5),
        'w_out': nrm((DEPTH, D_MODEL, D_MODEL), D_MODEL ** -0.5),
        'w_ffn_in': nrm((DEPTH, D_MODEL, 2 * D_FF), D_MODEL ** -0.5),
        'w_ffn_out': nrm((DEPTH, D_FF, D_MODEL), D_FF ** -0.5),
    }


def reference(x_prompt, x_sample, state_mlstm_C, state_mlstm_n, state_mlstm_m,
              cache_diff_k, cache_diff_v, cache_mla_ckv, cache_mla_krope, page_table,
              norm_gains, w_in, b_ml_gates, ml_head_norm, df_lambda, df_head_norm, rel_bias,
              mla_q_norm, mla_kv_norm, w_uq, w_uk, w_uv, w_br_ml, w_br_df, w_br_mla, w_out,
              w_ffn_in, w_ffn_out):
    dec_b, n_pages = page_table.shape
    page = cache_diff_k.shape[2]
    past_len = n_pages * page
    pos_p = jnp.arange(x_prompt.shape[1], dtype=jnp.int32)
    pos_s = past_len + jnp.arange(x_sample.shape[1], dtype=jnp.int32)
    past_pos = jnp.arange(past_len, dtype=jnp.int32)
    yp, ys = x_prompt, x_sample
    st_p, st_s = [], []
    for l in range(DEPTH):
        w = {'norm': norm_gains[l], 'w_in': w_in[l], 'b_ml': b_ml_gates[l], 'ml_norm': ml_head_norm[l],
             'df_lambda': df_lambda[l], 'df_norm': df_head_norm[l], 'rel_bias': rel_bias,
             'q_norm': mla_q_norm[l], 'kv_norm': mla_kv_norm[l], 'w_uq': w_uq[l], 'w_uk': w_uk[l],
             'w_uv': w_uv[l], 'w_br_ml': w_br_ml[l], 'w_br_df': w_br_df[l], 'w_br_mla': w_br_mla[l],
             'w_out': w_out[l], 'w_ffn_in': w_ffn_in[l], 'w_ffn_out': w_ffn_out[l]}
        yp, sp = decoder_layer(l, yp, pos_p, None, w)
        st_p.append(sp)
        pk = cache_diff_k[l, page_table].reshape(dec_b, past_len, DF_HEADS, 2, DF_DK)
        pv = cache_diff_v[l, page_table].reshape(dec_b, past_len, DF_HEADS, DF_DV)
        pc = cache_mla_ckv[l, page_table].reshape(dec_b, past_len, MLA_KV_LORA)
        pr = cache_mla_krope[l, page_table].reshape(dec_b, past_len, MLA_ROPE)
        past = (pk, pv, pc, pr, past_pos, state_mlstm_C[l], state_mlstm_n[l], state_mlstm_m[l])
        ys, ss = decoder_layer(l, ys, pos_s, past, w)
        st_s.append(ss)
    dk_p, dv_p, ckv_p, kr_p, C_p, n_p, m_p = [jnp.stack(a) for a in zip(*st_p)]
    dk_s, dv_s, ckv_s, kr_s, C_s, n_s, m_s = [jnp.stack(a) for a in zip(*st_s)]
    return (yp, ys, dk_p, dv_p, ckv_p, kr_p, C_p, n_p, m_p, dk_s, dv_s, ckv_s, kr_s, C_s, n_s, m_s)
```

```python
import functools
import math

import numpy as np
import jax
import jax.numpy as jnp
from jax import lax
from jax.experimental import pallas as pl
from jax.experimental.pallas import tpu as pltpu

F32 = jnp.float32
BF16 = jnp.bfloat16

D_MODEL = 1024
ML_HEADS = 4
ML_DK = 128
ML_DV = 128
DF_HEADS = 4
DF_DK = 64
DF_DV = 2 * DF_DK
MLA_HEADS = 8
MLA_NOPE = 64
MLA_ROPE = 32
MLA_V = 64
MLA_Q_LORA = 384
MLA_KV_LORA = 256
ROPE_THETA = 10000.0
REL_BUCKETS = 32
REL_MAX_DIST = 128
N_BRANCH = 3
D_FF = 2816
EPS = 1e-6
NEG_INF = -1e30
IN_SIZES = (ML_HEADS * ML_DK, ML_HEADS * ML_DK, ML_HEADS * ML_DV, ML_HEADS * ML_DV, ML_HEADS, ML_HEADS,
            DF_HEADS * 2 * DF_DK, DF_HEADS * 2 * DF_DK, DF_HEADS * DF_DV,
            MLA_Q_LORA, MLA_KV_LORA, MLA_ROPE,
            N_BRANCH * D_MODEL)

LANES = 128
SUBLANES = 8
HEAD_PAD = 128
ROW_TILE = 256
ATT_TILE = 256
ML_CHUNK = 256
SAMPLE_PAD = 8
PAGES_PER_STEP = 8
VMEM_LIMIT = 56 * 1024 * 1024

_NT = (((1,), (1,)), ((), ()))
_TN = (((0,), (0,)), ((), ()))


def _params(sem):
    return pltpu.CompilerParams(dimension_semantics=sem, vmem_limit_bytes=VMEM_LIMIT)


def _const_spec(shape):
    nd = len(shape)
    return pl.BlockSpec(shape, lambda *_: (0,) * nd, pipeline_mode=pl.Buffered(1))


def _rms(x, g):
    return x * lax.rsqrt(jnp.mean(x * x, axis=-1, keepdims=True) + EPS) * g


def _t5_bucket_np(n):
    n = np.asarray(n, np.int64)
    exact = REL_BUCKETS // 2
    nf = np.maximum(n, 1).astype(np.float32)
    large = exact + (np.log(nf / np.float32(exact)) / np.float32(math.log(REL_MAX_DIST / exact))
                     * np.float32(REL_BUCKETS - exact)).astype(np.int32)
    large = np.minimum(large, REL_BUCKETS - 1)
    b = np.where(n < exact, n, large)
    return np.where(n < 0, -1, b).astype(np.int32)


def _bias_table_kernel(rb_ref, idx_ref, out_ref):
    idx = idx_ref[...]
    for h in range(DF_HEADS):
        acc = jnp.full(idx.shape, NEG_INF, F32)
        for b in range(REL_BUCKETS):
            acc = jnp.where(idx == b, rb_ref[b, h], acc)
        out_ref[h] = acc


def _bias_table(rel_bias, idx_np):
    r, c = idx_np.shape
    return pl.pallas_call(
        _bias_table_kernel,
        out_shape=jax.ShapeDtypeStruct((DF_HEADS, r, c), F32),
        in_specs=[pl.BlockSpec(memory_space=pltpu.SMEM), pl.BlockSpec(memory_space=pltpu.VMEM)],
        out_specs=pl.BlockSpec(memory_space=pltpu.VMEM),
        name="bias_table",
    )(rel_bias, jnp.asarray(idx_np))


def _proj_kernel(x_ref, g_ref, cs_ref, sn_ref, wml, wdf, wcq, wckv, wmisc, wgl, qn_ref, kvn_ref,
                 wuqa, wuqb, wuk, wuva, wuvb,
                 ml_o, gif_o, dq_o, dk_o, dv_o, qm_o, ckv_o, kr_o, gate_o, *prompt_outs, prompt):
    x = x_ref[...]
    h = _rms(x, g_ref[...]).astype(BF16)

    def mm(w_ref, lo, hi):
        return jnp.dot(h, w_ref[:, lo:hi], preferred_element_type=F32)

    w = ML_HEADS * ML_DK
    act = ml_o.dtype
    ml_o[:, 0:w] = mm(wml, 0, w).astype(act)
    ml_o[:, w:2 * w] = (mm(wml, w, 2 * w) * ML_DK ** -0.5).astype(act)
    ml_o[:, 2 * w:3 * w] = mm(wml, 2 * w, 3 * w).astype(act)
    ml_o[:, 3 * w:4 * w] = jax.nn.sigmoid(mm(wml, 3 * w, 4 * w)).astype(act)

    misc = mm(wmisc, 0, 3 * LANES)
    gif_o[...] = misc[:, 0:LANES]
    cs = cs_ref[...]
    sn = sn_ref[...]
    krp = misc[:, LANES:2 * LANES] * cs + misc[:, 2 * LANES:3 * LANES] * sn
    kr_o[...] = krp[:, MLA_NOPE:MLA_NOPE + MLA_ROPE]

    wd = DF_HEADS * 2 * DF_DK
    dq_o[...] = (mm(wdf, 0, wd) * DF_DK ** -0.5).astype(act)
    dk = mm(wdf, wd, 2 * wd)
    dv = mm(wdf, 2 * wd, 3 * wd)
    dk_o[...] = dk
    dv_o[...] = dv

    c_q = _rms(mm(wcq, 0, MLA_Q_LORA), qn_ref[...]).astype(BF16)
    c_kv = _rms(mm(wckv, 0, MLA_KV_LORA), kvn_ref[...])
    ckv_o[...] = c_kv
    scale = (MLA_NOPE + MLA_ROPE) ** -0.5
    for hh in range(MLA_HEADS):
        sl = slice(hh * HEAD_PAD, (hh + 1) * HEAD_PAD)
        qa = jnp.dot(c_q, wuqa[:, sl], preferred_element_type=F32)
        qb = jnp.dot(c_q, wuqb[:, sl], preferred_element_type=F32)
        qm_o[:, sl] = ((qa * cs + qb * sn) * scale).astype(act)

    gw = 512
    for j in range(N_BRANCH * D_MODEL // gw):
        gate_o[:, j * gw:(j + 1) * gw] = jax.nn.sigmoid(mm(wgl, j * gw, (j + 1) * gw)).astype(BF16)

    if prompt:
        dkb_o, dvb_o, km_o, va_o, vb_o = prompt_outs
        dkb_o[...] = dk.astype(BF16)
        dvb_o[...] = dv.astype(BF16)
        ckb = c_kv.astype(BF16)
        for hh in range(MLA_HEADS):
            sl = slice(hh * HEAD_PAD, (hh + 1) * HEAD_PAD)
            kn = jnp.dot(ckb, wuk[:, sl], preferred_element_type=F32)
            km_o[:, sl] = (kn + krp).astype(BF16)
        va_o[...] = jnp.dot(ckb, wuva[...], preferred_element_type=F32).astype(BF16)
        vb_o[...] = jnp.dot(ckb, wuvb[...], preferred_element_type=F32).astype(BF16)


def _proj(x, lw, cs, sn, prompt):
    rows = x.shape[0]
    tm = min(ROW_TILE, rows)
    assert rows % tm == 0
    row = lambda width: pl.BlockSpec((tm, width), lambda i: (i, 0))
    weights = [lw['wml'], lw['wdf'], lw['wcq'], lw['wckv'], lw['wmisc'], lw['wgl'], lw['q_norm'], lw['kv_norm'],
               lw['wuqa'], lw['wuqb'], lw['wuk'], lw['wuva'], lw['wuvb']]
    act = BF16 if prompt else F32
    out_widths = [(4 * ML_HEADS * ML_DK, act), (LANES, F32), (512, act), (512, F32), (512, F32),
                  (MLA_HEADS * HEAD_PAD, act), (MLA_KV_LORA, F32), (MLA_ROPE, F32), (N_BRANCH * D_MODEL, BF16)]
    if prompt:
        out_widths += [(512, BF16), (512, BF16), (MLA_HEADS * HEAD_PAD, BF16), (512, BF16), (512, BF16)]
    return pl.pallas_call(
        functools.partial(_proj_kernel, prompt=prompt),
        grid=(rows // tm,),
        in_specs=[row(D_MODEL), _const_spec((1, D_MODEL)), row(LANES), row(LANES)]
                 + [_const_spec(w.shape) for w in weights],
        out_specs=[row(wd) for wd, _ in out_widths],
        out_shape=[jax.ShapeDtypeStruct((rows, wd), dt) for wd, dt in out_widths],
        compiler_params=_params(("parallel",)),
        name="proj_prompt" if prompt else "proj_sample",
    )(x, lw['g0'], cs, sn, *weights)


def _mlstm_kernel(m0_ref, ml_ref, gif_ref, gb_ref, hn_ref, c0_ref, n0_ref,
                  ym_o, c_o, n_o, m_o, cn_scr, m_scr, *, chunk, t_valid):
    b = pl.program_id(0)
    c = pl.program_id(1)
    L = chunk
    dk, dv = ML_DK, ML_DV
    lane = lax.broadcasted_iota(jnp.int32, (1, LANES), 1)

    @pl.when(c == 0)
    def _():
        for h in range(ML_HEADS):
            ncol = jnp.where(lax.broadcasted_iota(jnp.int32, (dk, LANES), 1) == 0, n0_ref[0, h], 0.0)
            cn_scr[h] = jnp.concatenate([c0_ref[0, h], ncol], axis=1)
            m_scr[h] = jnp.full((SUBLANES, LANES), m0_ref[b, h], F32)

    g = gif_ref[...] + gb_ref[...]
    g = jnp.where(lane < ML_HEADS, g, jnp.minimum(g, 0.0) - jnp.log(1.0 + jnp.exp(-jnp.abs(g))))
    row_i = lax.broadcasted_iota(jnp.int32, (L, L), 0)
    col_i = lax.broadcasted_iota(jnp.int32, (L, L), 1)
    if t_valid < L:
        tok = lax.broadcasted_iota(jnp.int32, (L, LANES), 0)
        g = jnp.where(tok < t_valid, g, jnp.where(lane < ML_HEADS, NEG_INF, 0.0))
    causal = col_i <= row_i
    eye = col_i == row_i

    for h in range(ML_HEADS):
        lic = g[:, h:h + 1]
        lfc = g[:, ML_HEADS + h:ML_HEADS + h + 1]
        br = jnp.sum(jnp.where(row_i <= col_i, lfc, 0.0), axis=0, keepdims=True)
        bc = jnp.sum(jnp.where(eye, br, 0.0), axis=1, keepdims=True)
        ir = jnp.sum(jnp.where(eye, lic, 0.0), axis=0, keepdims=True)
        b_last = jnp.sum(lfc, axis=0, keepdims=True)
        m_prev = m_scr[h][0:1, 0:1]

        q = ml_ref[:, h * dk:(h + 1) * dk].astype(BF16)
        k = ml_ref[:, (ML_HEADS + h) * dk:(ML_HEADS + h + 1) * dk].astype(BF16)
        v = ml_ref[:, (2 * ML_HEADS + h) * dk:(2 * ML_HEADS + h + 1) * dk].astype(BF16)
        og = ml_ref[:, (3 * ML_HEADS + h) * dk:(3 * ML_HEADS + h + 1) * dk]

        dmat = jnp.where(causal, bc - br + ir, NEG_INF)
        inter = bc + m_prev
        mt = jnp.maximum(inter, jnp.max(dmat, axis=1, keepdims=True))
        wgt = jnp.exp(dmat - mt)
        iw = jnp.exp(inter - mt)
        a = lax.dot_general(q, k, _NT, preferred_element_type=F32) * wgt
        cn = cn_scr[h]
        qc = jnp.dot(q, cn.astype(BF16), preferred_element_type=F32)
        num = jnp.dot(a.astype(BF16), v, preferred_element_type=F32) + iw * qc[:, 0:dv]
        den = jnp.sum(a, axis=1, keepdims=True) + iw * qc[:, dv:dv + 1]
        hh = num / jnp.maximum(jnp.abs(den), jnp.exp(-mt))
        y = _rms(hh, hn_ref[:, h * dv:(h + 1) * dv]) * og.astype(F32)
        ym_o[:, h * dv:(h + 1) * dv] = y.astype(ym_o.dtype)

        m_new = jnp.maximum(b_last + m_prev, jnp.max(b_last - br + ir, axis=1, keepdims=True))
        wl = jnp.exp(b_last - bc + lic - m_new)
        dec = jnp.exp(b_last + m_prev - m_new)
        ones_col = jnp.where(lax.broadcasted_iota(jnp.int32, (L, LANES), 1) == 0, wl, 0.0)
        wv = jnp.concatenate([wl * v.astype(F32), ones_col], axis=1).astype(BF16)
        cn_new = dec * cn + lax.dot_general(k, wv, _TN, preferred_element_type=F32)
        cn_scr[h] = cn_new
        m_scr[h] = jnp.broadcast_to(m_new, (SUBLANES, LANES))
        c_o[0, h] = cn_new[:, 0:dv]
        n_o[0, h] = cn_new[:, dv:dv + 1]
        m_o[0, h] = jnp.broadcast_to(m_new, (SUBLANES, LANES))


def _mlstm(ml, gif, lw, c0, n0, m0, batch, seq, chunk, t_valid):
    nc = seq // chunk
    assert seq % chunk == 0
    width = ML_HEADS * ML_DV
    grid_spec = pltpu.PrefetchScalarGridSpec(
        num_scalar_prefetch=1,
        grid=(batch, nc),
        in_specs=[
            pl.BlockSpec((chunk, 4 * width), lambda b, c, m: (b * nc + c, 0)),
            pl.BlockSpec((chunk, LANES), lambda b, c, m: (b * nc + c, 0)),
            pl.BlockSpec((1, LANES), lambda b, c, m: (0, 0)),
            pl.BlockSpec((1, width), lambda b, c, m: (0, 0)),
            pl.BlockSpec((1, ML_HEADS, ML_DK, ML_DV), lambda b, c, m: (b, 0, 0, 0)),
            pl.BlockSpec((1, ML_HEADS, ML_DK, 1), lambda b, c, m: (b, 0, 0, 0)),
        ],
        out_specs=[
            pl.BlockSpec((chunk, width), lambda b, c, m: (b * nc + c, 0)),
            pl.BlockSpec((1, ML_HEADS, ML_DK, ML_DV), lambda b, c, m: (b, 0, 0, 0)),
            pl.BlockSpec((1, ML_HEADS, ML_DK, 1), lambda b, c, m: (b, 0, 0, 0)),
            pl.BlockSpec((1, ML_HEADS, SUBLANES, LANES), lambda b, c, m: (b, 0, 0, 0)),
        ],
        scratch_shapes=[pltpu.VMEM((ML_HEADS, ML_DK, 2 * ML_DV), F32),
                        pltpu.VMEM((ML_HEADS, SUBLANES, LANES), F32)],
    )
    ym, c_new, n_new, m_new = pl.pallas_call(
        functools.partial(_mlstm_kernel, chunk=chunk, t_valid=t_valid),
        grid_spec=grid_spec,
        out_shape=[jax.ShapeDtypeStruct((batch * seq, width), ml.dtype),
                   jax.ShapeDtypeStruct((batch, ML_HEADS, ML_DK, ML_DV), F32),
                   jax.ShapeDtypeStruct((batch, ML_HEADS, ML_DK, 1), F32),
                   jax.ShapeDtypeStruct((batch, ML_HEADS, SUBLANES, LANES), F32)],
        compiler_params=_params(("parallel", "arbitrary")),
        name="mlstm",
    )(m0, ml, gif, lw['gate_bias'], lw['ml_norm'], c0, n0[..., None])
    return ym, c_new, n_new[..., 0], m_new[:, :, 0, 0]


def _softmax_first(s, pv):
    m = jnp.max(s, axis=1, keepdims=True)
    p = jnp.exp(s - m)
    return m, jnp.sum(p, axis=1, keepdims=True), pv(p.astype(BF16))


def _softmax_next(carry, s, pv):
    m, l, acc = carry
    m_new = jnp.maximum(m, jnp.max(s, axis=1, keepdims=True))
    alpha = jnp.exp(m - m_new)
    p = jnp.exp(s - m_new)
    return m_new, alpha * l + jnp.sum(p, axis=1, keepdims=True), alpha * acc + pv(p.astype(BF16))


def _lambda(lv_ref, lam_init):
    lv = lv_ref[...]
    e1 = jnp.exp(jnp.sum(lv[0:1] * lv[1:2], axis=1, keepdims=True))
    e2 = jnp.exp(jnp.sum(lv[2:3] * lv[3:4], axis=1, keepdims=True))
    return e1 - e2 + lam_init


def _dfp_kernel(rb_ref, q_ref, k_ref, v_ref, bias_ref, lv_ref, hn_ref, o_ref, *, tile, lam_init):
    h = pl.program_id(1)
    qi = pl.program_id(2)
    T = tile
    q = q_ref[...]
    lane = lax.broadcasted_iota(jnp.int32, (T, 2 * DF_DK), 1)
    zero = jnp.zeros_like(q)
    q2 = jnp.concatenate([jnp.where(lane < DF_DK, q, zero), jnp.where(lane >= DF_DK, q, zero)], axis=0)

    def tile_scores(start):
        kt = k_ref[pl.ds(start, T), :]
        vt = v_ref[pl.ds(start, T), :]
        s = lax.dot_general(q2, kt, _NT, preferred_element_type=F32)
        return s, lambda p: jnp.dot(p, vt, preferred_element_type=F32)

    s, pv = tile_scores(pl.multiple_of(qi * T, T))
    carry = _softmax_first(s + bias_ref[0, 0], pv)
    sub = jnp.maximum(qi - 1, 0)
    s, pv = tile_scores(pl.multiple_of(sub * T, T))
    carry = _softmax_next(carry, s + bias_ref[0, 1] + jnp.where(qi == 0, NEG_INF, 0.0), pv)
    far_bias = rb_ref[REL_BUCKETS - 1, h]

    def far(ki, carry):
        s, pv = tile_scores(pl.multiple_of(ki * T, T))
        return _softmax_next(carry, s + far_bias, pv)

    m, l, acc = lax.fori_loop(0, sub, far, carry)
    o = acc / l
    lam = _lambda(lv_ref, lam_init)
    od = o[0:T] - lam * o[T:2 * T]
    o_ref[...] = (_rms(od, hn_ref[...]) * (1.0 - lam_init)).astype(BF16)


def _dfp(dq, dkb, dvb, bias_tiles, rel_bias, lw, batch, seq, lam_init):
    T = ATT_TILE
    nq = seq // T
    assert seq % T == 0 and T >= REL_MAX_DIST
    hw = 2 * DF_DK
    grid_spec = pltpu.PrefetchScalarGridSpec(
        num_scalar_prefetch=0,
        grid=(batch, DF_HEADS, nq),
        in_specs=[
            pl.BlockSpec(memory_space=pltpu.SMEM),
            pl.BlockSpec((T, hw), lambda b, h, i: (b * nq + i, h)),
            pl.BlockSpec((seq, hw), lambda b, h, i: (b, h)),
            pl.BlockSpec((seq, hw), lambda b, h, i: (b, h)),
            pl.BlockSpec((1, 2, 2 * T, T), lambda b, h, i: (h, 0, 0, 0)),
            pl.BlockSpec((4, DF_DK), lambda b, h, i: (0, 0)),
            pl.BlockSpec((1, hw), lambda b, h, i: (0, h)),
        ],
        out_specs=pl.BlockSpec((T, hw), lambda b, h, i: (b * nq + i, h)),
    )
    return pl.pallas_call(
        functools.partial(_dfp_kernel, tile=T, lam_init=lam_init),
        grid_spec=grid_spec,
        out_shape=jax.ShapeDtypeStruct((batch * seq, DF_HEADS * hw), BF16),
        compiler_params=_params(("parallel", "parallel", "arbitrary")),
        name="dfp",
    )(rel_bias, dq, dkb, dvb, bias_tiles, lw['df_lambda'], lw['df_norm'])


def _mlp_kernel(q_ref, k_ref, va_ref, vb_ref, o_ref, *, tile):
    qi = pl.program_id(2)
    T = tile
    qa = q_ref[:, 0:HEAD_PAD]
    qb = q_ref[:, HEAD_PAD:2 * HEAD_PAD]
    row_i = lax.broadcasted_iota(jnp.int32, (T, T), 0)
    col_i = lax.broadcasted_iota(jnp.int32, (T, T), 1)
    mask = jnp.where(col_i <= row_i, 0.0, NEG_INF)
    mask2 = jnp.concatenate([mask, mask], axis=0)

    def tile_scores(start):
        kt = k_ref[pl.ds(start, T), :]
        va = va_ref[pl.ds(start, T), :]
        vb = vb_ref[pl.ds(start, T), :]
        s = jnp.concatenate([lax.dot_general(qa, kt[:, 0:HEAD_PAD], _NT, preferred_element_type=F32),
                             lax.dot_general(qb, kt[:, HEAD_PAD:2 * HEAD_PAD], _NT, preferred_element_type=F32)],
                            axis=0)

        def pv(p):
            return jnp.concatenate([jnp.dot(p[0:T], va, preferred_element_type=F32),
                                    jnp.dot(p[T:2 * T], vb, preferred_element_type=F32)], axis=0)
        return s, pv

    s, pv = tile_scores(pl.multiple_of(qi * T, T))
    carry = _softmax_first(s + mask2, pv)

    def far(ki, carry):
        s, pv = tile_scores(pl.multiple_of(ki * T, T))
        return _softmax_next(carry, s, pv)

    m, l, acc = lax.fori_loop(0, qi, far, carry)
    o = acc / l
    o_ref[...] = (o[0:T] + o[T:2 * T]).astype(BF16)


def _mlp(qm, km, va, vb, batch, seq):
    T = ATT_TILE
    nq = seq // T
    pw = 2 * HEAD_PAD
    vw = 2 * MLA_V
    return pl.pallas_call(
        functools.partial(_mlp_kernel, tile=T),
        grid=(batch, MLA_HEADS // 2, nq),
        in_specs=[
            pl.BlockSpec((T, pw), lambda b, p, i: (b * nq + i, p)),
            pl.BlockSpec((seq, pw), lambda b, p, i: (b, p)),
            pl.BlockSpec((seq, vw), lambda b, p, i: (b, p)),
            pl.BlockSpec((seq, vw), lambda b, p, i: (b, p)),
        ],
        out_specs=pl.BlockSpec((T, vw), lambda b, p, i: (b * nq + i, p)),
        out_shape=jax.ShapeDtypeStruct((batch * seq, MLA_HEADS * MLA_V), BF16),
        compiler_params=_params(("parallel", "parallel", "arbitrary")),
        name="mlp",
    )(qm, km, va, vb)


def _pad_rows(x, rows):
    return jnp.concatenate([x, jnp.zeros((rows - x.shape[0], x.shape[1]), x.dtype)], axis=0)


def _dfd_kernel(pt_ref, q_ref, kn_ref, vn_ref, bias_ref, lv_ref, hn_ref, *rest, pages, page, lam_init):
    k_refs = rest[0:pages]
    v_refs = rest[pages:2 * pages]
    o_ref, q_scr, m_scr, l_scr, acc_scr = rest[2 * pages:]
    pg = pl.program_id(1)
    last = pl.num_programs(1) - 1
    nrow = DF_HEADS * 2 * SAMPLE_PAD
    width = DF_HEADS * 2 * DF_DK

    @pl.when(pg == 0)
    def _():
        q = q_ref[...].astype(F32)
        qt = jnp.concatenate([q] * (DF_HEADS * 2), axis=0)
        rblk = lax.broadcasted_iota(jnp.int32, (nrow, width), 0) // SAMPLE_PAD
        cblk = lax.broadcasted_iota(jnp.int32, (nrow, width), 1) // DF_DK
        q_scr[...] = jnp.where(rblk == cblk, qt, 0.0).astype(BF16)
        m_scr[...] = jnp.full(m_scr.shape, NEG_INF, F32)
        l_scr[...] = jnp.zeros(l_scr.shape, F32)
        acc_scr[...] = jnp.zeros(acc_scr.shape, F32)

    qbd = q_scr[...]

    def step(carry, kt, vt, bias):
        s = lax.dot_general(qbd, kt.astype(BF16), _NT, preferred_element_type=F32) + bias
        vb = vt.astype(BF16)
        return _softmax_next(carry, s, lambda p: jnp.dot(p, vb, preferred_element_type=F32))

    carry = (m_scr[...], l_scr[...], acc_scr[...])
    for j in range(pages):
        bias = bias_ref[0]
        if j == pages - 1:
            bias = jnp.where(pg == last, bias_ref[1], bias)
        carry = step(carry, k_refs[j][...], v_refs[j][...], bias)
    m_scr[...], l_scr[...], acc_scr[...] = carry

    @pl.when(pg == last)
    def _():
        m, l, acc = step(carry, _pad_rows(kn_ref[...], page), _pad_rows(vn_ref[...], page), bias_ref[2])
        o = acc / l
        lam = _lambda(lv_ref, lam_init)
        for h in range(DF_HEADS):
            r0 = h * 2 * SAMPLE_PAD
            c = slice(h * DF_DV, (h + 1) * DF_DV)
            od = o[r0:r0 + SAMPLE_PAD, c] - lam * o[r0 + SAMPLE_PAD:r0 + 2 * SAMPLE_PAD, c]
            o_ref[:, c] = _rms(od, hn_ref[:, c]) * (1.0 - lam_init)


def _dfd(layer, dq, dk_new, dv_new, dbias, cache_k, cache_v, page_table, lw, lam_init):
    nb, n_pages = page_table.shape
    page = cache_k.shape[2]
    P = math.gcd(PAGES_PER_STEP, n_pages)
    width = DF_HEADS * 2 * DF_DK
    nrow = DF_HEADS * 2 * SAMPLE_PAD
    assert page == LANES

    def page_spec(j):
        return pl.BlockSpec((None, None, page, width), lambda b, g, pt: (layer, pt[b, g * P + j], 0, 0))

    row = pl.BlockSpec((SAMPLE_PAD, width), lambda b, g, pt: (b, 0))
    grid_spec = pltpu.PrefetchScalarGridSpec(
        num_scalar_prefetch=1,
        grid=(nb, n_pages // P),
        in_specs=[pl.BlockSpec((SAMPLE_PAD, width), lambda b, g, pt: (b, 0)),
                  row, row,
                  pl.BlockSpec((3, nrow, page), lambda b, g, pt: (0, 0, 0)),
                  pl.BlockSpec((4, DF_DK), lambda b, g, pt: (0, 0)),
                  pl.BlockSpec((1, width), lambda b, g, pt: (0, 0))]
                 + [page_spec(j) for j in range(P)] + [page_spec(j) for j in range(P)],
        out_specs=pl.BlockSpec((SAMPLE_PAD, width), lambda b, g, pt: (b, 0)),
        scratch_shapes=[pltpu.VMEM((nrow, width), BF16), pltpu.VMEM((nrow, 1), F32),
                        pltpu.VMEM((nrow, 1), F32), pltpu.VMEM((nrow, width), F32)],
    )
    return pl.pallas_call(
        functools.partial(_dfd_kernel, pages=P, page=page, lam_init=lam_init),
        grid_spec=grid_spec,
        out_shape=jax.ShapeDtypeStruct((nb * SAMPLE_PAD, width), F32),
        compiler_params=_params(("parallel", "arbitrary")),
        name="dfd",
    )(page_table, dq, dk_new, dv_new, dbias, lw['df_lambda'], lw['df_norm'],
      *([cache_k] * P), *([cache_v] * P))


def _mld_kernel(pt_ref, q_ref, cn_ref, rn_ref, mask_ref, wuk_ref, wuva_ref, wuvb_ref, *rest, pages, page):
    c_refs = rest[0:pages]
    r_refs = rest[pages:2 * pages]
    o_ref, ql_scr, qr_scr, m_scr, l_scr, acc_scr = rest[2 * pages:]
    pg = pl.program_id(1)
    last = pl.num_programs(1) - 1

    @pl.when(pg == 0)
    def _():
        for h in range(MLA_HEADS):
            sl = slice(h * HEAD_PAD, (h + 1) * HEAD_PAD)
            qh = q_ref[:, sl].astype(BF16)
            rows = slice(h * SAMPLE_PAD, (h + 1) * SAMPLE_PAD)
            ql_scr[rows, :] = lax.dot_general(qh, wuk_ref[:, sl], _NT, preferred_element_type=F32)
            qr_scr[rows, :] = q_ref[:, h * HEAD_PAD + MLA_NOPE:h * HEAD_PAD + MLA_NOPE + MLA_ROPE]
        m_scr[...] = jnp.full(m_scr.shape, NEG_INF, F32)
        l_scr[...] = jnp.zeros(l_scr.shape, F32)
        acc_scr[...] = jnp.zeros(acc_scr.shape, F32)

    ql = ql_scr[...].astype(BF16)
    qr = qr_scr[...].astype(BF16)

    def step(carry, ct, rt, mask):
        cb = ct.astype(BF16)
        s = (lax.dot_general(ql, cb, _NT, preferred_element_type=F32)
             + lax.dot_general(qr, rt.astype(BF16), _NT, preferred_element_type=F32))
        if mask is not None:
            s = s + mask
        return _softmax_next(carry, s, lambda p: jnp.dot(p, cb, preferred_element_type=F32))

    carry = (m_scr[...], l_scr[...], acc_scr[...])
    for j in range(pages):
        carry = step(carry, c_refs[j][...], r_refs[j][...], None)
    m_scr[...], l_scr[...], acc_scr[...] = carry

    @pl.when(pg == last)
    def _():
        m, l, acc = step(carry, _pad_rows(cn_ref[...], page), _pad_rows(rn_ref[...], page), mask_ref[...])
        o = (acc / l).astype(BF16)
        for p in range(MLA_HEADS // 2):
            c = slice(p * 2 * MLA_V, (p + 1) * 2 * MLA_V)
            ra = slice(2 * p * SAMPLE_PAD, (2 * p + 1) * SAMPLE_PAD)
            rb = slice((2 * p + 1) * SAMPLE_PAD, (2 * p + 2) * SAMPLE_PAD)
            y = (jnp.dot(o[ra], wuva_ref[:, c], preferred_element_type=F32)
                 + jnp.dot(o[rb], wuvb_ref[:, c], preferred_element_type=F32))
            o_ref[:, c] = y


def _mld(layer, qm, ckv_new, kr_new, new_mask, cache_c, cache_r, page_table, lw):
    nb, n_pages = page_table.shape
    page = cache_c.shape[2]
    P = math.gcd(PAGES_PER_STEP, n_pages)
    nrow = MLA_HEADS * SAMPLE_PAD
    qw = MLA_HEADS * HEAD_PAD
    ow = MLA_HEADS * MLA_V

    def page_spec(width, j):
        return pl.BlockSpec((None, None, page, width), lambda b, g, pt: (layer, pt[b, g * P + j], 0, 0))

    const = lambda shape: pl.BlockSpec(shape, lambda b, g, pt: (0,) * len(shape))
    grid_spec = pltpu.PrefetchScalarGridSpec(
        num_scalar_prefetch=1,
        grid=(nb, n_pages // P),
        in_specs=[pl.BlockSpec((SAMPLE_PAD, qw), lambda b, g, pt: (b, 0)),
                  pl.BlockSpec((SAMPLE_PAD, MLA_KV_LORA), lambda b, g, pt: (b, 0)),
                  pl.BlockSpec((SAMPLE_PAD, MLA_ROPE), lambda b, g, pt: (b, 0)),
                  const((nrow, page)), const((MLA_KV_LORA, qw)), const((MLA_KV_LORA, ow)),
                  const((MLA_KV_LORA, ow))]
                 + [page_spec(MLA_KV_LORA, j) for j in range(P)] + [page_spec(MLA_ROPE, j) for j in range(P)],
        out_specs=pl.BlockSpec((SAMPLE_PAD, ow), lambda b, g, pt: (b, 0)),
        scratch_shapes=[pltpu.VMEM((nrow, MLA_KV_LORA), F32), pltpu.VMEM((nrow, MLA_ROPE), F32),
                        pltpu.VMEM((nrow, 1), F32), pltpu.VMEM((nrow, 1), F32),
                        pltpu.VMEM((nrow, MLA_KV_LORA), F32)],
    )
    return pl.pallas_call(
        functools.partial(_mld_kernel, pages=P, page=page),
        grid_spec=grid_spec,
        out_shape=jax.ShapeDtypeStruct((nb * SAMPLE_PAD, ow), F32),
        compiler_params=_params(("parallel", "arbitrary")),
        name="mld",
    )(page_table, qm, ckv_new, kr_new, new_mask, lw['wuk'], lw['wuva'], lw['wuvb'],
      *([cache_c] * P), *([cache_r] * P))


def _post_kernel(x_ref, ym_ref, yd_ref, yc_ref, gate_ref, g_ref, wbm, wbd, wbc, wout, wfi, wfo, o_ref):
    u = None
    for j, (y_ref, w_ref) in enumerate(((ym_ref, wbm), (yd_ref, wbd), (yc_ref, wbc))):
        t = gate_ref[:, j * D_MODEL:(j + 1) * D_MODEL].astype(F32) * jnp.dot(
            y_ref[...].astype(BF16), w_ref[...], preferred_element_type=F32)
        u = t if u is None else u + t
    x = x_ref[...] + _rms(jnp.dot(u.astype(BF16), wout[...], preferred_element_type=F32), g_ref[0:1])
    hf = _rms(x, g_ref[1:2]).astype(BF16)
    cw = D_FF // 2
    t = None
    for j in range(2):
        a = jnp.dot(hf, wfi[:, j * cw:(j + 1) * cw], preferred_element_type=F32)
        bb = jnp.dot(hf, wfi[:, D_FF + j * cw:D_FF + (j + 1) * cw], preferred_element_type=F32)
        s = (a * jax.nn.sigmoid(a) * bb).astype(BF16)
        d = jnp.dot(s, wfo[j * cw:(j + 1) * cw, :], preferred_element_type=F32)
        t = d if t is None else t + d
    o_ref[...] = x + _rms(t, g_ref[2:3])


def _post(x, ym, yd, yc, gates, lw):
    rows = x.shape[0]
    tm = min(ROW_TILE, rows)
    row = lambda width: pl.BlockSpec((tm, width), lambda i: (i, 0))
    weights = [lw['g123'], lw['w_br_ml'], lw['w_br_df'], lw['w_br_mla'], lw['w_out'], lw['w_ffn_in'], lw['w_ffn_out']]
    return pl.pallas_call(
        _post_kernel,
        grid=(rows // tm,),
        in_specs=[row(D_MODEL), row(512), row(512), row(512), row(N_BRANCH * D_MODEL)]
                 + [_const_spec(w.shape) for w in weights],
        out_specs=row(D_MODEL),
        out_shape=jax.ShapeDtypeStruct((rows, D_MODEL), F32),
        compiler_params=_params(("parallel",)),
        name="post",
    )(x, ym, yd, yc, gates, *weights)


def _rot_half_cols(w):
    half = w.shape[-1] // 2
    return jnp.concatenate([-w[..., half:], w[..., :half]], axis=-1)


def _head_pad(nope, rope):
    k, h = nope.shape[0], nope.shape[1]
    pad = jnp.zeros((k, h, HEAD_PAD - MLA_NOPE - MLA_ROPE), nope.dtype)
    return jnp.concatenate([nope, rope, pad], axis=-1).reshape(k, h * HEAD_PAD)


def _layer_weights(l, norm_gains, w_in, b_ml_gates, ml_head_norm, df_lambda, df_head_norm, mla_q_norm,
                   mla_kv_norm, w_uq, w_uk, w_uv, w_br_ml, w_br_df, w_br_mla, w_out, w_ffn_in, w_ffn_out):
    offs = np.cumsum((0,) + IN_SIZES)
    col = lambda i: w_in[l][:, offs[i]:offs[i + 1]]
    mq, mk, mv, mo, mi, mf, dq, dk, dv, cq, ckv, kr, gl = (col(i) for i in range(len(IN_SIZES)))
    zeros = lambda n: jnp.zeros((D_MODEL, n), F32)
    rope_grp = lambda w: jnp.concatenate([zeros(MLA_NOPE), w, zeros(HEAD_PAD - MLA_NOPE - MLA_ROPE)], axis=1)
    wmisc = jnp.concatenate([mi, mf, zeros(LANES - 2 * ML_HEADS), rope_grp(kr), rope_grp(_rot_half_cols(kr))], axis=1)
    uq = w_uq[l].reshape(MLA_Q_LORA, MLA_HEADS, MLA_NOPE + MLA_ROPE)
    uq_n, uq_r = uq[..., :MLA_NOPE], uq[..., MLA_NOPE:]
    uv = w_uv[l]
    zv = jnp.zeros_like(uv[:, 0::2])
    bf = lambda a: a.astype(BF16)
    return {
        'g0': norm_gains[l, 0:1], 'g123': norm_gains[l, 1:4],
        'wml': bf(jnp.concatenate([mq, mk, mv, mo], axis=1)), 'wdf': bf(jnp.concatenate([dq, dk, dv], axis=1)),
        'wcq': bf(cq), 'wckv': bf(ckv), 'wmisc': bf(wmisc), 'wgl': bf(gl),
        'q_norm': mla_q_norm[l][None], 'kv_norm': mla_kv_norm[l][None],
        'wuqa': bf(_head_pad(uq_n, uq_r)), 'wuqb': bf(_head_pad(jnp.zeros_like(uq_n), _rot_half_cols(uq_r))),
        'wuk': bf(_head_pad(w_uk[l], jnp.zeros((MLA_KV_LORA, MLA_HEADS, MLA_ROPE), F32))),
        'wuva': bf(jnp.concatenate([uv[:, 0::2], zv], axis=-1).reshape(MLA_KV_LORA, MLA_HEADS * MLA_V)),
        'wuvb': bf(jnp.concatenate([zv, uv[:, 1::2]], axis=-1).reshape(MLA_KV_LORA, MLA_HEADS * MLA_V)),
        'gate_bias': jnp.concatenate([b_ml_gates[l], jnp.zeros((LANES - 2 * ML_HEADS,), F32)])[None],
        'ml_norm': ml_head_norm[l][None], 'df_lambda': df_lambda[l], 'df_norm': df_head_norm[l][None],
        'w_br_ml': bf(w_br_ml[l]), 'w_br_df': bf(w_br_df[l]), 'w_br_mla': bf(w_br_mla[l]),
        'w_out': bf(w_out[l]), 'w_ffn_in': bf(w_ffn_in[l]), 'w_ffn_out': bf(w_ffn_out[l]),
    }


def _rope_tables(pos):
    half = MLA_ROPE // 2
    freqs = ROPE_THETA ** (-jnp.arange(half, dtype=F32) / half)
    ang = pos.astype(F32)[:, None] * freqs[None, :]
    cos, sin = jnp.cos(ang), jnp.sin(ang)
    n = pos.shape[0]
    tail = jnp.zeros((n, HEAD_PAD - MLA_NOPE - MLA_ROPE), F32)
    cs = jnp.concatenate([jnp.ones((n, MLA_NOPE), F32), cos, cos, tail], axis=1)
    sn = jnp.concatenate([jnp.zeros((n, MLA_NOPE), F32), sin, sin, tail], axis=1)
    return cs, sn


def kernel(x_prompt, x_sample, state_mlstm_C, state_mlstm_n, state_mlstm_m, cache_diff_k, cache_diff_v,
           cache_mla_ckv, cache_mla_krope, page_table, norm_gains, w_in, b_ml_gates, ml_head_norm, df_lambda,
           df_head_norm, rel_bias, mla_q_norm, mla_kv_norm, w_uq, w_uk, w_uv, w_br_ml, w_br_df, w_br_mla, w_out,
           w_ffn_in, w_ffn_out):
    depth = w_in.shape[0]
    B, S, _ = x_prompt.shape
    DB, DS, _ = x_sample.shape
    n_pages = page_table.shape[1]
    n_pool, page = cache_diff_k.shape[1], cache_diff_k.shape[2]
    past_len = n_pages * page
    assert DS <= SAMPLE_PAD
    T = ATT_TILE

    cs_p, sn_p = _rope_tables(jnp.tile(jnp.arange(S, dtype=jnp.int32), B))
    pos_s = past_len + jnp.arange(SAMPLE_PAD, dtype=jnp.int32)
    cs_s, sn_s = _rope_tables(jnp.tile(pos_s, DB))
    ii, jj = np.meshgrid(np.arange(T), np.arange(T), indexing='ij')
    diag, sub = _t5_bucket_np(ii - jj), _t5_bucket_np(T + ii - jj)
    bias_p = _bias_table(rel_bias, np.concatenate([diag, diag, sub, sub], axis=0)).reshape(DF_HEADS, 2, 2 * T, T)
    rr, kk = np.meshgrid(np.arange(SAMPLE_PAD), np.arange(page), indexing='ij')
    new_ok = (kk <= rr) & (kk < DS)
    idx_d = np.concatenate([np.full((SAMPLE_PAD, page), REL_BUCKETS - 1, np.int32),
                            _t5_bucket_np(page + rr - kk),
                            np.where(new_ok, _t5_bucket_np(rr - kk), -1)], axis=0)
    bias_d = _bias_table(rel_bias, idx_d).reshape(DF_HEADS, 3, 1, SAMPLE_PAD, page)
    bias_d = jnp.broadcast_to(bias_d, (DF_HEADS, 3, 2, SAMPLE_PAD, page))
    bias_d = jnp.transpose(bias_d, (1, 0, 2, 3, 4)).reshape(3, DF_HEADS * 2 * SAMPLE_PAD, page)
    mla_new_mask = jnp.asarray(np.tile(np.where(new_ok, 0.0, NEG_INF).astype(np.float32), (MLA_HEADS, 1)))

    ck = cache_diff_k.reshape(depth, n_pool, page, DF_HEADS * 2 * DF_DK)
    cv = cache_diff_v.reshape(depth, n_pool, page, DF_HEADS * DF_DV)

    xp = x_prompt.reshape(B * S, D_MODEL)
    xs = jnp.pad(x_sample, ((0, 0), (0, SAMPLE_PAD - DS), (0, 0))).reshape(DB * SAMPLE_PAD, D_MODEL)
    zero_c = jnp.zeros((B, ML_HEADS, ML_DK, ML_DV), F32)
    zero_n = jnp.zeros((B, ML_HEADS, ML_DK), F32)
    zero_m = jnp.zeros((B, ML_HEADS), F32)
    st_p, st_s = [], []
    for l in range(depth):
        lw = _layer_weights(l, norm_gains, w_in, b_ml_gates, ml_head_norm, df_lambda, df_head_norm, mla_q_norm,
                            mla_kv_norm, w_uq, w_uk, w_uv, w_br_ml, w_br_df, w_br_mla, w_out, w_ffn_in, w_ffn_out)
        lam_init = 0.8 - 0.6 * math.exp(-0.3 * l)

        (ml, gif, dq, dk, dv, qm, ckv, kr, gates, dkb, dvb, km, va, vb) = _proj(xp, lw, cs_p, sn_p, True)
        ym, c_p, n_p, m_p = _mlstm(ml, gif, lw, zero_c, zero_n, zero_m, B, S, math.gcd(S, ML_CHUNK), S)
        yd = _dfp(dq, dkb, dvb, bias_p, rel_bias, lw, B, S, lam_init)
        yc = _mlp(qm, km, va, vb, B, S)
        xp = _post(xp, ym, yd, yc, gates, lw)
        st_p.append((dk.reshape(B, S, DF_HEADS, 2, DF_DK), dv.reshape(B, S, DF_HEADS, DF_DV),
                     ckv.reshape(B, S, MLA_KV_LORA), kr.reshape(B, S, MLA_ROPE), c_p, n_p, m_p))

        (ml, gif, dq, dk, dv, qm, ckv, kr, gates) = _proj(xs, lw, cs_s, sn_s, False)
        ym, c_s, n_s, m_s = _mlstm(ml, gif, lw, state_mlstm_C[l], state_mlstm_n[l], state_mlstm_m[l],
                                   DB, SAMPLE_PAD, SAMPLE_PAD, DS)
        yd = _dfd(l, dq, dk, dv, bias_d, ck, cv, page_table, lw, lam_init)
        yc = _mld(l, qm, ckv, kr, mla_new_mask, cache_mla_ckv, cache_mla_krope, page_table, lw)
        xs = _post(xs, ym, yd, yc, gates, lw)
        tok = lambda a, *tail: a.reshape((DB, SAMPLE_PAD) + tail)[:, :DS]
        st_s.append((tok(dk, DF_HEADS, 2, DF_DK), tok(dv, DF_HEADS, DF_DV), tok(ckv, MLA_KV_LORA),
                     tok(kr, MLA_ROPE), c_s, n_s, m_s))

    outs_p = [jnp.stack(a) for a in zip(*st_p)]
    outs_s = [jnp.stack(a) for a in zip(*st_s)]
    yp = xp.reshape(B, S, D_MODEL)
    ys = xs.reshape(DB, SAMPLE_PAD, D_MODEL)[:, :DS]
    return (yp, ys, *outs_p, *outs_s)
```

```python
import functools
import math

import numpy as np
import jax
import jax.numpy as jnp
from jax import lax
from jax.experimental import pallas as pl
from jax.experimental.pallas import tpu as pltpu

F32 = jnp.float32
BF16 = jnp.bfloat16

D_MODEL = 1024
ML_HEADS = 4
ML_DK = 128
ML_DV = 128
DF_HEADS = 4
DF_DK = 64
DF_DV = 2 * DF_DK
MLA_HEADS = 8
MLA_NOPE = 64
MLA_ROPE = 32
MLA_V = 64
MLA_Q_LORA = 384
MLA_KV_LORA = 256
ROPE_THETA = 10000.0
REL_BUCKETS = 32
REL_MAX_DIST = 128
N_BRANCH = 3
D_FF = 2816
EPS = 1e-6
NEG_INF = -1e30
IN_SIZES = (ML_HEADS * ML_DK, ML_HEADS * ML_DK, ML_HEADS * ML_DV, ML_HEADS * ML_DV, ML_HEADS, ML_HEADS,
            DF_HEADS * 2 * DF_DK, DF_HEADS * 2 * DF_DK, DF_HEADS * DF_DV,
            MLA_Q_LORA, MLA_KV_LORA, MLA_ROPE,
            N_BRANCH * D_MODEL)

LANES = 128
SUBLANES = 8
HEAD_PAD = 128
ROW_TILE = 256
ATT_TILE = 256
ML_CHUNK = 256
SAMPLE_PAD = 8
DFD_PAGES = 16
MLD_PAGES = 32
VMEM_LIMIT = 56 * 1024 * 1024

_NT = (((1,), (1,)), ((), ()))
_TN = (((0,), (0,)), ((), ()))


def _params(sem):
    return pltpu.CompilerParams(dimension_semantics=sem, vmem_limit_bytes=VMEM_LIMIT)


def _const_spec(shape):
    nd = len(shape)
    return pl.BlockSpec(shape, lambda *_: (0,) * nd, pipeline_mode=pl.Buffered(1))


def _rms(x, g):
    return x * lax.rsqrt(jnp.mean(x * x, axis=-1, keepdims=True) + EPS) * g


def _t5_bucket_np(n):
    n = np.asarray(n, np.int64)
    exact = REL_BUCKETS // 2
    nf = np.maximum(n, 1).astype(np.float32)
    large = exact + (np.log(nf / np.float32(exact)) / np.float32(math.log(REL_MAX_DIST / exact))
                     * np.float32(REL_BUCKETS - exact)).astype(np.int32)
    large = np.minimum(large, REL_BUCKETS - 1)
    b = np.where(n < exact, n, large)
    return np.where(n < 0, -1, b).astype(np.int32)


def _bias_table_kernel(rb_ref, idx_ref, out_ref):
    idx = idx_ref[...]
    for h in range(DF_HEADS):
        acc = jnp.full(idx.shape, NEG_INF, F32)
        for b in range(REL_BUCKETS):
            acc = jnp.where(idx == b, rb_ref[b, h], acc)
        out_ref[h] = acc


def _bias_table(rel_bias, idx_np):
    r, c = idx_np.shape
    return pl.pallas_call(
        _bias_table_kernel,
        out_shape=jax.ShapeDtypeStruct((DF_HEADS, r, c), F32),
        in_specs=[pl.BlockSpec(memory_space=pltpu.SMEM), pl.BlockSpec(memory_space=pltpu.VMEM)],
        out_specs=pl.BlockSpec(memory_space=pltpu.VMEM),
        name="bias_table",
    )(rel_bias, jnp.asarray(idx_np))


def _proj_kernel(x_ref, g_ref, cs_ref, sn_ref, wml, wdf, wcq, wckv, wmisc, wgl, qn_ref, kvn_ref,
                 wuqa, wuqb, wuk, wuva, wuvb,
                 ml_o, gif_o, dq_o, dk_o, dv_o, qm_o, ckv_o, kr_o, gate_o, *prompt_outs, prompt):
    x = x_ref[...]
    h = _rms(x, g_ref[...]).astype(BF16)

    def mm(w_ref, lo, hi):
        return jnp.dot(h, w_ref[:, lo:hi], preferred_element_type=F32)

    w = ML_HEADS * ML_DK
    act = ml_o.dtype
    ml_o[:, 0:w] = mm(wml, 0, w).astype(act)
    ml_o[:, w:2 * w] = (mm(wml, w, 2 * w) * ML_DK ** -0.5).astype(act)
    ml_o[:, 2 * w:3 * w] = mm(wml, 2 * w, 3 * w).astype(act)
    ml_o[:, 3 * w:4 * w] = jax.nn.sigmoid(mm(wml, 3 * w, 4 * w)).astype(act)

    misc = mm(wmisc, 0, 3 * LANES)
    gif_o[...] = misc[:, 0:LANES]
    cs = cs_ref[...]
    sn = sn_ref[...]
    krp = misc[:, LANES:2 * LANES] * cs + misc[:, 2 * LANES:3 * LANES] * sn
    kr_o[...] = krp[:, MLA_NOPE:MLA_NOPE + MLA_ROPE]

    wd = DF_HEADS * 2 * DF_DK
    dq_o[...] = (mm(wdf, 0, wd) * DF_DK ** -0.5).astype(act)
    dk = mm(wdf, wd, 2 * wd)
    dv = mm(wdf, 2 * wd, 3 * wd)
    dk_o[...] = dk
    dv_o[...] = dv

    c_q = _rms(mm(wcq, 0, MLA_Q_LORA), qn_ref[...]).astype(BF16)
    c_kv = _rms(mm(wckv, 0, MLA_KV_LORA), kvn_ref[...])
    ckv_o[...] = c_kv
    scale = (MLA_NOPE + MLA_ROPE) ** -0.5
    for hh in range(MLA_HEADS):
        sl = slice(hh * HEAD_PAD, (hh + 1) * HEAD_PAD)
        qa = jnp.dot(c_q, wuqa[:, sl], preferred_element_type=F32)
        qb = jnp.dot(c_q, wuqb[:, sl], preferred_element_type=F32)
        qm_o[:, sl] = ((qa * cs + qb * sn) * scale).astype(act)

    gw = 512
    for j in range(N_BRANCH * D_MODEL // gw):
        gate_o[:, j * gw:(j + 1) * gw] = jax.nn.sigmoid(mm(wgl, j * gw, (j + 1) * gw)).astype(BF16)

    if prompt:
        dkb_o, dvb_o, km_o, va_o, vb_o = prompt_outs
        dkb_o[...] = dk.astype(BF16)
        dvb_o[...] = dv.astype(BF16)
        ckb = c_kv.astype(BF16)
        for hh in range(MLA_HEADS):
            sl = slice(hh * HEAD_PAD, (hh + 1) * HEAD_PAD)
            kn = jnp.dot(ckb, wuk[:, sl], preferred_element_type=F32)
            km_o[:, sl] = (kn + krp).astype(BF16)
        va_o[...] = jnp.dot(ckb, wuva[...], preferred_element_type=F32).astype(BF16)
        vb_o[...] = jnp.dot(ckb, wuvb[...], preferred_element_type=F32).astype(BF16)


def _proj(x, lw, cs, sn, prompt):
    rows = x.shape[0]
    tm = min(ROW_TILE, rows)
    assert rows % tm == 0
    row = lambda width: pl.BlockSpec((tm, width), lambda i: (i, 0))
    weights = [lw['wml'], lw['wdf'], lw['wcq'], lw['wckv'], lw['wmisc'], lw['wgl'], lw['q_norm'], lw['kv_norm'],
               lw['wuqa'], lw['wuqb'], lw['wuk'], lw['wuva'], lw['wuvb']]
    act = BF16 if prompt else F32
    out_widths = [(4 * ML_HEADS * ML_DK, act), (LANES, F32), (512, act), (512, F32), (512, F32),
                  (MLA_HEADS * HEAD_PAD, act), (MLA_KV_LORA, F32), (MLA_ROPE, F32), (N_BRANCH * D_MODEL, BF16)]
    if prompt:
        out_widths += [(512, BF16), (512, BF16), (MLA_HEADS * HEAD_PAD, BF16), (512, BF16), (512, BF16)]
    return pl.pallas_call(
        functools.partial(_proj_kernel, prompt=prompt),
        grid=(rows // tm,),
        in_specs=[row(D_MODEL), _const_spec((1, D_MODEL)), row(LANES), row(LANES)]
                 + [_const_spec(w.shape) for w in weights],
        out_specs=[row(wd) for wd, _ in out_widths],
        out_shape=[jax.ShapeDtypeStruct((rows, wd), dt) for wd, dt in out_widths],
        compiler_params=_params(("parallel",)),
        name="proj_prompt" if prompt else "proj_sample",
    )(x, lw['g0'], cs, sn, *weights)


def _mlstm_kernel(m0_ref, ml_ref, gif_ref, gb_ref, hn_ref, c0_ref, n0_ref,
                  ym_o, c_o, n_o, m_o, cn_scr, m_scr, *, chunk, t_valid):
    b = pl.program_id(0)
    c = pl.program_id(1)
    L = chunk
    dk, dv = ML_DK, ML_DV
    lane = lax.broadcasted_iota(jnp.int32, (1, LANES), 1)

    @pl.when(c == 0)
    def _():
        for h in range(ML_HEADS):
            ncol = jnp.where(lax.broadcasted_iota(jnp.int32, (dk, LANES), 1) == 0, n0_ref[0, h], 0.0)
            cn_scr[h] = jnp.concatenate([c0_ref[0, h], ncol], axis=1)
            m_scr[h] = jnp.full((SUBLANES, LANES), m0_ref[b, h], F32)

    g = gif_ref[...] + gb_ref[...]
    g = jnp.where(lane < ML_HEADS, g, jnp.minimum(g, 0.0) - jnp.log(1.0 + jnp.exp(-jnp.abs(g))))
    row_i = lax.broadcasted_iota(jnp.int32, (L, L), 0)
    col_i = lax.broadcasted_iota(jnp.int32, (L, L), 1)
    if t_valid < L:
        tok = lax.broadcasted_iota(jnp.int32, (L, LANES), 0)
        g = jnp.where(tok < t_valid, g, jnp.where(lane < ML_HEADS, NEG_INF, 0.0))
    causal = col_i <= row_i
    eye = col_i == row_i

    for h in range(ML_HEADS):
        lic = g[:, h:h + 1]
        lfc = g[:, ML_HEADS + h:ML_HEADS + h + 1]
        br = jnp.sum(jnp.where(row_i <= col_i, lfc, 0.0), axis=0, keepdims=True)
        bc = jnp.sum(jnp.where(eye, br, 0.0), axis=1, keepdims=True)
        ir = jnp.sum(jnp.where(eye, lic, 0.0), axis=0, keepdims=True)
        b_last = jnp.sum(lfc, axis=0, keepdims=True)
        m_prev = m_scr[h][0:1, 0:1]

        q = ml_ref[:, h * dk:(h + 1) * dk].astype(BF16)
        k = ml_ref[:, (ML_HEADS + h) * dk:(ML_HEADS + h + 1) * dk].astype(BF16)
        v = ml_ref[:, (2 * ML_HEADS + h) * dk:(2 * ML_HEADS + h + 1) * dk].astype(BF16)
        og = ml_ref[:, (3 * ML_HEADS + h) * dk:(3 * ML_HEADS + h + 1) * dk]

        dmat = jnp.where(causal, bc - br + ir, NEG_INF)
        inter = bc + m_prev
        mt = jnp.maximum(inter, jnp.max(dmat, axis=1, keepdims=True))
        wgt = jnp.exp(dmat - mt)
        iw = jnp.exp(inter - mt)
        a = lax.dot_general(q, k, _NT, preferred_element_type=F32) * wgt
        cn = cn_scr[h]
        qc = jnp.dot(q, cn.astype(BF16), preferred_element_type=F32)
        num = jnp.dot(a.astype(BF16), v, preferred_element_type=F32) + iw * qc[:, 0:dv]
        den = jnp.sum(a, axis=1, keepdims=True) + iw * qc[:, dv:dv + 1]
        hh = num / jnp.maximum(jnp.abs(den), jnp.exp(-mt))
        y = _rms(hh, hn_ref[:, h * dv:(h + 1) * dv]) * og.astype(F32)
        ym_o[:, h * dv:(h + 1) * dv] = y.astype(ym_o.dtype)

        m_new = jnp.maximum(b_last + m_prev, jnp.max(b_last - br + ir, axis=1, keepdims=True))
        wl = jnp.exp(b_last - bc + lic - m_new)
        dec = jnp.exp(b_last + m_prev - m_new)
        ones_col = jnp.where(lax.broadcasted_iota(jnp.int32, (L, LANES), 1) == 0, wl, 0.0)
        wv = jnp.concatenate([wl * v.astype(F32), ones_col], axis=1).astype(BF16)
        cn_new = dec * cn + lax.dot_general(k, wv, _TN, preferred_element_type=F32)
        cn_scr[h] = cn_new
        m_scr[h] = jnp.broadcast_to(m_new, (SUBLANES, LANES))
        c_o[0, h] = cn_new[:, 0:dv]
        n_o[0, h] = cn_new[:, dv:dv + 1]
        m_o[0, h] = jnp.broadcast_to(m_new, (SUBLANES, LANES))


def _mlstm(ml, gif, lw, c0, n0, m0, batch, seq, chunk, t_valid):
    nc = seq // chunk
    assert seq % chunk == 0
    width = ML_HEADS * ML_DV
    grid_spec = pltpu.PrefetchScalarGridSpec(
        num_scalar_prefetch=1,
        grid=(batch, nc),
        in_specs=[
            pl.BlockSpec((chunk, 4 * width), lambda b, c, m: (b * nc + c, 0)),
            pl.BlockSpec((chunk, LANES), lambda b, c, m: (b * nc + c, 0)),
            pl.BlockSpec((1, LANES), lambda b, c, m: (0, 0)),
            pl.BlockSpec((1, width), lambda b, c, m: (0, 0)),
            pl.BlockSpec((1, ML_HEADS, ML_DK, ML_DV), lambda b, c, m: (b, 0, 0, 0)),
            pl.BlockSpec((1, ML_HEADS, ML_DK, 1), lambda b, c, m: (b, 0, 0, 0)),
        ],
        out_specs=[
            pl.BlockSpec((chunk, width), lambda b, c, m: (b * nc + c, 0)),
            pl.BlockSpec((1, ML_HEADS, ML_DK, ML_DV), lambda b, c, m: (b, 0, 0, 0)),
            pl.BlockSpec((1, ML_HEADS, ML_DK, 1), lambda b, c, m: (b, 0, 0, 0)),
            pl.BlockSpec((1, ML_HEADS, SUBLANES, LANES), lambda b, c, m: (b, 0, 0, 0)),
        ],
        scratch_shapes=[pltpu.VMEM((ML_HEADS, ML_DK, 2 * ML_DV), F32),
                        pltpu.VMEM((ML_HEADS, SUBLANES, LANES), F32)],
    )
    ym, c_new, n_new, m_new = pl.pallas_call(
        functools.partial(_mlstm_kernel, chunk=chunk, t_valid=t_valid),
        grid_spec=grid_spec,
        out_shape=[jax.ShapeDtypeStruct((batch * seq, width), ml.dtype),
                   jax.ShapeDtypeStruct((batch, ML_HEADS, ML_DK, ML_DV), F32),
                   jax.ShapeDtypeStruct((batch, ML_HEADS, ML_DK, 1), F32),
                   jax.ShapeDtypeStruct((batch, ML_HEADS, SUBLANES, LANES), F32)],
        compiler_params=_params(("parallel", "arbitrary")),
        name="mlstm",
    )(m0, ml, gif, lw['gate_bias'], lw['ml_norm'], c0, n0[..., None])
    return ym, c_new, n_new[..., 0], m_new[:, :, 0, 0]


def _softmax_first(s, pv):
    m = jnp.max(s, axis=1, keepdims=True)
    p = jnp.exp(s - m)
    return m, jnp.sum(p, axis=1, keepdims=True), pv(p.astype(BF16))


def _softmax_next(carry, s, pv):
    m, l, acc = carry
    m_new = jnp.maximum(m, jnp.max(s, axis=1, keepdims=True))
    alpha = jnp.exp(m - m_new)
    p = jnp.exp(s - m_new)
    return m_new, alpha * l + jnp.sum(p, axis=1, keepdims=True), alpha * acc + pv(p.astype(BF16))


def _lambda(lv_ref, lam_init):
    lv = lv_ref[...]
    e1 = jnp.exp(jnp.sum(lv[0:1] * lv[1:2], axis=1, keepdims=True))
    e2 = jnp.exp(jnp.sum(lv[2:3] * lv[3:4], axis=1, keepdims=True))
    return e1 - e2 + lam_init


def _dfp_kernel(rb_ref, q_ref, k_ref, v_ref, bias_ref, lv_ref, hn_ref, o_ref, *, tile, lam_init):
    h = pl.program_id(1)
    qi = pl.program_id(2)
    T = tile
    q = q_ref[...]
    lane = lax.broadcasted_iota(jnp.int32, (T, 2 * DF_DK), 1)
    zero = jnp.zeros_like(q)
    q2 = jnp.concatenate([jnp.where(lane < DF_DK, q, zero), jnp.where(lane >= DF_DK, q, zero)], axis=0)

    def tile_scores(start):
        kt = k_ref[pl.ds(start, T), :]
        vt = v_ref[pl.ds(start, T), :]
        s = lax.dot_general(q2, kt, _NT, preferred_element_type=F32)
        return s, lambda p: jnp.dot(p, vt, preferred_element_type=F32)

    s, pv = tile_scores(pl.multiple_of(qi * T, T))
    carry = _softmax_first(s + bias_ref[0, 0], pv)
    sub = jnp.maximum(qi - 1, 0)
    s, pv = tile_scores(pl.multiple_of(sub * T, T))
    carry = _softmax_next(carry, s + bias_ref[0, 1] + jnp.where(qi == 0, NEG_INF, 0.0), pv)
    far_bias = rb_ref[REL_BUCKETS - 1, h]

    def far(ki, carry):
        s, pv = tile_scores(pl.multiple_of(ki * T, T))
        return _softmax_next(carry, s + far_bias, pv)

    m, l, acc = lax.fori_loop(0, sub, far, carry)
    o = acc / l
    lam = _lambda(lv_ref, lam_init)
    od = o[0:T] - lam * o[T:2 * T]
    o_ref[...] = (_rms(od, hn_ref[...]) * (1.0 - lam_init)).astype(BF16)


def _dfp(dq, dkb, dvb, bias_tiles, rel_bias, lw, batch, seq, lam_init):
    T = ATT_TILE
    nq = seq // T
    assert seq % T == 0 and T >= REL_MAX_DIST
    hw = 2 * DF_DK
    grid_spec = pltpu.PrefetchScalarGridSpec(
        num_scalar_prefetch=0,
        grid=(batch, DF_HEADS, nq),
        in_specs=[
            pl.BlockSpec(memory_space=pltpu.SMEM),
            pl.BlockSpec((T, hw), lambda b, h, i: (b * nq + i, h)),
            pl.BlockSpec((seq, hw), lambda b, h, i: (b, h)),
            pl.BlockSpec((seq, hw), lambda b, h, i: (b, h)),
            pl.BlockSpec((1, 2, 2 * T, T), lambda b, h, i: (h, 0, 0, 0)),
            pl.BlockSpec((4, DF_DK), lambda b, h, i: (0, 0)),
            pl.BlockSpec((1, hw), lambda b, h, i: (0, h)),
        ],
        out_specs=pl.BlockSpec((T, hw), lambda b, h, i: (b * nq + i, h)),
    )
    return pl.pallas_call(
        functools.partial(_dfp_kernel, tile=T, lam_init=lam_init),
        grid_spec=grid_spec,
        out_shape=jax.ShapeDtypeStruct((batch * seq, DF_HEADS * hw), BF16),
        compiler_params=_params(("parallel", "parallel", "arbitrary")),
        name="dfp",
    )(rel_bias, dq, dkb, dvb, bias_tiles, lw['df_lambda'], lw['df_norm'])


def _mlp_kernel(q_ref, k_ref, va_ref, vb_ref, o_ref, *, tile):
    qi = pl.program_id(2)
    T = tile
    qa = q_ref[:, 0:HEAD_PAD]
    qb = q_ref[:, HEAD_PAD:2 * HEAD_PAD]
    row_i = lax.broadcasted_iota(jnp.int32, (T, T), 0)
    col_i = lax.broadcasted_iota(jnp.int32, (T, T), 1)
    mask = jnp.where(col_i <= row_i, 0.0, NEG_INF)
    mask2 = jnp.concatenate([mask, mask], axis=0)

    def tile_scores(start):
        kt = k_ref[pl.ds(start, T), :]
        va = va_ref[pl.ds(start, T), :]
        vb = vb_ref[pl.ds(start, T), :]
        s = jnp.concatenate([lax.dot_general(qa, kt[:, 0:HEAD_PAD], _NT, preferred_element_type=F32),
                             lax.dot_general(qb, kt[:, HEAD_PAD:2 * HEAD_PAD], _NT, preferred_element_type=F32)],
                            axis=0)

        def pv(p):
            return jnp.concatenate([jnp.dot(p[0:T], va, preferred_element_type=F32),
                                    jnp.dot(p[T:2 * T], vb, preferred_element_type=F32)], axis=0)
        return s, pv

    s, pv = tile_scores(pl.multiple_of(qi * T, T))
    carry = _softmax_first(s + mask2, pv)

    def far(ki, carry):
        s, pv = tile_scores(pl.multiple_of(ki * T, T))
        return _softmax_next(carry, s, pv)

    m, l, acc = lax.fori_loop(0, qi, far, carry)
    o = acc / l
    o_ref[...] = (o[0:T] + o[T:2 * T]).astype(BF16)


def _mlp(qm, km, va, vb, batch, seq):
    T = ATT_TILE
    nq = seq // T
    pw = 2 * HEAD_PAD
    vw = 2 * MLA_V
    return pl.pallas_call(
        functools.partial(_mlp_kernel, tile=T),
        grid=(batch, MLA_HEADS // 2, nq),
        in_specs=[
            pl.BlockSpec((T, pw), lambda b, p, i: (b * nq + i, p)),
            pl.BlockSpec((seq, pw), lambda b, p, i: (b, p)),
            pl.BlockSpec((seq, vw), lambda b, p, i: (b, p)),
            pl.BlockSpec((seq, vw), lambda b, p, i: (b, p)),
        ],
        out_specs=pl.BlockSpec((T, vw), lambda b, p, i: (b * nq + i, p)),
        out_shape=jax.ShapeDtypeStruct((batch * seq, MLA_HEADS * MLA_V), BF16),
        compiler_params=_params(("parallel", "parallel", "arbitrary")),
        name="mlp",
    )(qm, km, va, vb)


def _pad_rows(x, rows):
    return jnp.concatenate([x, jnp.zeros((rows - x.shape[0], x.shape[1]), x.dtype)], axis=0)


def _dfd_kernel(pt_ref, q_ref, kn_ref, vn_ref, bias_ref, lv_ref, hn_ref, *rest, pages, page, lam_init):
    k_refs = rest[0:pages]
    v_refs = rest[pages:2 * pages]
    o_ref, q_scr, m_scr, l_scr, acc_scr = rest[2 * pages:]
    pg = pl.program_id(1)
    last = pl.num_programs(1) - 1
    nrow = DF_HEADS * 2 * SAMPLE_PAD
    width = DF_HEADS * 2 * DF_DK

    @pl.when(pg == 0)
    def _():
        q = q_ref[...].astype(F32)
        qt = jnp.concatenate([q] * (DF_HEADS * 2), axis=0)
        rblk = lax.broadcasted_iota(jnp.int32, (nrow, width), 0) // SAMPLE_PAD
        cblk = lax.broadcasted_iota(jnp.int32, (nrow, width), 1) // DF_DK
        q_scr[...] = jnp.where(rblk == cblk, qt, 0.0).astype(BF16)
        m_scr[...] = jnp.full(m_scr.shape, NEG_INF, F32)
        l_scr[...] = jnp.zeros(l_scr.shape, F32)
        acc_scr[...] = jnp.zeros(acc_scr.shape, F32)

    qbd = q_scr[...]
    hrows = 2 * SAMPLE_PAD

    def pv_heads(p, v_of_head):
        return jnp.concatenate(
            [jnp.dot(p[h * hrows:(h + 1) * hrows], v_of_head(h), preferred_element_type=F32)
             for h in range(DF_HEADS)], axis=0)

    far = bias_ref[0]
    near = jnp.where(pg == last, bias_ref[1], far)
    s = jnp.concatenate(
        [jnp.dot(qbd, k_refs[j][...].astype(BF16), preferred_element_type=F32)
         + (near if j == pages - 1 else far) for j in range(pages)], axis=1)
    v_all = [jnp.concatenate([v_refs[j][pl.ds(h, page, stride=DF_HEADS), :].astype(BF16) for j in range(pages)],
                             axis=0) for h in range(DF_HEADS)]
    carry = _softmax_next((m_scr[...], l_scr[...], acc_scr[...]), s, lambda p: pv_heads(p, lambda h: v_all[h]))
    m_scr[...], l_scr[...], acc_scr[...] = carry

    @pl.when(pg == last)
    def _():
        kn = _pad_rows(kn_ref[...], page).astype(BF16)
        vn = _pad_rows(vn_ref[...], page).astype(BF16)
        s_new = lax.dot_general(qbd, kn, _NT, preferred_element_type=F32) + bias_ref[2]
        m, l, acc = _softmax_next(
            carry, s_new, lambda p: pv_heads(p, lambda h: vn[:, h * DF_DV:(h + 1) * DF_DV]))
        o = acc / l
        lam = _lambda(lv_ref, lam_init)
        for h in range(DF_HEADS):
            r0 = h * hrows
            c = slice(h * DF_DV, (h + 1) * DF_DV)
            od = o[r0:r0 + SAMPLE_PAD] - lam * o[r0 + SAMPLE_PAD:r0 + hrows]
            o_ref[:, c] = _rms(od, hn_ref[:, c]) * (1.0 - lam_init)


def _dfd(layer, dq, dk_new, dv_new, dbias, cache_kt, cache_v, page_table, lw, lam_init):
    nb, n_pages = page_table.shape
    page = cache_kt.shape[3]
    P = math.gcd(DFD_PAGES, n_pages)
    width = DF_HEADS * 2 * DF_DK
    nrow = DF_HEADS * 2 * SAMPLE_PAD
    assert page == LANES

    def kt_spec(j):
        return pl.BlockSpec((None, None, width, page), lambda b, g, pt: (layer, pt[b, g * P + j], 0, 0))

    def v_spec(j):
        return pl.BlockSpec((None, None, page * DF_HEADS, DF_DV),
                            lambda b, g, pt: (layer, pt[b, g * P + j], 0, 0))

    row = pl.BlockSpec((SAMPLE_PAD, width), lambda b, g, pt: (b, 0))
    grid_spec = pltpu.PrefetchScalarGridSpec(
        num_scalar_prefetch=1,
        grid=(nb, n_pages // P),
        in_specs=[pl.BlockSpec((SAMPLE_PAD, width), lambda b, g, pt: (b, 0)),
                  row, row,
                  pl.BlockSpec((3, nrow, page), lambda b, g, pt: (0, 0, 0)),
                  pl.BlockSpec((4, DF_DK), lambda b, g, pt: (0, 0)),
                  pl.BlockSpec((1, width), lambda b, g, pt: (0, 0))]
                 + [kt_spec(j) for j in range(P)] + [v_spec(j) for j in range(P)],
        out_specs=pl.BlockSpec((SAMPLE_PAD, width), lambda b, g, pt: (b, 0)),
        scratch_shapes=[pltpu.VMEM((nrow, width), BF16), pltpu.VMEM((nrow, 1), F32),
                        pltpu.VMEM((nrow, 1), F32), pltpu.VMEM((nrow, DF_DV), F32)],
    )
    return pl.pallas_call(
        functools.partial(_dfd_kernel, pages=P, page=page, lam_init=lam_init),
        grid_spec=grid_spec,
        out_shape=jax.ShapeDtypeStruct((nb * SAMPLE_PAD, width), F32),
        compiler_params=_params(("parallel", "arbitrary")),
        name="dfd",
    )(page_table, dq, dk_new, dv_new, dbias, lw['df_lambda'], lw['df_norm'],
      *([cache_kt] * P), *([cache_v] * P))


def _mld_kernel(pt_ref, q_ref, cn_ref, rn_ref, mask_ref, wuk_ref, wuva_ref, wuvb_ref, *rest, pages, page):
    c_refs = rest[0:pages]
    r_refs = rest[pages:2 * pages]
    o_ref, ql_scr, qr_scr, m_scr, l_scr, acc_scr = rest[2 * pages:]
    pg = pl.program_id(1)
    last = pl.num_programs(1) - 1

    @pl.when(pg == 0)
    def _():
        for h in range(MLA_HEADS):
            sl = slice(h * HEAD_PAD, (h + 1) * HEAD_PAD)
            qh = q_ref[:, sl].astype(BF16)
            rows = slice(h * SAMPLE_PAD, (h + 1) * SAMPLE_PAD)
            ql_scr[rows, :] = lax.dot_general(qh, wuk_ref[:, sl], _NT, preferred_element_type=F32)
            qr_scr[rows, :] = q_ref[:, h * HEAD_PAD + MLA_NOPE:h * HEAD_PAD + MLA_NOPE + MLA_ROPE]
        m_scr[...] = jnp.full(m_scr.shape, NEG_INF, F32)
        l_scr[...] = jnp.zeros(l_scr.shape, F32)
        acc_scr[...] = jnp.zeros(acc_scr.shape, F32)

    ql = ql_scr[...].astype(BF16)
    qr = qr_scr[...].astype(BF16)

    cb = [c_refs[j][...].astype(BF16) for j in range(pages)]
    s = jnp.concatenate(
        [lax.dot_general(ql, cb[j], _NT, preferred_element_type=F32)
         + jnp.dot(qr, r_refs[j][...].astype(BF16), preferred_element_type=F32) for j in range(pages)], axis=1)
    c_all = jnp.concatenate(cb, axis=0)
    carry = _softmax_next((m_scr[...], l_scr[...], acc_scr[...]), s,
                          lambda p: jnp.dot(p, c_all, preferred_element_type=F32))
    m_scr[...], l_scr[...], acc_scr[...] = carry

    @pl.when(pg == last)
    def _():
        cn = _pad_rows(cn_ref[...], page).astype(BF16)
        rn = _pad_rows(rn_ref[...], page).astype(BF16)
        s_new = (lax.dot_general(ql, cn, _NT, preferred_element_type=F32)
                 + lax.dot_general(qr, rn, _NT, preferred_element_type=F32) + mask_ref[...])
        m, l, acc = _softmax_next(carry, s_new, lambda p: jnp.dot(p, cn, preferred_element_type=F32))
        o = (acc / l).astype(BF16)
        for p in range(MLA_HEADS // 2):
            c = slice(p * 2 * MLA_V, (p + 1) * 2 * MLA_V)
            ra = slice(2 * p * SAMPLE_PAD, (2 * p + 1) * SAMPLE_PAD)
            rb = slice((2 * p + 1) * SAMPLE_PAD, (2 * p + 2) * SAMPLE_PAD)
            y = (jnp.dot(o[ra], wuva_ref[:, c], preferred_element_type=F32)
                 + jnp.dot(o[rb], wuvb_ref[:, c], preferred_element_type=F32))
            o_ref[:, c] = y


def _mld(layer, qm, ckv_new, kr_new, new_mask, cache_c, cache_rt, page_table, lw):
    nb, n_pages = page_table.shape
    page = cache_c.shape[2]
    P = math.gcd(MLD_PAGES, n_pages)
    nrow = MLA_HEADS * SAMPLE_PAD
    qw = MLA_HEADS * HEAD_PAD
    ow = MLA_HEADS * MLA_V

    def page_spec(rows, cols, j):
        return pl.BlockSpec((None, None, rows, cols), lambda b, g, pt: (layer, pt[b, g * P + j], 0, 0))

    const = lambda shape: pl.BlockSpec(shape, lambda b, g, pt: (0,) * len(shape))
    grid_spec = pltpu.PrefetchScalarGridSpec(
        num_scalar_prefetch=1,
        grid=(nb, n_pages // P),
        in_specs=[pl.BlockSpec((SAMPLE_PAD, qw), lambda b, g, pt: (b, 0)),
                  pl.BlockSpec((SAMPLE_PAD, MLA_KV_LORA), lambda b, g, pt: (b, 0)),
                  pl.BlockSpec((SAMPLE_PAD, MLA_ROPE), lambda b, g, pt: (b, 0)),
                  const((nrow, page)), const((MLA_KV_LORA, qw)), const((MLA_KV_LORA, ow)),
                  const((MLA_KV_LORA, ow))]
                 + [page_spec(page, MLA_KV_LORA, j) for j in range(P)]
                 + [page_spec(MLA_ROPE, page, j) for j in range(P)],
        out_specs=pl.BlockSpec((SAMPLE_PAD, ow), lambda b, g, pt: (b, 0)),
        scratch_shapes=[pltpu.VMEM((nrow, MLA_KV_LORA), F32), pltpu.VMEM((nrow, MLA_ROPE), F32),
                        pltpu.VMEM((nrow, 1), F32), pltpu.VMEM((nrow, 1), F32),
                        pltpu.VMEM((nrow, MLA_KV_LORA), F32)],
    )
    return pl.pallas_call(
        functools.partial(_mld_kernel, pages=P, page=page),
        grid_spec=grid_spec,
        out_shape=jax.ShapeDtypeStruct((nb * SAMPLE_PAD, ow), F32),
        compiler_params=_params(("parallel", "arbitrary")),
        name="mld",
    )(page_table, qm, ckv_new, kr_new, new_mask, lw['wuk'], lw['wuva'], lw['wuvb'],
      *([cache_c] * P), *([cache_rt] * P))


def _post_kernel(x_ref, ym_ref, yd_ref, yc_ref, gate_ref, g_ref, wbm, wbd, wbc, wout, wfi, wfo, o_ref):
    u = None
    for j, (y_ref, w_ref) in enumerate(((ym_ref, wbm), (yd_ref, wbd), (yc_ref, wbc))):
        t = gate_ref[:, j * D_MODEL:(j + 1) * D_MODEL].astype(F32) * jnp.dot(
            y_ref[...].astype(BF16), w_ref[...], preferred_element_type=F32)
        u = t if u is None else u + t
    x = x_ref[...] + _rms(jnp.dot(u.astype(BF16), wout[...], preferred_element_type=F32), g_ref[0:1])
    hf = _rms(x, g_ref[1:2]).astype(BF16)
    cw = D_FF // 2
    t = None
    for j in range(2):
        a = jnp.dot(hf, wfi[:, j * cw:(j + 1) * cw], preferred_element_type=F32)
        bb = jnp.dot(hf, wfi[:, D_FF + j * cw:D_FF + (j + 1) * cw], preferred_element_type=F32)
        s = (a * jax.nn.sigmoid(a) * bb).astype(BF16)
        d = jnp.dot(s, wfo[j * cw:(j + 1) * cw, :], preferred_element_type=F32)
        t = d if t is None else t + d
    o_ref[...] = x + _rms(t, g_ref[2:3])


def _post(x, ym, yd, yc, gates, lw):
    rows = x.shape[0]
    tm = min(ROW_TILE, rows)
    row = lambda width: pl.BlockSpec((tm, width), lambda i: (i, 0))
    weights = [lw['g123'], lw['w_br_ml'], lw['w_br_df'], lw['w_br_mla'], lw['w_out'], lw['w_ffn_in'], lw['w_ffn_out']]
    return pl.pallas_call(
        _post_kernel,
        grid=(rows // tm,),
        in_specs=[row(D_MODEL), row(512), row(512), row(512), row(N_BRANCH * D_MODEL)]
                 + [_const_spec(w.shape) for w in weights],
        out_specs=row(D_MODEL),
        out_shape=jax.ShapeDtypeStruct((rows, D_MODEL), F32),
        compiler_params=_params(("parallel",)),
        name="post",
    )(x, ym, yd, yc, gates, *weights)


def _rot_half_cols(w):
    half = w.shape[-1] // 2
    return jnp.concatenate([-w[..., half:], w[..., :half]], axis=-1)


def _head_pad(nope, rope):
    k, h = nope.shape[0], nope.shape[1]
    pad = jnp.zeros((k, h, HEAD_PAD - MLA_NOPE - MLA_ROPE), nope.dtype)
    return jnp.concatenate([nope, rope, pad], axis=-1).reshape(k, h * HEAD_PAD)


def _layer_weights(l, norm_gains, w_in, b_ml_gates, ml_head_norm, df_lambda, df_head_norm, mla_q_norm,
                   mla_kv_norm, w_uq, w_uk, w_uv, w_br_ml, w_br_df, w_br_mla, w_out, w_ffn_in, w_ffn_out):
    offs = np.cumsum((0,) + IN_SIZES)
    col = lambda i: w_in[l][:, offs[i]:offs[i + 1]]
    mq, mk, mv, mo, mi, mf, dq, dk, dv, cq, ckv, kr, gl = (col(i) for i in range(len(IN_SIZES)))
    zeros = lambda n: jnp.zeros((D_MODEL, n), F32)
    rope_grp = lambda w: jnp.concatenate([zeros(MLA_NOPE), w, zeros(HEAD_PAD - MLA_NOPE - MLA_ROPE)], axis=1)
    wmisc = jnp.concatenate([mi, mf, zeros(LANES - 2 * ML_HEADS), rope_grp(kr), rope_grp(_rot_half_cols(kr))], axis=1)
    uq = w_uq[l].reshape(MLA_Q_LORA, MLA_HEADS, MLA_NOPE + MLA_ROPE)
    uq_n, uq_r = uq[..., :MLA_NOPE], uq[..., MLA_NOPE:]
    uv = w_uv[l]
    zv = jnp.zeros_like(uv[:, 0::2])
    bf = lambda a: a.astype(BF16)
    return {
        'g0': norm_gains[l, 0:1], 'g123': norm_gains[l, 1:4],
        'wml': bf(jnp.concatenate([mq, mk, mv, mo], axis=1)), 'wdf': bf(jnp.concatenate([dq, dk, dv], axis=1)),
        'wcq': bf(cq), 'wckv': bf(ckv), 'wmisc': bf(wmisc), 'wgl': bf(gl),
        'q_norm': mla_q_norm[l][None], 'kv_norm': mla_kv_norm[l][None],
        'wuqa': bf(_head_pad(uq_n, uq_r)), 'wuqb': bf(_head_pad(jnp.zeros_like(uq_n), _rot_half_cols(uq_r))),
        'wuk': bf(_head_pad(w_uk[l], jnp.zeros((MLA_KV_LORA, MLA_HEADS, MLA_ROPE), F32))),
        'wuva': bf(jnp.concatenate([uv[:, 0::2], zv], axis=-1).reshape(MLA_KV_LORA, MLA_HEADS * MLA_V)),
        'wuvb': bf(jnp.concatenate([zv, uv[:, 1::2]], axis=-1).reshape(MLA_KV_LORA, MLA_HEADS * MLA_V)),
        'gate_bias': jnp.concatenate([b_ml_gates[l], jnp.zeros((LANES - 2 * ML_HEADS,), F32)])[None],
        'ml_norm': ml_head_norm[l][None], 'df_lambda': df_lambda[l], 'df_norm': df_head_norm[l][None],
        'w_br_ml': bf(w_br_ml[l]), 'w_br_df': bf(w_br_df[l]), 'w_br_mla': bf(w_br_mla[l]),
        'w_out': bf(w_out[l]), 'w_ffn_in': bf(w_ffn_in[l]), 'w_ffn_out': bf(w_ffn_out[l]),
    }


def _rope_tables(pos):
    half = MLA_ROPE // 2
    freqs = ROPE_THETA ** (-jnp.arange(half, dtype=F32) / half)
    ang = pos.astype(F32)[:, None] * freqs[None, :]
    cos, sin = jnp.cos(ang), jnp.sin(ang)
    n = pos.shape[0]
    tail = jnp.zeros((n, HEAD_PAD - MLA_NOPE - MLA_ROPE), F32)
    cs = jnp.concatenate([jnp.ones((n, MLA_NOPE), F32), cos, cos, tail], axis=1)
    sn = jnp.concatenate([jnp.zeros((n, MLA_NOPE), F32), sin, sin, tail], axis=1)
    return cs, sn


def kernel(x_prompt, x_sample, state_mlstm_C, state_mlstm_n, state_mlstm_m, cache_diff_k, cache_diff_v,
           cache_mla_ckv, cache_mla_krope, page_table, norm_gains, w_in, b_ml_gates, ml_head_norm, df_lambda,
           df_head_norm, rel_bias, mla_q_norm, mla_kv_norm, w_uq, w_uk, w_uv, w_br_ml, w_br_df, w_br_mla, w_out,
           w_ffn_in, w_ffn_out):
    depth = w_in.shape[0]
    B, S, _ = x_prompt.shape
    DB, DS, _ = x_sample.shape
    n_pages = page_table.shape[1]
    n_pool, page = cache_diff_k.shape[1], cache_diff_k.shape[2]
    past_len = n_pages * page
    assert DS <= SAMPLE_PAD
    T = ATT_TILE

    cs_p, sn_p = _rope_tables(jnp.tile(jnp.arange(S, dtype=jnp.int32), B))
    pos_s = past_len + jnp.arange(SAMPLE_PAD, dtype=jnp.int32)
    cs_s, sn_s = _rope_tables(jnp.tile(pos_s, DB))
    ii, jj = np.meshgrid(np.arange(T), np.arange(T), indexing='ij')
    diag, sub = _t5_bucket_np(ii - jj), _t5_bucket_np(T + ii - jj)
    bias_p = _bias_table(rel_bias, np.concatenate([diag, diag, sub, sub], axis=0)).reshape(DF_HEADS, 2, 2 * T, T)
    rr, kk = np.meshgrid(np.arange(SAMPLE_PAD), np.arange(page), indexing='ij')
    new_ok = (kk <= rr) & (kk < DS)
    idx_d = np.concatenate([np.full((SAMPLE_PAD, page), REL_BUCKETS - 1, np.int32),
                            _t5_bucket_np(page + rr - kk),
                            np.where(new_ok, _t5_bucket_np(rr - kk), -1)], axis=0)
    bias_d = _bias_table(rel_bias, idx_d).reshape(DF_HEADS, 3, 1, SAMPLE_PAD, page)
    bias_d = jnp.broadcast_to(bias_d, (DF_HEADS, 3, 2, SAMPLE_PAD, page))
    bias_d = jnp.transpose(bias_d, (1, 0, 2, 3, 4)).reshape(3, DF_HEADS * 2 * SAMPLE_PAD, page)
    mla_new_mask = jnp.asarray(np.tile(np.where(new_ok, 0.0, NEG_INF).astype(np.float32), (MLA_HEADS, 1)))

    ckt = jnp.transpose(cache_diff_k, (0, 1, 3, 4, 5, 2)).reshape(depth, n_pool, DF_HEADS * 2 * DF_DK, page)
    crt = jnp.transpose(cache_mla_krope, (0, 1, 3, 2))
    cv = cache_diff_v.reshape(depth, n_pool, page * DF_HEADS, DF_DV)

    xp = x_prompt.reshape(B * S, D_MODEL)
    xs = jnp.pad(x_sample, ((0, 0), (0, SAMPLE_PAD - DS), (0, 0))).reshape(DB * SAMPLE_PAD, D_MODEL)
    zero_c = jnp.zeros((B, ML_HEADS, ML_DK, ML_DV), F32)
    zero_n = jnp.zeros((B, ML_HEADS, ML_DK), F32)
    zero_m = jnp.zeros((B, ML_HEADS), F32)
    st_p, st_s = [], []
    for l in range(depth):
        lw = _layer_weights(l, norm_gains, w_in, b_ml_gates, ml_head_norm, df_lambda, df_head_norm, mla_q_norm,
                            mla_kv_norm, w_uq, w_uk, w_uv, w_br_ml, w_br_df, w_br_mla, w_out, w_ffn_in, w_ffn_out)
        lam_init = 0.8 - 0.6 * math.exp(-0.3 * l)

        (ml, gif, dq, dk, dv, qm, ckv, kr, gates, dkb, dvb, km, va, vb) = _proj(xp, lw, cs_p, sn_p, True)
        ym, c_p, n_p, m_p = _mlstm(ml, gif, lw, zero_c, zero_n, zero_m, B, S, math.gcd(S, ML_CHUNK), S)
        yd = _dfp(dq, dkb, dvb, bias_p, rel_bias, lw, B, S, lam_init)
        yc = _mlp(qm, km, va, vb, B, S)
        xp = _post(xp, ym, yd, yc, gates, lw)
        st_p.append((dk.reshape(B, S, DF_HEADS, 2, DF_DK), dv.reshape(B, S, DF_HEADS, DF_DV),
                     ckv.reshape(B, S, MLA_KV_LORA), kr.reshape(B, S, MLA_ROPE), c_p, n_p, m_p))

        (ml, gif, dq, dk, dv, qm, ckv, kr, gates) = _proj(xs, lw, cs_s, sn_s, False)
        ym, c_s, n_s, m_s = _mlstm(ml, gif, lw, state_mlstm_C[l], state_mlstm_n[l], state_mlstm_m[l],
                                   DB, SAMPLE_PAD, SAMPLE_PAD, DS)
        yd = _dfd(l, dq, dk, dv, bias_d, ckt, cv, page_table, lw, lam_init)
        yc = _mld(l, qm, ckv, kr, mla_new_mask, cache_mla_ckv, crt, page_table, lw)
        xs = _post(xs, ym, yd, yc, gates, lw)
        tok = lambda a, *tail: a.reshape((DB, SAMPLE_PAD) + tail)[:, :DS]
        st_s.append((tok(dk, DF_HEADS, 2, DF_DK), tok(dv, DF_HEADS, DF_DV), tok(ckv, MLA_KV_LORA),
                     tok(kr, MLA_ROPE), c_s, n_s, m_s))

    outs_p = [jnp.stack(a) for a in zip(*st_p)]
    outs_s = [jnp.stack(a) for a in zip(*st_s)]
    yp = xp.reshape(B, S, D_MODEL)
    ys = xs.reshape(DB, SAMPLE_PAD, D_MODEL)[:, :DS]
    return (yp, ys, *outs_p, *outs_s)
```

```python
import functools
import math

import numpy as np
import jax
import jax.numpy as jnp
from jax import lax
from jax.experimental import pallas as pl
from jax.experimental.pallas import tpu as pltpu

F32 = jnp.float32
BF16 = jnp.bfloat16

D_MODEL = 1024
ML_HEADS = 4
ML_DK = 128
ML_DV = 128
DF_HEADS = 4
DF_DK = 64
DF_DV = 2 * DF_DK
MLA_HEADS = 8
MLA_NOPE = 64
MLA_ROPE = 32
MLA_V = 64
MLA_Q_LORA = 384
MLA_KV_LORA = 256
ROPE_THETA = 10000.0
REL_BUCKETS = 32
REL_MAX_DIST = 128
N_BRANCH = 3
D_FF = 2816
EPS = 1e-6
NEG_INF = -1e30
LOG2E = math.log2(math.e)
IN_SIZES = (ML_HEADS * ML_DK, ML_HEADS * ML_DK, ML_HEADS * ML_DV, ML_HEADS * ML_DV, ML_HEADS, ML_HEADS,
            DF_HEADS * 2 * DF_DK, DF_HEADS * 2 * DF_DK, DF_HEADS * DF_DV,
            MLA_Q_LORA, MLA_KV_LORA, MLA_ROPE,
            N_BRANCH * D_MODEL)

LANES = 128
SUBLANES = 8
HEAD_PAD = 128
ROW_TILE = 256
ATT_TILE = 512
ML_CHUNK = 256
SAMPLE_PAD = 8
DFD_PAGES = 16
MLD_PAGES = 32
VMEM_LIMIT = 56 * 1024 * 1024

_NT = (((1,), (1,)), ((), ()))
_TN = (((0,), (0,)), ((), ()))


def _params(sem):
    return pltpu.CompilerParams(dimension_semantics=sem, vmem_limit_bytes=VMEM_LIMIT)


def _const_spec(shape):
    nd = len(shape)
    return pl.BlockSpec(shape, lambda *_: (0,) * nd, pipeline_mode=pl.Buffered(1))


def _rms(x, g):
    return x * lax.rsqrt(jnp.mean(x * x, axis=-1, keepdims=True) + EPS) * g


def _t5_bucket_np(n):
    n = np.asarray(n, np.int64)
    exact = REL_BUCKETS // 2
    nf = np.maximum(n, 1).astype(np.float32)
    large = exact + (np.log(nf / np.float32(exact)) / np.float32(math.log(REL_MAX_DIST / exact))
                     * np.float32(REL_BUCKETS - exact)).astype(np.int32)
    large = np.minimum(large, REL_BUCKETS - 1)
    b = np.where(n < exact, n, large)
    return np.where(n < 0, -1, b).astype(np.int32)


def _bias_table_kernel(rb_ref, idx_ref, out_ref):
    idx = idx_ref[...]
    for h in range(DF_HEADS):
        acc = jnp.full(idx.shape, NEG_INF, F32)
        for b in range(REL_BUCKETS):
            acc = jnp.where(idx == b, rb_ref[b, h] * LOG2E, acc)
        out_ref[h] = acc


def _bias_table(rel_bias, idx_np):
    r, c = idx_np.shape
    return pl.pallas_call(
        _bias_table_kernel,
        out_shape=jax.ShapeDtypeStruct((DF_HEADS, r, c), F32),
        in_specs=[pl.BlockSpec(memory_space=pltpu.SMEM), pl.BlockSpec(memory_space=pltpu.VMEM)],
        out_specs=pl.BlockSpec(memory_space=pltpu.VMEM),
        name="bias_table",
    )(rel_bias, jnp.asarray(idx_np))


def _proj_kernel(x_ref, g_ref, cs_ref, sn_ref, wml, wdf, wcq, wckv, wmisc, wgl, qn_ref, kvn_ref,
                 wuqa, wuqb, wuk, wdvt, wuvt,
                 ml_o, gif_o, dq_o, dk_o, dv_o, qm_o, ckv_o, kr_o, gate_o, *prompt_outs, prompt):
    x = x_ref[...]
    h = _rms(x, g_ref[...]).astype(BF16)

    def mm(w_ref, lo, hi):
        return jnp.dot(h, w_ref[:, lo:hi], preferred_element_type=F32)

    w = ML_HEADS * ML_DK
    act = ml_o.dtype
    ml_o[:, 0:w] = mm(wml, 0, w).astype(act)
    ml_o[:, w:2 * w] = (mm(wml, w, 2 * w) * ML_DK ** -0.5).astype(act)
    ml_o[:, 2 * w:3 * w] = mm(wml, 2 * w, 3 * w).astype(act)
    ml_o[:, 3 * w:4 * w] = jax.nn.sigmoid(mm(wml, 3 * w, 4 * w)).astype(act)

    misc = mm(wmisc, 0, 3 * LANES)
    gif_o[...] = misc[:, 0:LANES]
    cs = cs_ref[...]
    sn = sn_ref[...]
    krp = misc[:, LANES:2 * LANES] * cs + misc[:, 2 * LANES:3 * LANES] * sn
    kr_o[...] = krp[:, MLA_NOPE:MLA_NOPE + MLA_ROPE]

    wd = DF_HEADS * 2 * DF_DK
    dq_o[...] = (mm(wdf, 0, wd) * (DF_DK ** -0.5 * LOG2E)).astype(act)
    dk = mm(wdf, wd, 2 * wd)
    dv = mm(wdf, 2 * wd, 3 * wd)
    dk_o[...] = dk
    dv_o[...] = dv

    c_q = _rms(mm(wcq, 0, MLA_Q_LORA), qn_ref[...]).astype(BF16)
    c_kv = _rms(mm(wckv, 0, MLA_KV_LORA), kvn_ref[...])
    ckv_o[...] = c_kv
    scale = (MLA_NOPE + MLA_ROPE) ** -0.5 * LOG2E
    for hh in range(MLA_HEADS):
        sl = slice(hh * HEAD_PAD, (hh + 1) * HEAD_PAD)
        qa = jnp.dot(c_q, wuqa[:, sl], preferred_element_type=F32)
        qb = jnp.dot(c_q, wuqb[:, sl], preferred_element_type=F32)
        qm_o[:, sl] = ((qa * cs + qb * sn) * scale).astype(act)

    gw = 512
    for j in range(N_BRANCH * D_MODEL // gw):
        gate_o[:, j * gw:(j + 1) * gw] = jax.nn.sigmoid(mm(wgl, j * gw, (j + 1) * gw)).astype(BF16)

    if prompt:
        dkb_o, dvt_o, km_o, vmt_o = prompt_outs
        dkb_o[...] = dk.astype(BF16)
        dvt_o[...] = lax.dot_general(wdvt[...], h, _NT, preferred_element_type=F32).astype(BF16)
        ckb = c_kv.astype(BF16)
        for hh in range(MLA_HEADS):
            sl = slice(hh * HEAD_PAD, (hh + 1) * HEAD_PAD)
            kn = jnp.dot(ckb, wuk[:, sl], preferred_element_type=F32)
            km_o[:, sl] = (kn + krp).astype(BF16)
        vmt_o[...] = lax.dot_general(wuvt[...], ckb, _NT, preferred_element_type=F32).astype(BF16)


def _proj(x, lw, cs, sn, prompt):
    rows = x.shape[0]
    tm = min(ROW_TILE, rows)
    assert rows % tm == 0
    row = lambda width: pl.BlockSpec((tm, width), lambda i: (i, 0))
    assert cs.shape[0] % tm == 0 and rows % cs.shape[0] == 0
    pos_tiles = cs.shape[0] // tm
    pos_row = pl.BlockSpec((tm, LANES), lambda i: (i % pos_tiles, 0))
    weights = [lw['wml'], lw['wdf'], lw['wcq'], lw['wckv'], lw['wmisc'], lw['wgl'], lw['q_norm'], lw['kv_norm'],
               lw['wuqa'], lw['wuqb'], lw['wuk'], lw['wdvt'], lw['wuvt']]
    act = BF16 if prompt else F32
    out_widths = [(4 * ML_HEADS * ML_DK, act), (LANES, F32), (512, act), (512, F32), (512, F32),
                  (MLA_HEADS * HEAD_PAD, act), (MLA_KV_LORA, F32), (MLA_ROPE, F32), (N_BRANCH * D_MODEL, BF16)]
    out_specs = [row(wd) for wd, _ in out_widths]
    out_shape = [jax.ShapeDtypeStruct((rows, wd), dt) for wd, dt in out_widths]
    if prompt:
        col = lambda height: pl.BlockSpec((height, tm), lambda i: (0, i))
        out_specs += [row(512), col(512), row(MLA_HEADS * HEAD_PAD), col(512)]
        out_shape += [jax.ShapeDtypeStruct((rows, 512), BF16), jax.ShapeDtypeStruct((512, rows), BF16),
                      jax.ShapeDtypeStruct((rows, MLA_HEADS * HEAD_PAD), BF16),
                      jax.ShapeDtypeStruct((512, rows), BF16)]
    return pl.pallas_call(
        functools.partial(_proj_kernel, prompt=prompt),
        grid=(rows // tm,),
        in_specs=[row(D_MODEL), _const_spec((1, D_MODEL)), pos_row, pos_row]
                 + [_const_spec(w.shape) for w in weights],
        out_specs=out_specs,
        out_shape=out_shape,
        compiler_params=_params(("parallel",)),
        name="proj_prompt" if prompt else "proj_sample",
    )(x, lw['g0'], cs, sn, *weights)


def _mlstm_kernel(m0_ref, ml_ref, gif_ref, gb_ref, hn_ref, c0_ref, n0_ref,
                  ym_o, c_o, n_o, m_o, cn_scr, m_scr, *, chunk, t_valid):
    b = pl.program_id(0)
    c = pl.program_id(1)
    L = chunk
    dk, dv = ML_DK, ML_DV
    lane = lax.broadcasted_iota(jnp.int32, (1, LANES), 1)

    @pl.when(c == 0)
    def _():
        for h in range(ML_HEADS):
            ncol = jnp.where(lax.broadcasted_iota(jnp.int32, (dk, LANES), 1) == 0, n0_ref[0, h], 0.0)
            cn_scr[h] = jnp.concatenate([c0_ref[0, h], ncol], axis=1)
            m_scr[h] = jnp.full((SUBLANES, LANES), m0_ref[b, h], F32)

    g = gif_ref[...] + gb_ref[...]
    g = jnp.where(lane < ML_HEADS, g, jnp.minimum(g, 0.0) - jnp.log(1.0 + jnp.exp(-jnp.abs(g))))
    row_i = lax.broadcasted_iota(jnp.int32, (L, L), 0)
    col_i = lax.broadcasted_iota(jnp.int32, (L, L), 1)
    if t_valid < L:
        tok = lax.broadcasted_iota(jnp.int32, (L, LANES), 0)
        g = jnp.where(tok < t_valid, g, jnp.where(lane < ML_HEADS, NEG_INF, 0.0))
    causal = col_i <= row_i
    eye = col_i == row_i

    for h in range(ML_HEADS):
        lic = g[:, h:h + 1]
        lfc = g[:, ML_HEADS + h:ML_HEADS + h + 1]
        br = jnp.sum(jnp.where(row_i <= col_i, lfc, 0.0), axis=0, keepdims=True)
        bc = jnp.sum(jnp.where(eye, br, 0.0), axis=1, keepdims=True)
        ir = jnp.sum(jnp.where(eye, lic, 0.0), axis=0, keepdims=True)
        b_last = jnp.sum(lfc, axis=0, keepdims=True)
        m_prev = m_scr[h][0:1, 0:1]

        q = ml_ref[:, h * dk:(h + 1) * dk].astype(BF16)
        k = ml_ref[:, (ML_HEADS + h) * dk:(ML_HEADS + h + 1) * dk].astype(BF16)
        v = ml_ref[:, (2 * ML_HEADS + h) * dk:(2 * ML_HEADS + h + 1) * dk].astype(BF16)
        og = ml_ref[:, (3 * ML_HEADS + h) * dk:(3 * ML_HEADS + h + 1) * dk]

        dmat = jnp.where(causal, bc - br + ir, NEG_INF)
        inter = bc + m_prev
        mt = jnp.maximum(inter, jnp.max(dmat, axis=1, keepdims=True))
        wgt = jnp.exp(dmat - mt)
        iw = jnp.exp(inter - mt)
        a = lax.dot_general(q, k, _NT, preferred_element_type=F32) * wgt
        cn = cn_scr[h]
        qc = jnp.dot(q, cn.astype(BF16), preferred_element_type=F32)
        num = jnp.dot(a.astype(BF16), v, preferred_element_type=F32) + iw * qc[:, 0:dv]
        den = jnp.sum(a, axis=1, keepdims=True) + iw * qc[:, dv:dv + 1]
        hh = num / jnp.maximum(jnp.abs(den), jnp.exp(-mt))
        y = _rms(hh, hn_ref[:, h * dv:(h + 1) * dv]) * og.astype(F32)
        ym_o[:, h * dv:(h + 1) * dv] = y.astype(ym_o.dtype)

        m_new = jnp.maximum(b_last + m_prev, jnp.max(b_last - br + ir, axis=1, keepdims=True))
        wl = jnp.exp(b_last - bc + lic - m_new)
        dec = jnp.exp(b_last + m_prev - m_new)
        ones_col = jnp.where(lax.broadcasted_iota(jnp.int32, (L, LANES), 1) == 0, wl, 0.0)
        wv = jnp.concatenate([wl * v.astype(F32), ones_col], axis=1).astype(BF16)
        cn_new = dec * cn + lax.dot_general(k, wv, _TN, preferred_element_type=F32)
        cn_scr[h] = cn_new
        m_scr[h] = jnp.broadcast_to(m_new, (SUBLANES, LANES))
        c_o[0, h] = cn_new[:, 0:dv]
        n_o[0, h] = cn_new[:, dv:dv + 1]
        m_o[0, h] = jnp.broadcast_to(m_new, (SUBLANES, LANES))


def _mlstm(ml, gif, lw, c0, n0, m0, batch, seq, chunk, t_valid):
    nc = seq // chunk
    assert seq % chunk == 0
    width = ML_HEADS * ML_DV
    grid_spec = pltpu.PrefetchScalarGridSpec(
        num_scalar_prefetch=1,
        grid=(batch, nc),
        in_specs=[
            pl.BlockSpec((chunk, 4 * width), lambda b, c, m: (b * nc + c, 0)),
            pl.BlockSpec((chunk, LANES), lambda b, c, m: (b * nc + c, 0)),
            pl.BlockSpec((1, LANES), lambda b, c, m: (0, 0)),
            pl.BlockSpec((1, width), lambda b, c, m: (0, 0)),
            pl.BlockSpec((1, ML_HEADS, ML_DK, ML_DV), lambda b, c, m: (b, 0, 0, 0)),
            pl.BlockSpec((1, ML_HEADS, ML_DK, 1), lambda b, c, m: (b, 0, 0, 0)),
        ],
        out_specs=[
            pl.BlockSpec((chunk, width), lambda b, c, m: (b * nc + c, 0)),
            pl.BlockSpec((1, ML_HEADS, ML_DK, ML_DV), lambda b, c, m: (b, 0, 0, 0)),
            pl.BlockSpec((1, ML_HEADS, ML_DK, 1), lambda b, c, m: (b, 0, 0, 0)),
            pl.BlockSpec((1, ML_HEADS, SUBLANES, LANES), lambda b, c, m: (b, 0, 0, 0)),
        ],
        scratch_shapes=[pltpu.VMEM((ML_HEADS, ML_DK, 2 * ML_DV), F32),
                        pltpu.VMEM((ML_HEADS, SUBLANES, LANES), F32)],
    )
    ym, c_new, n_new, m_new = pl.pallas_call(
        functools.partial(_mlstm_kernel, chunk=chunk, t_valid=t_valid),
        grid_spec=grid_spec,
        out_shape=[jax.ShapeDtypeStruct((batch * seq, width), ml.dtype),
                   jax.ShapeDtypeStruct((batch, ML_HEADS, ML_DK, ML_DV), F32),
                   jax.ShapeDtypeStruct((batch, ML_HEADS, ML_DK, 1), F32),
                   jax.ShapeDtypeStruct((batch, ML_HEADS, SUBLANES, LANES), F32)],
        compiler_params=_params(("parallel", "arbitrary")),
        name="mlstm",
    )(m0, ml, gif, lw['gate_bias'], lw['ml_norm'], c0, n0[..., None])
    return ym, c_new, n_new[..., 0], m_new[:, :, 0, 0]


def _softmax_first(s, pv, axis=1):
    m = jnp.max(s, axis=axis, keepdims=True)
    p = jnp.exp2(s - m)
    return m, jnp.sum(p, axis=axis, keepdims=True), pv(p.astype(BF16))


def _softmax_next(carry, s, pv, axis=1, shift=None):
    m, l, acc = carry
    top = jnp.max(s, axis=axis, keepdims=True)
    m_new = jnp.maximum(m, top if shift is None else top + shift)
    alpha = jnp.exp2(m - m_new)
    p = jnp.exp2(s - (m_new if shift is None else m_new - shift))
    return m_new, alpha * l + jnp.sum(p, axis=axis, keepdims=True), alpha * acc + pv(p.astype(BF16))


def _lambda(lv_ref, lam_init):
    lv = lv_ref[...]
    e1 = jnp.exp(jnp.sum(lv[0:1] * lv[1:2], axis=1, keepdims=True))
    e2 = jnp.exp(jnp.sum(lv[2:3] * lv[3:4], axis=1, keepdims=True))
    return e1 - e2 + lam_init


def _dfp_kernel(rb_ref, q_ref, k_ref, vt_ref, bias_ref, lv_ref, hn_ref, o_ref, *, tile, lam_init):
    h = pl.program_id(1)
    qi = pl.program_id(2)
    T = tile
    q = q_ref[...]
    lane = lax.broadcasted_iota(jnp.int32, (T, 2 * DF_DK), 1)
    zero = jnp.zeros_like(q)
    qs = (jnp.where(lane < DF_DK, q, zero), jnp.where(lane >= DF_DK, q, zero))

    def tile_scores(start):
        kt = k_ref[pl.ds(start, T), :]
        vt = vt_ref[:, pl.ds(start, T)]
        s = [lax.dot_general(kt, qm, _NT, preferred_element_type=F32) for qm in qs]
        return s, lambda p: jnp.dot(vt, p, preferred_element_type=F32)

    s, pv = tile_scores(pl.multiple_of(qi * T, T))
    carry = tuple(_softmax_first(sm + bias_ref[0, 0], pv, axis=0) for sm in s)
    sub = jnp.maximum(qi - 1, 0)
    s, pv = tile_scores(pl.multiple_of(sub * T, T))
    sub_bias = bias_ref[0, 1] + jnp.where(qi == 0, NEG_INF, 0.0)
    carry = tuple(_softmax_next(c, sm + sub_bias, pv, axis=0) for c, sm in zip(carry, s))
    far_bias = rb_ref[REL_BUCKETS - 1, h] * LOG2E

    def far(ki, carry):
        s, pv = tile_scores(pl.multiple_of(ki * T, T))
        return tuple(_softmax_next(c, sm, pv, axis=0, shift=far_bias) for c, sm in zip(carry, s))

    (_, l0, acc0), (_, l1, acc1) = lax.fori_loop(0, sub, far, carry)
    lam = _lambda(lv_ref, lam_init)
    od = (acc0 / l0 - lam * (acc1 / l1)).T
    o_ref[...] = (_rms(od, hn_ref[...]) * (1.0 - lam_init)).astype(BF16)


def _dfp(dq, dkb, dvt, bias_tiles, rel_bias, lw, batch, seq, lam_init):
    T = ATT_TILE
    nq = seq // T
    assert seq % T == 0 and T >= REL_MAX_DIST
    hw = 2 * DF_DK
    grid_spec = pltpu.PrefetchScalarGridSpec(
        num_scalar_prefetch=0,
        grid=(batch, DF_HEADS, nq),
        in_specs=[
            pl.BlockSpec(memory_space=pltpu.SMEM),
            pl.BlockSpec((T, hw), lambda b, h, i: (b * nq + i, h)),
            pl.BlockSpec((seq, hw), lambda b, h, i: (b, h)),
            pl.BlockSpec((DF_DV, seq), lambda b, h, i: (h, b)),
            pl.BlockSpec((1, 2, T, T), lambda b, h, i: (h, 0, 0, 0)),
            pl.BlockSpec((4, DF_DK), lambda b, h, i: (0, 0)),
            pl.BlockSpec((1, hw), lambda b, h, i: (0, h)),
        ],
        out_specs=pl.BlockSpec((T, hw), lambda b, h, i: (b * nq + i, h)),
    )
    return pl.pallas_call(
        functools.partial(_dfp_kernel, tile=T, lam_init=lam_init),
        grid_spec=grid_spec,
        out_shape=jax.ShapeDtypeStruct((batch * seq, DF_HEADS * hw), BF16),
        compiler_params=_params(("parallel", "parallel", "arbitrary")),
        name="dfp",
    )(rel_bias, dq, dkb, dvt, bias_tiles, lw['df_lambda'], lw['df_norm'])


def _mlp_kernel(q_ref, k_ref, vt_ref, o_ref, *, tile):
    qi = pl.program_id(2)
    T = tile
    qs = (q_ref[:, 0:HEAD_PAD], q_ref[:, HEAD_PAD:2 * HEAD_PAD])
    key_i = lax.broadcasted_iota(jnp.int32, (T, T), 0)
    qry_i = lax.broadcasted_iota(jnp.int32, (T, T), 1)
    mask = jnp.where(key_i <= qry_i, 0.0, NEG_INF)

    def tile_scores(start):
        kt = k_ref[pl.ds(start, T), :]
        vt = vt_ref[:, pl.ds(start, T)]
        s = [lax.dot_general(kt[:, j * HEAD_PAD:(j + 1) * HEAD_PAD], qs[j], _NT, preferred_element_type=F32)
             for j in range(2)]
        return s, lambda p: jnp.dot(vt, p, preferred_element_type=F32)

    s, pv = tile_scores(pl.multiple_of(qi * T, T))
    carry = tuple(_softmax_first(sm + mask, pv, axis=0) for sm in s)

    def far(ki, carry):
        s, pv = tile_scores(pl.multiple_of(ki * T, T))
        return tuple(_softmax_next(c, sm, pv, axis=0) for c, sm in zip(carry, s))

    (_, la, acca), (_, lb, accb) = lax.fori_loop(0, qi, far, carry)
    row = lax.broadcasted_iota(jnp.int32, (2 * MLA_V, T), 0)
    o_ref[...] = jnp.where(row < MLA_V, acca / la, accb / lb).T.astype(BF16)


def _mlp(qm, km, vmt, batch, seq):
    T = ATT_TILE
    nq = seq // T
    pw = 2 * HEAD_PAD
    vw = 2 * MLA_V
    return pl.pallas_call(
        functools.partial(_mlp_kernel, tile=T),
        grid=(batch, MLA_HEADS // 2, nq),
        in_specs=[
            pl.BlockSpec((T, pw), lambda b, p, i: (b * nq + i, p)),
            pl.BlockSpec((seq, pw), lambda b, p, i: (b, p)),
            pl.BlockSpec((vw, seq), lambda b, p, i: (p, b)),
        ],
        out_specs=pl.BlockSpec((T, vw), lambda b, p, i: (b * nq + i, p)),
        out_shape=jax.ShapeDtypeStruct((batch * seq, MLA_HEADS * MLA_V), BF16),
        compiler_params=_params(("parallel", "parallel", "arbitrary")),
        name="mlp",
    )(qm, km, vmt)


def _pad_rows(x, rows):
    return jnp.concatenate([x, jnp.zeros((rows - x.shape[0], x.shape[1]), x.dtype)], axis=0)


def _dfd_kernel(pt_ref, q_ref, kn_ref, vn_ref, bias_ref, lv_ref, hn_ref, *rest, pages, page, lam_init):
    k_refs = rest[0:pages]
    v_refs = rest[pages:2 * pages]
    o_ref, q_scr, m_scr, l_scr, acc_scr = rest[2 * pages:]
    pg = pl.program_id(1)
    last = pl.num_programs(1) - 1
    nrow = DF_HEADS * 2 * SAMPLE_PAD
    width = DF_HEADS * 2 * DF_DK

    @pl.when(pg == 0)
    def _():
        q = q_ref[...].astype(F32)
        qt = jnp.concatenate([q] * (DF_HEADS * 2), axis=0)
        rblk = lax.broadcasted_iota(jnp.int32, (nrow, width), 0) // SAMPLE_PAD
        cblk = lax.broadcasted_iota(jnp.int32, (nrow, width), 1) // DF_DK
        q_scr[...] = jnp.where(rblk == cblk, qt, 0.0).astype(BF16)
        m_scr[...] = jnp.full(m_scr.shape, NEG_INF, F32)
        l_scr[...] = jnp.zeros(l_scr.shape, F32)
        acc_scr[...] = jnp.zeros(acc_scr.shape, F32)

    qbd = q_scr[...]
    hrows = 2 * SAMPLE_PAD

    def pv_heads(p, v_of_head):
        return jnp.concatenate(
            [jnp.dot(p[h * hrows:(h + 1) * hrows], v_of_head(h), preferred_element_type=F32)
             for h in range(DF_HEADS)], axis=0)

    far = bias_ref[0]
    near = jnp.where(pg == last, bias_ref[1], far)
    s = jnp.concatenate(
        [jnp.dot(qbd, k_refs[j][...].astype(BF16), preferred_element_type=F32)
         + (near if j == pages - 1 else far) for j in range(pages)], axis=1)
    v_all = [jnp.concatenate([v_refs[j][pl.ds(h, page, stride=DF_HEADS), :].astype(BF16) for j in range(pages)],
                             axis=0) for h in range(DF_HEADS)]
    carry = _softmax_next((m_scr[...], l_scr[...], acc_scr[...]), s, lambda p: pv_heads(p, lambda h: v_all[h]))
    m_scr[...], l_scr[...], acc_scr[...] = carry

    @pl.when(pg == last)
    def _():
        kn = _pad_rows(kn_ref[...], page).astype(BF16)
        vn = _pad_rows(vn_ref[...], page).astype(BF16)
        s_new = lax.dot_general(qbd, kn, _NT, preferred_element_type=F32) + bias_ref[2]
        m, l, acc = _softmax_next(
            carry, s_new, lambda p: pv_heads(p, lambda h: vn[:, h * DF_DV:(h + 1) * DF_DV]))
        o = acc / l
        lam = _lambda(lv_ref, lam_init)
        for h in range(DF_HEADS):
            r0 = h * hrows
            c = slice(h * DF_DV, (h + 1) * DF_DV)
            od = o[r0:r0 + SAMPLE_PAD] - lam * o[r0 + SAMPLE_PAD:r0 + hrows]
            o_ref[:, c] = _rms(od, hn_ref[:, c]) * (1.0 - lam_init)


def _dfd(layer, dq, dk_new, dv_new, dbias, cache_kt, cache_v, page_table, lw, lam_init):
    nb, n_pages = page_table.shape
    page = cache_kt.shape[3]
    P = math.gcd(DFD_PAGES, n_pages)
    width = DF_HEADS * 2 * DF_DK
    nrow = DF_HEADS * 2 * SAMPLE_PAD
    assert page == LANES

    def kt_spec(j):
        return pl.BlockSpec((None, None, width, page), lambda b, g, pt: (layer, pt[b, g * P + j], 0, 0))

    def v_spec(j):
        return pl.BlockSpec((None, None, page * DF_HEADS, DF_DV),
                            lambda b, g, pt: (layer, pt[b, g * P + j], 0, 0))

    row = pl.BlockSpec((SAMPLE_PAD, width), lambda b, g, pt: (b, 0))
    grid_spec = pltpu.PrefetchScalarGridSpec(
        num_scalar_prefetch=1,
        grid=(nb, n_pages // P),
        in_specs=[pl.BlockSpec((SAMPLE_PAD, width), lambda b, g, pt: (b, 0)),
                  row, row,
                  pl.BlockSpec((3, nrow, page), lambda b, g, pt: (0, 0, 0)),
                  pl.BlockSpec((4, DF_DK), lambda b, g, pt: (0, 0)),
                  pl.BlockSpec((1, width), lambda b, g, pt: (0, 0))]
                 + [kt_spec(j) for j in range(P)] + [v_spec(j) for j in range(P)],
        out_specs=pl.BlockSpec((SAMPLE_PAD, width), lambda b, g, pt: (b, 0)),
        scratch_shapes=[pltpu.VMEM((nrow, width), BF16), pltpu.VMEM((nrow, 1), F32),
                        pltpu.VMEM((nrow, 1), F32), pltpu.VMEM((nrow, DF_DV), F32)],
    )
    return pl.pallas_call(
        functools.partial(_dfd_kernel, pages=P, page=page, lam_init=lam_init),
        grid_spec=grid_spec,
        out_shape=jax.ShapeDtypeStruct((nb * SAMPLE_PAD, width), F32),
        compiler_params=_params(("parallel", "arbitrary")),
        name="dfd",
    )(page_table, dq, dk_new, dv_new, dbias, lw['df_lambda'], lw['df_norm'],
      *([cache_kt] * P), *([cache_v] * P))


def _mld_kernel(pt_ref, q_ref, cn_ref, rn_ref, mask_ref, wuk_ref, wuva_ref, wuvb_ref, *rest, pages, page):
    c_refs = rest[0:pages]
    r_refs = rest[pages:2 * pages]
    o_ref, ql_scr, qr_scr, m_scr, l_scr, acc_scr = rest[2 * pages:]
    pg = pl.program_id(1)
    last = pl.num_programs(1) - 1

    @pl.when(pg == 0)
    def _():
        for h in range(MLA_HEADS):
            sl = slice(h * HEAD_PAD, (h + 1) * HEAD_PAD)
            qh = q_ref[:, sl].astype(BF16)
            rows = slice(h * SAMPLE_PAD, (h + 1) * SAMPLE_PAD)
            ql_scr[rows, :] = lax.dot_general(qh, wuk_ref[:, sl], _NT, preferred_element_type=F32)
            qr_scr[rows, :] = q_ref[:, h * HEAD_PAD + MLA_NOPE:h * HEAD_PAD + MLA_NOPE + MLA_ROPE]
        m_scr[...] = jnp.full(m_scr.shape, NEG_INF, F32)
        l_scr[...] = jnp.zeros(l_scr.shape, F32)
        acc_scr[...] = jnp.zeros(acc_scr.shape, F32)

    ql = ql_scr[...].astype(BF16)
    qr = qr_scr[...].astype(BF16)

    cb = [c_refs[j][...].astype(BF16) for j in range(pages)]
    s = jnp.concatenate(
        [lax.dot_general(ql, cb[j], _NT, preferred_element_type=F32)
         + jnp.dot(qr, r_refs[j][...].astype(BF16), preferred_element_type=F32) for j in range(pages)], axis=1)
    c_all = jnp.concatenate(cb, axis=0)
    carry = _softmax_next((m_scr[...], l_scr[...], acc_scr[...]), s,
                          lambda p: jnp.dot(p, c_all, preferred_element_type=F32))
    m_scr[...], l_scr[...], acc_scr[...] = carry

    @pl.when(pg == last)
    def _():
        cn = _pad_rows(cn_ref[...], page).astype(BF16)
        rn = _pad_rows(rn_ref[...], page).astype(BF16)
        s_new = (lax.dot_general(ql, cn, _NT, preferred_element_type=F32)
                 + lax.dot_general(qr, rn, _NT, preferred_element_type=F32) + mask_ref[...])
        m, l, acc = _softmax_next(carry, s_new, lambda p: jnp.dot(p, cn, preferred_element_type=F32))
        o = (acc / l).astype(BF16)
        for p in range(MLA_HEADS // 2):
            c = slice(p * 2 * MLA_V, (p + 1) * 2 * MLA_V)
            ra = slice(2 * p * SAMPLE_PAD, (2 * p + 1) * SAMPLE_PAD)
            rb = slice((2 * p + 1) * SAMPLE_PAD, (2 * p + 2) * SAMPLE_PAD)
            y = (jnp.dot(o[ra], wuva_ref[:, c], preferred_element_type=F32)
                 + jnp.dot(o[rb], wuvb_ref[:, c], preferred_element_type=F32))
            o_ref[:, c] = y


def _mld(layer, qm, ckv_new, kr_new, new_mask, cache_c, cache_rt, page_table, lw):
    nb, n_pages = page_table.shape
    page = cache_c.shape[2]
    P = math.gcd(MLD_PAGES, n_pages)
    nrow = MLA_HEADS * SAMPLE_PAD
    qw = MLA_HEADS * HEAD_PAD
    ow = MLA_HEADS * MLA_V

    def page_spec(rows, cols, j):
        return pl.BlockSpec((None, None, rows, cols), lambda b, g, pt: (layer, pt[b, g * P + j], 0, 0))

    const = lambda shape: pl.BlockSpec(shape, lambda b, g, pt: (0,) * len(shape))
    grid_spec = pltpu.PrefetchScalarGridSpec(
        num_scalar_prefetch=1,
        grid=(nb, n_pages // P),
        in_specs=[pl.BlockSpec((SAMPLE_PAD, qw), lambda b, g, pt: (b, 0)),
                  pl.BlockSpec((SAMPLE_PAD, MLA_KV_LORA), lambda b, g, pt: (b, 0)),
                  pl.BlockSpec((SAMPLE_PAD, MLA_ROPE), lambda b, g, pt: (b, 0)),
                  const((nrow, page)), const((MLA_KV_LORA, qw)), const((MLA_KV_LORA, ow)),
                  const((MLA_KV_LORA, ow))]
                 + [page_spec(page, MLA_KV_LORA, j) for j in range(P)]
                 + [page_spec(MLA_ROPE, page, j) for j in range(P)],
        out_specs=pl.BlockSpec((SAMPLE_PAD, ow), lambda b, g, pt: (b, 0)),
        scratch_shapes=[pltpu.VMEM((nrow, MLA_KV_LORA), F32), pltpu.VMEM((nrow, MLA_ROPE), F32),
                        pltpu.VMEM((nrow, 1), F32), pltpu.VMEM((nrow, 1), F32),
                        pltpu.VMEM((nrow, MLA_KV_LORA), F32)],
    )
    return pl.pallas_call(
        functools.partial(_mld_kernel, pages=P, page=page),
        grid_spec=grid_spec,
        out_shape=jax.ShapeDtypeStruct((nb * SAMPLE_PAD, ow), F32),
        compiler_params=_params(("parallel", "arbitrary")),
        name="mld",
    )(page_table, qm, ckv_new, kr_new, new_mask, lw['wuk'], lw['wuva'], lw['wuvb'],
      *([cache_c] * P), *([cache_rt] * P))


def _post_kernel(x_ref, ym_ref, yd_ref, yc_ref, gate_ref, g_ref, wbm, wbd, wbc, wout, wfi, wfo, o_ref):
    u = None
    for j, (y_ref, w_ref) in enumerate(((ym_ref, wbm), (yd_ref, wbd), (yc_ref, wbc))):
        t = gate_ref[:, j * D_MODEL:(j + 1) * D_MODEL].astype(F32) * jnp.dot(
            y_ref[...].astype(BF16), w_ref[...], preferred_element_type=F32)
        u = t if u is None else u + t
    x = x_ref[...] + _rms(jnp.dot(u.astype(BF16), wout[...], preferred_element_type=F32), g_ref[0:1])
    hf = _rms(x, g_ref[1:2]).astype(BF16)
    cw = D_FF // 2
    t = None
    for j in range(2):
        a = jnp.dot(hf, wfi[:, j * cw:(j + 1) * cw], preferred_element_type=F32)
        bb = jnp.dot(hf, wfi[:, D_FF + j * cw:D_FF + (j + 1) * cw], preferred_element_type=F32)
        s = (a * jax.nn.sigmoid(a) * bb).astype(BF16)
        d = jnp.dot(s, wfo[j * cw:(j + 1) * cw, :], preferred_element_type=F32)
        t = d if t is None else t + d
    o_ref[...] = x + _rms(t, g_ref[2:3])


def _post(x, ym, yd, yc, gates, lw):
    rows = x.shape[0]
    tm = min(ROW_TILE, rows)
    row = lambda width: pl.BlockSpec((tm, width), lambda i: (i, 0))
    weights = [lw['g123'], lw['w_br_ml'], lw['w_br_df'], lw['w_br_mla'], lw['w_out'], lw['w_ffn_in'], lw['w_ffn_out']]
    return pl.pallas_call(
        _post_kernel,
        grid=(rows // tm,),
        in_specs=[row(D_MODEL), row(512), row(512), row(512), row(N_BRANCH * D_MODEL)]
                 + [_const_spec(w.shape) for w in weights],
        out_specs=row(D_MODEL),
        out_shape=jax.ShapeDtypeStruct((rows, D_MODEL), F32),
        compiler_params=_params(("parallel",)),
        name="post",
    )(x, ym, yd, yc, gates, *weights)


def _rot_half_cols(w):
    half = w.shape[-1] // 2
    return jnp.concatenate([-w[..., half:], w[..., :half]], axis=-1)


def _head_pad(nope, rope):
    k, h = nope.shape[0], nope.shape[1]
    pad = jnp.zeros((k, h, HEAD_PAD - MLA_NOPE - MLA_ROPE), nope.dtype)
    return jnp.concatenate([nope, rope, pad], axis=-1).reshape(k, h * HEAD_PAD)


def _layer_weights(l, norm_gains, w_in, b_ml_gates, ml_head_norm, df_lambda, df_head_norm, mla_q_norm,
                   mla_kv_norm, w_uq, w_uk, w_uv, w_br_ml, w_br_df, w_br_mla, w_out, w_ffn_in, w_ffn_out):
    offs = np.cumsum((0,) + IN_SIZES)
    col = lambda i: w_in[l][:, offs[i]:offs[i + 1]]
    mq, mk, mv, mo, mi, mf, dq, dk, dv, cq, ckv, kr, gl = (col(i) for i in range(len(IN_SIZES)))
    zeros = lambda n: jnp.zeros((D_MODEL, n), F32)
    rope_grp = lambda w: jnp.concatenate([zeros(MLA_NOPE), w, zeros(HEAD_PAD - MLA_NOPE - MLA_ROPE)], axis=1)
    wmisc = jnp.concatenate([mi, mf, zeros(LANES - 2 * ML_HEADS), rope_grp(kr), rope_grp(_rot_half_cols(kr))], axis=1)
    uq = w_uq[l].reshape(MLA_Q_LORA, MLA_HEADS, MLA_NOPE + MLA_ROPE)
    uq_n, uq_r = uq[..., :MLA_NOPE], uq[..., MLA_NOPE:]
    uv = w_uv[l]
    zv = jnp.zeros_like(uv[:, 0::2])
    bf = lambda a: a.astype(BF16)
    return {
        'g0': norm_gains[l, 0:1], 'g123': norm_gains[l, 1:4],
        'wml': bf(jnp.concatenate([mq, mk, mv, mo], axis=1)), 'wdf': bf(jnp.concatenate([dq, dk, dv], axis=1)),
        'wcq': bf(cq), 'wckv': bf(ckv), 'wmisc': bf(wmisc), 'wgl': bf(gl),
        'q_norm': mla_q_norm[l][None], 'kv_norm': mla_kv_norm[l][None],
        'wuqa': bf(_head_pad(uq_n, uq_r)), 'wuqb': bf(_head_pad(jnp.zeros_like(uq_n), _rot_half_cols(uq_r))),
        'wuk': bf(_head_pad(w_uk[l], jnp.zeros((MLA_KV_LORA, MLA_HEADS, MLA_ROPE), F32))),
        'wdvt': bf(dv.T), 'wuvt': bf(uv.reshape(MLA_KV_LORA, MLA_HEADS * MLA_V).T),
        'wuva': bf(jnp.concatenate([uv[:, 0::2], zv], axis=-1).reshape(MLA_KV_LORA, MLA_HEADS * MLA_V)),
        'wuvb': bf(jnp.concatenate([zv, uv[:, 1::2]], axis=-1).reshape(MLA_KV_LORA, MLA_HEADS * MLA_V)),
        'gate_bias': jnp.concatenate([b_ml_gates[l], jnp.zeros((LANES - 2 * ML_HEADS,), F32)])[None],
        'ml_norm': ml_head_norm[l][None], 'df_lambda': df_lambda[l], 'df_norm': df_head_norm[l][None],
        'w_br_ml': bf(w_br_ml[l]), 'w_br_df': bf(w_br_df[l]), 'w_br_mla': bf(w_br_mla[l]),
        'w_out': bf(w_out[l]), 'w_ffn_in': bf(w_ffn_in[l]), 'w_ffn_out': bf(w_ffn_out[l]),
    }


def _rope_tables(pos):
    half = MLA_ROPE // 2
    freqs = ROPE_THETA ** (-jnp.arange(half, dtype=F32) / half)
    ang = pos.astype(F32)[:, None] * freqs[None, :]
    cos, sin = jnp.cos(ang), jnp.sin(ang)
    n = pos.shape[0]
    tail = jnp.zeros((n, HEAD_PAD - MLA_NOPE - MLA_ROPE), F32)
    cs = jnp.concatenate([jnp.ones((n, MLA_NOPE), F32), cos, cos, tail], axis=1)
    sn = jnp.concatenate([jnp.zeros((n, MLA_NOPE), F32), sin, sin, tail], axis=1)
    return cs, sn


def kernel(x_prompt, x_sample, state_mlstm_C, state_mlstm_n, state_mlstm_m, cache_diff_k, cache_diff_v,
           cache_mla_ckv, cache_mla_krope, page_table, norm_gains, w_in, b_ml_gates, ml_head_norm, df_lambda,
           df_head_norm, rel_bias, mla_q_norm, mla_kv_norm, w_uq, w_uk, w_uv, w_br_ml, w_br_df, w_br_mla, w_out,
           w_ffn_in, w_ffn_out):
    depth = w_in.shape[0]
    B, S, _ = x_prompt.shape
    DB, DS, _ = x_sample.shape
    n_pages = page_table.shape[1]
    n_pool, page = cache_diff_k.shape[1], cache_diff_k.shape[2]
    past_len = n_pages * page
    assert DS <= SAMPLE_PAD
    T = ATT_TILE

    cs_p, sn_p = _rope_tables(jnp.arange(S, dtype=jnp.int32))
    pos_s = past_len + jnp.arange(SAMPLE_PAD, dtype=jnp.int32)
    cs_s, sn_s = _rope_tables(jnp.tile(pos_s, min(DB, ROW_TILE // SAMPLE_PAD)))
    ii, jj = np.meshgrid(np.arange(T), np.arange(T), indexing='ij')
    diag, sub = _t5_bucket_np(jj - ii), _t5_bucket_np(T + jj - ii)
    bias_p = _bias_table(rel_bias, np.concatenate([diag, sub], axis=0)).reshape(DF_HEADS, 2, T, T)
    rr, kk = np.meshgrid(np.arange(SAMPLE_PAD), np.arange(page), indexing='ij')
    new_ok = (kk <= rr) & (kk < DS)
    idx_d = np.concatenate([np.full((SAMPLE_PAD, page), REL_BUCKETS - 1, np.int32),
                            _t5_bucket_np(page + rr - kk),
                            np.where(new_ok, _t5_bucket_np(rr - kk), -1)], axis=0)
    bias_d = _bias_table(rel_bias, idx_d).reshape(DF_HEADS, 3, 1, SAMPLE_PAD, page)
    bias_d = jnp.broadcast_to(bias_d, (DF_HEADS, 3, 2, SAMPLE_PAD, page))
    bias_d = jnp.transpose(bias_d, (1, 0, 2, 3, 4)).reshape(3, DF_HEADS * 2 * SAMPLE_PAD, page)
    mla_new_mask = jnp.asarray(np.tile(np.where(new_ok, 0.0, NEG_INF).astype(np.float32), (MLA_HEADS, 1)))

    ckt = jnp.transpose(cache_diff_k, (0, 1, 3, 4, 5, 2)).reshape(depth, n_pool, DF_HEADS * 2 * DF_DK, page)
    crt = jnp.transpose(cache_mla_krope, (0, 1, 3, 2))
    cv = cache_diff_v.reshape(depth, n_pool, page * DF_HEADS, DF_DV)

    xp = x_prompt.reshape(B * S, D_MODEL)
    xs = jnp.pad(x_sample, ((0, 0), (0, SAMPLE_PAD - DS), (0, 0))).reshape(DB * SAMPLE_PAD, D_MODEL)
    zero_c = jnp.zeros((B, ML_HEADS, ML_DK, ML_DV), F32)
    zero_n = jnp.zeros((B, ML_HEADS, ML_DK), F32)
    zero_m = jnp.zeros((B, ML_HEADS), F32)
    st_p, st_s = [], []
    for l in range(depth):
        lw = _layer_weights(l, norm_gains, w_in, b_ml_gates, ml_head_norm, df_lambda, df_head_norm, mla_q_norm,
                            mla_kv_norm, w_uq, w_uk, w_uv, w_br_ml, w_br_df, w_br_mla, w_out, w_ffn_in, w_ffn_out)
        lam_init = 0.8 - 0.6 * math.exp(-0.3 * l)

        (ml, gif, dq, dk, dv, qm, ckv, kr, gates, dkb, dvt, km, vmt) = _proj(xp, lw, cs_p, sn_p, True)
        ym, c_p, n_p, m_p = _mlstm(ml, gif, lw, zero_c, zero_n, zero_m, B, S, math.gcd(S, ML_CHUNK), S)
        yd = _dfp(dq, dkb, dvt, bias_p, rel_bias, lw, B, S, lam_init)
        yc = _mlp(qm, km, vmt, B, S)
        xp = _post(xp, ym, yd, yc, gates, lw)
        st_p.append((dk.reshape(B, S, DF_HEADS, 2, DF_DK), dv.reshape(B, S, DF_HEADS, DF_DV),
                     ckv.reshape(B, S, MLA_KV_LORA), kr.reshape(B, S, MLA_ROPE), c_p, n_p, m_p))

        (ml, gif, dq, dk, dv, qm, ckv, kr, gates) = _proj(xs, lw, cs_s, sn_s, False)
        ym, c_s, n_s, m_s = _mlstm(ml, gif, lw, state_mlstm_C[l], state_mlstm_n[l], state_mlstm_m[l],
                                   DB, SAMPLE_PAD, SAMPLE_PAD, DS)
        yd = _dfd(l, dq, dk, dv, bias_d, ckt, cv, page_table, lw, lam_init)
        yc = _mld(l, qm, ckv, kr, mla_new_mask, cache_mla_ckv, crt, page_table, lw)
        xs = _post(xs, ym, yd, yc, gates, lw)
        tok = lambda a, *tail: a.reshape((DB, SAMPLE_PAD) + tail)[:, :DS]
        st_s.append((tok(dk, DF_HEADS, 2, DF_DK), tok(dv, DF_HEADS, DF_DV), tok(ckv, MLA_KV_LORA),
                     tok(kr, MLA_ROPE), c_s, n_s, m_s))

    outs_p = [jnp.stack(a) for a in zip(*st_p)]
    outs_s = [jnp.stack(a) for a in zip(*st_s)]
    yp = xp.reshape(B, S, D_MODEL)
    ys = xs.reshape(DB, SAMPLE_PAD, D_MODEL)[:, :DS]
    return (yp, ys, *outs_p, *outs_s)
```

```python
import functools
import math

import numpy as np
import jax
import jax.numpy as jnp
from jax import lax
from jax.experimental import pallas as pl
from jax.experimental.pallas import tpu as pltpu

F32 = jnp.float32
BF16 = jnp.bfloat16

D_MODEL = 1024
ML_HEADS = 4
ML_DK = 128
ML_DV = 128
DF_HEADS = 4
DF_DK = 64
DF_DV = 2 * DF_DK
MLA_HEADS = 8
MLA_NOPE = 64
MLA_ROPE = 32
MLA_V = 64
MLA_Q_LORA = 384
MLA_KV_LORA = 256
ROPE_THETA = 10000.0
REL_BUCKETS = 32
REL_MAX_DIST = 128
N_BRANCH = 3
D_FF = 2816
EPS = 1e-6
NEG_INF = -1e30
LOG2E = math.log2(math.e)
IN_SIZES = (ML_HEADS * ML_DK, ML_HEADS * ML_DK, ML_HEADS * ML_DV, ML_HEADS * ML_DV, ML_HEADS, ML_HEADS,
            DF_HEADS * 2 * DF_DK, DF_HEADS * 2 * DF_DK, DF_HEADS * DF_DV,
            MLA_Q_LORA, MLA_KV_LORA, MLA_ROPE,
            N_BRANCH * D_MODEL)

LANES = 128
SUBLANES = 8
HEAD_PAD = 128
ROW_TILE = 256
ATT_TILE = 512
MLP_HEADS_PER_STEP = 4
ML_CHUNK = 256
SAMPLE_PAD = 8
DFD_PAGES = 16
MLD_PAGES = 32
MLD_CHAINS = 2
VMEM_LIMIT = 56 * 1024 * 1024

_NT = (((1,), (1,)), ((), ()))
_TN = (((0,), (0,)), ((), ()))


def _params(sem):
    return pltpu.CompilerParams(dimension_semantics=sem, vmem_limit_bytes=VMEM_LIMIT)


def _const_spec(shape):
    nd = len(shape)
    return pl.BlockSpec(shape, lambda *_: (0,) * nd, pipeline_mode=pl.Buffered(1))


def _rms(x, g):
    return x * lax.rsqrt(jnp.mean(x * x, axis=-1, keepdims=True) + EPS) * g


def _t5_bucket_np(n):
    n = np.asarray(n, np.int64)
    exact = REL_BUCKETS // 2
    nf = np.maximum(n, 1).astype(np.float32)
    large = exact + (np.log(nf / np.float32(exact)) / np.float32(math.log(REL_MAX_DIST / exact))
                     * np.float32(REL_BUCKETS - exact)).astype(np.int32)
    large = np.minimum(large, REL_BUCKETS - 1)
    b = np.where(n < exact, n, large)
    return np.where(n < 0, -1, b).astype(np.int32)


def _bias_table_kernel(rb_ref, idx_ref, out_ref):
    idx = idx_ref[...]
    for h in range(DF_HEADS):
        acc = jnp.full(idx.shape, NEG_INF, F32)
        for b in range(REL_BUCKETS):
            acc = jnp.where(idx == b, rb_ref[b, h] * LOG2E, acc)
        out_ref[h] = acc


def _bias_table(rel_bias, idx_np):
    r, c = idx_np.shape
    return pl.pallas_call(
        _bias_table_kernel,
        out_shape=jax.ShapeDtypeStruct((DF_HEADS, r, c), F32),
        in_specs=[pl.BlockSpec(memory_space=pltpu.SMEM), pl.BlockSpec(memory_space=pltpu.VMEM)],
        out_specs=pl.BlockSpec(memory_space=pltpu.VMEM),
        name="bias_table",
    )(rel_bias, jnp.asarray(idx_np))


def _proj_kernel(x_ref, g_ref, cs_ref, sn_ref, wml, wdf, wcq, wckv, wmisc, wgl, qn_ref, kvn_ref,
                 wuqa, wuqb, wuk, wdvt, wuvt,
                 ml_o, gif_o, dq_o, dk_o, dv_o, qm_o, ckv_o, kr_o, gate_o, *prompt_outs, prompt):
    x = x_ref[...]
    h = _rms(x, g_ref[...]).astype(BF16)

    def mm(w_ref, lo, hi):
        return jnp.dot(h, w_ref[:, lo:hi], preferred_element_type=F32)

    w = ML_HEADS * ML_DK
    act = ml_o.dtype
    ml_o[:, 0:w] = mm(wml, 0, w).astype(act)
    ml_o[:, w:2 * w] = (mm(wml, w, 2 * w) * ML_DK ** -0.5).astype(act)
    ml_o[:, 2 * w:3 * w] = mm(wml, 2 * w, 3 * w).astype(act)
    ml_o[:, 3 * w:4 * w] = jax.nn.sigmoid(mm(wml, 3 * w, 4 * w)).astype(act)

    misc = mm(wmisc, 0, 3 * LANES)
    gif_o[...] = misc[:, 0:LANES]
    cs = cs_ref[...]
    sn = sn_ref[...]
    krp = misc[:, LANES:2 * LANES] * cs + misc[:, 2 * LANES:3 * LANES] * sn
    kr_o[...] = krp[:, MLA_NOPE:MLA_NOPE + MLA_ROPE]

    wd = DF_HEADS * 2 * DF_DK
    dq_o[...] = (mm(wdf, 0, wd) * (DF_DK ** -0.5 * LOG2E)).astype(act)
    dk = mm(wdf, wd, 2 * wd)
    dv = mm(wdf, 2 * wd, 3 * wd)
    dk_o[...] = dk
    dv_o[...] = dv

    c_q = _rms(mm(wcq, 0, MLA_Q_LORA), qn_ref[...]).astype(BF16)
    c_kv = _rms(mm(wckv, 0, MLA_KV_LORA), kvn_ref[...])
    ckv_o[...] = c_kv
    scale = (MLA_NOPE + MLA_ROPE) ** -0.5 * LOG2E
    for hh in range(MLA_HEADS):
        sl = slice(hh * HEAD_PAD, (hh + 1) * HEAD_PAD)
        qa = jnp.dot(c_q, wuqa[:, sl], preferred_element_type=F32)
        qb = jnp.dot(c_q, wuqb[:, sl], preferred_element_type=F32)
        qm_o[:, sl] = ((qa * cs + qb * sn) * scale).astype(act)

    gw = 512
    for j in range(N_BRANCH * D_MODEL // gw):
        gate_o[:, j * gw:(j + 1) * gw] = jax.nn.sigmoid(mm(wgl, j * gw, (j + 1) * gw)).astype(BF16)

    if prompt:
        dkb_o, dvt_o, km_o, vmt_o = prompt_outs
        dkb_o[...] = dk.astype(BF16)
        dvt_o[...] = lax.dot_general(wdvt[...], h, _NT, preferred_element_type=F32).astype(BF16)
        ckb = c_kv.astype(BF16)
        for hh in range(MLA_HEADS):
            sl = slice(hh * HEAD_PAD, (hh + 1) * HEAD_PAD)
            kn = jnp.dot(ckb, wuk[:, sl], preferred_element_type=F32)
            km_o[:, sl] = (kn + krp).astype(BF16)
        vmt_o[...] = lax.dot_general(wuvt[...], ckb, _NT, preferred_element_type=F32).astype(BF16)


def _proj(x, lw, cs, sn, prompt):
    rows = x.shape[0]
    tm = min(ROW_TILE, rows)
    assert rows % tm == 0
    row = lambda width: pl.BlockSpec((tm, width), lambda i: (i, 0))
    assert cs.shape[0] % tm == 0 and rows % cs.shape[0] == 0
    pos_tiles = cs.shape[0] // tm
    pos_row = pl.BlockSpec((tm, LANES), lambda i: (i % pos_tiles, 0))
    weights = [lw['wml'], lw['wdf'], lw['wcq'], lw['wckv'], lw['wmisc'], lw['wgl'], lw['q_norm'], lw['kv_norm'],
               lw['wuqa'], lw['wuqb'], lw['wuk'], lw['wdvt'], lw['wuvt']]
    act = BF16 if prompt else F32
    out_widths = [(4 * ML_HEADS * ML_DK, act), (LANES, F32), (512, act), (512, F32), (512, F32),
                  (MLA_HEADS * HEAD_PAD, act), (MLA_KV_LORA, F32), (MLA_ROPE, F32), (N_BRANCH * D_MODEL, BF16)]
    out_specs = [row(wd) for wd, _ in out_widths]
    out_shape = [jax.ShapeDtypeStruct((rows, wd), dt) for wd, dt in out_widths]
    if prompt:
        col = lambda height: pl.BlockSpec((height, tm), lambda i: (0, i))
        out_specs += [row(512), col(512), row(MLA_HEADS * HEAD_PAD), col(512)]
        out_shape += [jax.ShapeDtypeStruct((rows, 512), BF16), jax.ShapeDtypeStruct((512, rows), BF16),
                      jax.ShapeDtypeStruct((rows, MLA_HEADS * HEAD_PAD), BF16),
                      jax.ShapeDtypeStruct((512, rows), BF16)]
    return pl.pallas_call(
        functools.partial(_proj_kernel, prompt=prompt),
        grid=(rows // tm,),
        in_specs=[row(D_MODEL), _const_spec((1, D_MODEL)), pos_row, pos_row]
                 + [_const_spec(w.shape) for w in weights],
        out_specs=out_specs,
        out_shape=out_shape,
        compiler_params=_params(("parallel",)),
        name="proj_prompt" if prompt else "proj_sample",
    )(x, lw['g0'], cs, sn, *weights)


def _mlstm_kernel(m0_ref, ml_ref, gif_ref, gb_ref, hn_ref, c0_ref, n0_ref,
                  ym_o, c_o, n_o, m_o, cn_scr, m_scr, *, chunk, t_valid):
    b = pl.program_id(0)
    c = pl.program_id(1)
    L = chunk
    dk, dv = ML_DK, ML_DV
    lane = lax.broadcasted_iota(jnp.int32, (1, LANES), 1)

    @pl.when(c == 0)
    def _():
        for h in range(ML_HEADS):
            ncol = jnp.where(lax.broadcasted_iota(jnp.int32, (dk, LANES), 1) == 0, n0_ref[0, h], 0.0)
            cn_scr[h] = jnp.concatenate([c0_ref[0, h], ncol], axis=1)
            m_scr[h] = jnp.full((SUBLANES, LANES), m0_ref[b, h], F32)

    g = gif_ref[...] + gb_ref[...]
    g = jnp.where(lane < ML_HEADS, g, jnp.minimum(g, 0.0) - jnp.log(1.0 + jnp.exp(-jnp.abs(g))))
    row_i = lax.broadcasted_iota(jnp.int32, (L, L), 0)
    col_i = lax.broadcasted_iota(jnp.int32, (L, L), 1)
    if t_valid < L:
        tok = lax.broadcasted_iota(jnp.int32, (L, LANES), 0)
        g = jnp.where(tok < t_valid, g, jnp.where(lane < ML_HEADS, NEG_INF, 0.0))
    causal = col_i <= row_i
    eye = col_i == row_i

    for h in range(ML_HEADS):
        lic = g[:, h:h + 1]
        lfc = g[:, ML_HEADS + h:ML_HEADS + h + 1]
        br = jnp.sum(jnp.where(row_i <= col_i, lfc, 0.0), axis=0, keepdims=True)
        bc = jnp.sum(jnp.where(eye, br, 0.0), axis=1, keepdims=True)
        ir = jnp.sum(jnp.where(eye, lic, 0.0), axis=0, keepdims=True)
        b_last = jnp.sum(lfc, axis=0, keepdims=True)
        m_prev = m_scr[h][0:1, 0:1]

        q = ml_ref[:, h * dk:(h + 1) * dk].astype(BF16)
        k = ml_ref[:, (ML_HEADS + h) * dk:(ML_HEADS + h + 1) * dk].astype(BF16)
        v = ml_ref[:, (2 * ML_HEADS + h) * dk:(2 * ML_HEADS + h + 1) * dk].astype(BF16)
        og = ml_ref[:, (3 * ML_HEADS + h) * dk:(3 * ML_HEADS + h + 1) * dk]

        dmat = jnp.where(causal, bc - br + ir, NEG_INF)
        inter = bc + m_prev
        mt = jnp.maximum(inter, jnp.max(dmat, axis=1, keepdims=True))
        wgt = jnp.exp(dmat - mt)
        iw = jnp.exp(inter - mt)
        a = lax.dot_general(q, k, _NT, preferred_element_type=F32) * wgt
        cn = cn_scr[h]
        qc = jnp.dot(q, cn.astype(BF16), preferred_element_type=F32)
        num = jnp.dot(a.astype(BF16), v, preferred_element_type=F32) + iw * qc[:, 0:dv]
        den = jnp.sum(a, axis=1, keepdims=True) + iw * qc[:, dv:dv + 1]
        hh = num / jnp.maximum(jnp.abs(den), jnp.exp(-mt))
        y = _rms(hh, hn_ref[:, h * dv:(h + 1) * dv]) * og.astype(F32)
        ym_o[:, h * dv:(h + 1) * dv] = y.astype(ym_o.dtype)

        m_new = jnp.maximum(b_last + m_prev, jnp.max(b_last - br + ir, axis=1, keepdims=True))
        wl = jnp.exp(b_last - bc + lic - m_new)
        dec = jnp.exp(b_last + m_prev - m_new)
        ones_col = jnp.where(lax.broadcasted_iota(jnp.int32, (L, LANES), 1) == 0, wl, 0.0)
        wv = jnp.concatenate([wl * v.astype(F32), ones_col], axis=1).astype(BF16)
        cn_new = dec * cn + lax.dot_general(k, wv, _TN, preferred_element_type=F32)
        cn_scr[h] = cn_new
        m_scr[h] = jnp.broadcast_to(m_new, (SUBLANES, LANES))
        c_o[0, h] = cn_new[:, 0:dv]
        n_o[0, h] = cn_new[:, dv:dv + 1]
        m_o[0, h] = jnp.broadcast_to(m_new, (SUBLANES, LANES))


def _mlstm(ml, gif, lw, c0, n0, m0, batch, seq, chunk, t_valid):
    nc = seq // chunk
    assert seq % chunk == 0
    width = ML_HEADS * ML_DV
    grid_spec = pltpu.PrefetchScalarGridSpec(
        num_scalar_prefetch=1,
        grid=(batch, nc),
        in_specs=[
            pl.BlockSpec((chunk, 4 * width), lambda b, c, m: (b * nc + c, 0)),
            pl.BlockSpec((chunk, LANES), lambda b, c, m: (b * nc + c, 0)),
            pl.BlockSpec((1, LANES), lambda b, c, m: (0, 0)),
            pl.BlockSpec((1, width), lambda b, c, m: (0, 0)),
            pl.BlockSpec((1, ML_HEADS, ML_DK, ML_DV), lambda b, c, m: (b, 0, 0, 0)),
            pl.BlockSpec((1, ML_HEADS, ML_DK, 1), lambda b, c, m: (b, 0, 0, 0)),
        ],
        out_specs=[
            pl.BlockSpec((chunk, width), lambda b, c, m: (b * nc + c, 0)),
            pl.BlockSpec((1, ML_HEADS, ML_DK, ML_DV), lambda b, c, m: (b, 0, 0, 0)),
            pl.BlockSpec((1, ML_HEADS, ML_DK, 1), lambda b, c, m: (b, 0, 0, 0)),
            pl.BlockSpec((1, ML_HEADS, SUBLANES, LANES), lambda b, c, m: (b, 0, 0, 0)),
        ],
        scratch_shapes=[pltpu.VMEM((ML_HEADS, ML_DK, 2 * ML_DV), F32),
                        pltpu.VMEM((ML_HEADS, SUBLANES, LANES), F32)],
    )
    ym, c_new, n_new, m_new = pl.pallas_call(
        functools.partial(_mlstm_kernel, chunk=chunk, t_valid=t_valid),
        grid_spec=grid_spec,
        out_shape=[jax.ShapeDtypeStruct((batch * seq, width), ml.dtype),
                   jax.ShapeDtypeStruct((batch, ML_HEADS, ML_DK, ML_DV), F32),
                   jax.ShapeDtypeStruct((batch, ML_HEADS, ML_DK, 1), F32),
                   jax.ShapeDtypeStruct((batch, ML_HEADS, SUBLANES, LANES), F32)],
        compiler_params=_params(("parallel", "arbitrary")),
        name="mlstm",
    )(m0, ml, gif, lw['gate_bias'], lw['ml_norm'], c0, n0[..., None])
    return ym, c_new, n_new[..., 0], m_new[:, :, 0, 0]


def _softmax_first(s, pv, axis=1):
    m = jnp.max(s, axis=axis, keepdims=True)
    p = jnp.exp2(s - m)
    return m, jnp.sum(p, axis=axis, keepdims=True), pv(p.astype(BF16))


def _softmax_next(carry, s, pv, axis=1, shift=None):
    m, l, acc = carry
    top = jnp.max(s, axis=axis, keepdims=True)
    m_new = jnp.maximum(m, top if shift is None else top + shift)
    alpha = jnp.exp2(m - m_new)
    p = jnp.exp2(s - (m_new if shift is None else m_new - shift))
    return m_new, alpha * l + jnp.sum(p, axis=axis, keepdims=True), alpha * acc + pv(p.astype(BF16))


def _lambda(lv_ref, lam_init):
    lv = lv_ref[...]
    e1 = jnp.exp(jnp.sum(lv[0:1] * lv[1:2], axis=1, keepdims=True))
    e2 = jnp.exp(jnp.sum(lv[2:3] * lv[3:4], axis=1, keepdims=True))
    return e1 - e2 + lam_init


def _dfp_kernel(rb_ref, q_ref, k_ref, vt_ref, bias_ref, lv_ref, hn_ref, o_ref, *, tile, lam_init):
    h = pl.program_id(1)
    qi = pl.program_id(2)
    T = tile
    q = q_ref[...]
    lane = lax.broadcasted_iota(jnp.int32, (T, 2 * DF_DK), 1)
    zero = jnp.zeros_like(q)
    qs = (jnp.where(lane < DF_DK, q, zero), jnp.where(lane >= DF_DK, q, zero))

    def tile_scores(start):
        kt = k_ref[pl.ds(start, T), :]
        vt = vt_ref[:, pl.ds(start, T)]
        s = [lax.dot_general(kt, qm, _NT, preferred_element_type=F32) for qm in qs]
        return s, lambda p: jnp.dot(vt, p, preferred_element_type=F32)

    s, pv = tile_scores(pl.multiple_of(qi * T, T))
    carry = tuple(_softmax_first(sm + bias_ref[0, 0], pv, axis=0) for sm in s)
    sub = jnp.maximum(qi - 1, 0)
    s, pv = tile_scores(pl.multiple_of(sub * T, T))
    sub_bias = bias_ref[0, 1] + jnp.where(qi == 0, NEG_INF, 0.0)
    carry = tuple(_softmax_next(c, sm + sub_bias, pv, axis=0) for c, sm in zip(carry, s))
    far_bias = rb_ref[REL_BUCKETS - 1, h] * LOG2E

    def far(ki, carry):
        s, pv = tile_scores(pl.multiple_of(ki * T, T))
        return tuple(_softmax_next(c, sm, pv, axis=0, shift=far_bias) for c, sm in zip(carry, s))

    (_, l0, acc0), (_, l1, acc1) = lax.fori_loop(0, sub, far, carry)
    lam = _lambda(lv_ref, lam_init)
    od = (acc0 / l0 - lam * (acc1 / l1)).T
    o_ref[...] = (_rms(od, hn_ref[...]) * (1.0 - lam_init)).astype(BF16)


def _dfp(dq, dkb, dvt, bias_tiles, rel_bias, lw, batch, seq, lam_init):
    T = ATT_TILE
    nq = seq // T
    assert seq % T == 0 and T >= REL_MAX_DIST
    hw = 2 * DF_DK
    grid_spec = pltpu.PrefetchScalarGridSpec(
        num_scalar_prefetch=0,
        grid=(batch, DF_HEADS, nq),
        in_specs=[
            pl.BlockSpec(memory_space=pltpu.SMEM),
            pl.BlockSpec((T, hw), lambda b, h, i: (b * nq + i, h)),
            pl.BlockSpec((seq, hw), lambda b, h, i: (b, h)),
            pl.BlockSpec((DF_DV, seq), lambda b, h, i: (h, b)),
            pl.BlockSpec((1, 2, T, T), lambda b, h, i: (h, 0, 0, 0)),
            pl.BlockSpec((4, DF_DK), lambda b, h, i: (0, 0)),
            pl.BlockSpec((1, hw), lambda b, h, i: (0, h)),
        ],
        out_specs=pl.BlockSpec((T, hw), lambda b, h, i: (b * nq + i, h)),
    )
    return pl.pallas_call(
        functools.partial(_dfp_kernel, tile=T, lam_init=lam_init),
        grid_spec=grid_spec,
        out_shape=jax.ShapeDtypeStruct((batch * seq, DF_HEADS * hw), BF16),
        compiler_params=_params(("parallel", "parallel", "arbitrary")),
        name="dfp",
    )(rel_bias, dq, dkb, dvt, bias_tiles, lw['df_lambda'], lw['df_norm'])


def _mlp_kernel(q_ref, k_ref, vt_ref, o_ref, *, tile, heads):
    qi = pl.program_id(2)
    T = tile
    qs = [q_ref[:, j * HEAD_PAD:(j + 1) * HEAD_PAD] for j in range(heads)]
    key_i = lax.broadcasted_iota(jnp.int32, (T, T), 0)
    qry_i = lax.broadcasted_iota(jnp.int32, (T, T), 1)
    mask = jnp.where(key_i <= qry_i, 0.0, NEG_INF)
    vw = 2 * MLA_V

    def tile_scores(start):
        kt = k_ref[pl.ds(start, T), :]
        s = [lax.dot_general(kt[:, j * HEAD_PAD:(j + 1) * HEAD_PAD], qs[j], _NT, preferred_element_type=F32)
             for j in range(heads)]
        pvs = [lambda p, j=j: jnp.dot(vt_ref[(j // 2) * vw:(j // 2 + 1) * vw, pl.ds(start, T)], p,
                                      preferred_element_type=F32) for j in range(heads)]
        return s, pvs

    s, pvs = tile_scores(pl.multiple_of(qi * T, T))
    carry = tuple(_softmax_first(sm + mask, pv, axis=0) for sm, pv in zip(s, pvs))

    def far(ki, carry):
        s, pvs = tile_scores(pl.multiple_of(ki * T, T))
        return tuple(_softmax_next(c, sm, pv, axis=0) for c, sm, pv in zip(carry, s, pvs))

    carry = lax.fori_loop(0, qi, far, carry)
    row = lax.broadcasted_iota(jnp.int32, (vw, T), 0)
    for p in range(heads // 2):
        (_, la, acca), (_, lb, accb) = carry[2 * p], carry[2 * p + 1]
        o_ref[:, p * vw:(p + 1) * vw] = jnp.where(row < MLA_V, acca / la, accb / lb).T.astype(BF16)


def _mlp(qm, km, vmt, batch, seq):
    T = ATT_TILE
    nq = seq // T
    hs = MLP_HEADS_PER_STEP
    pw = hs * HEAD_PAD
    vw = hs * MLA_V
    return pl.pallas_call(
        functools.partial(_mlp_kernel, tile=T, heads=hs),
        grid=(batch, MLA_HEADS // hs, nq),
        in_specs=[
            pl.BlockSpec((T, pw), lambda b, p, i: (b * nq + i, p)),
            pl.BlockSpec((seq, pw), lambda b, p, i: (b, p)),
            pl.BlockSpec((vw, seq), lambda b, p, i: (p, b)),
        ],
        out_specs=pl.BlockSpec((T, vw), lambda b, p, i: (b * nq + i, p)),
        out_shape=jax.ShapeDtypeStruct((batch * seq, MLA_HEADS * MLA_V), BF16),
        compiler_params=_params(("parallel", "parallel", "arbitrary")),
        name="mlp",
    )(qm, km, vmt)


def _pad_rows(x, rows):
    return jnp.concatenate([x, jnp.zeros((rows - x.shape[0], x.shape[1]), x.dtype)], axis=0)


def _dfd_kernel(pt_ref, q_ref, kn_ref, vn_ref, bias_ref, lv_ref, hn_ref, *rest, pages, page, lam_init):
    k_refs = rest[0:pages]
    v_refs = rest[pages:2 * pages]
    o_ref, q_scr, m_scr, l_scr, acc_scr = rest[2 * pages:]
    pg = pl.program_id(1)
    last = pl.num_programs(1) - 1
    nrow = DF_HEADS * 2 * SAMPLE_PAD
    width = DF_HEADS * 2 * DF_DK

    @pl.when(pg == 0)
    def _():
        q = q_ref[...].astype(F32)
        qt = jnp.concatenate([q] * (DF_HEADS * 2), axis=0)
        rblk = lax.broadcasted_iota(jnp.int32, (nrow, width), 0) // SAMPLE_PAD
        cblk = lax.broadcasted_iota(jnp.int32, (nrow, width), 1) // DF_DK
        q_scr[...] = jnp.where(rblk == cblk, qt, 0.0).astype(BF16)
        m_scr[...] = jnp.full(m_scr.shape, NEG_INF, F32)
        l_scr[...] = jnp.zeros(l_scr.shape, F32)
        acc_scr[...] = jnp.zeros(acc_scr.shape, F32)

    qbd = q_scr[...]
    hrows = 2 * SAMPLE_PAD

    def pv_heads(p, v_of_head):
        return jnp.concatenate(
            [jnp.dot(p[h * hrows:(h + 1) * hrows], v_of_head(h), preferred_element_type=F32)
             for h in range(DF_HEADS)], axis=0)

    far = bias_ref[0]
    near = jnp.where(pg == last, bias_ref[1], far)
    s = jnp.concatenate(
        [jnp.dot(qbd, k_refs[j][...].astype(BF16), preferred_element_type=F32)
         + (near if j == pages - 1 else far) for j in range(pages)], axis=1)
    v_all = [jnp.concatenate([v_refs[j][pl.ds(h, page, stride=DF_HEADS), :].astype(BF16) for j in range(pages)],
                             axis=0) for h in range(DF_HEADS)]
    carry = _softmax_next((m_scr[...], l_scr[...], acc_scr[...]), s, lambda p: pv_heads(p, lambda h: v_all[h]))
    m_scr[...], l_scr[...], acc_scr[...] = carry

    @pl.when(pg == last)
    def _():
        kn = _pad_rows(kn_ref[...], page).astype(BF16)
        vn = _pad_rows(vn_ref[...], page).astype(BF16)
        s_new = lax.dot_general(qbd, kn, _NT, preferred_element_type=F32) + bias_ref[2]
        m, l, acc = _softmax_next(
            carry, s_new, lambda p: pv_heads(p, lambda h: vn[:, h * DF_DV:(h + 1) * DF_DV]))
        o = acc / l
        lam = _lambda(lv_ref, lam_init)
        for h in range(DF_HEADS):
            r0 = h * hrows
            c = slice(h * DF_DV, (h + 1) * DF_DV)
            od = o[r0:r0 + SAMPLE_PAD] - lam * o[r0 + SAMPLE_PAD:r0 + hrows]
            o_ref[:, c] = _rms(od, hn_ref[:, c]) * (1.0 - lam_init)


def _dfd(layer, dq, dk_new, dv_new, dbias, cache_kt, cache_v, page_table, lw, lam_init):
    nb, n_pages = page_table.shape
    page = cache_kt.shape[3]
    P = math.gcd(DFD_PAGES, n_pages)
    width = DF_HEADS * 2 * DF_DK
    nrow = DF_HEADS * 2 * SAMPLE_PAD
    assert page == LANES

    def kt_spec(j):
        return pl.BlockSpec((None, None, width, page), lambda b, g, pt: (layer, pt[b, g * P + j], 0, 0))

    def v_spec(j):
        return pl.BlockSpec((None, None, page * DF_HEADS, DF_DV),
                            lambda b, g, pt: (layer, pt[b, g * P + j], 0, 0))

    row = pl.BlockSpec((SAMPLE_PAD, width), lambda b, g, pt: (b, 0))
    grid_spec = pltpu.PrefetchScalarGridSpec(
        num_scalar_prefetch=1,
        grid=(nb, n_pages // P),
        in_specs=[pl.BlockSpec((SAMPLE_PAD, width), lambda b, g, pt: (b, 0)),
                  row, row,
                  pl.BlockSpec((3, nrow, page), lambda b, g, pt: (0, 0, 0)),
                  pl.BlockSpec((4, DF_DK), lambda b, g, pt: (0, 0)),
                  pl.BlockSpec((1, width), lambda b, g, pt: (0, 0))]
                 + [kt_spec(j) for j in range(P)] + [v_spec(j) for j in range(P)],
        out_specs=pl.BlockSpec((SAMPLE_PAD, width), lambda b, g, pt: (b, 0)),
        scratch_shapes=[pltpu.VMEM((nrow, width), BF16), pltpu.VMEM((nrow, 1), F32),
                        pltpu.VMEM((nrow, 1), F32), pltpu.VMEM((nrow, DF_DV), F32)],
    )
    return pl.pallas_call(
        functools.partial(_dfd_kernel, pages=P, page=page, lam_init=lam_init),
        grid_spec=grid_spec,
        out_shape=jax.ShapeDtypeStruct((nb * SAMPLE_PAD, width), F32),
        compiler_params=_params(("parallel", "arbitrary")),
        name="dfd",
    )(page_table, dq, dk_new, dv_new, dbias, lw['df_lambda'], lw['df_norm'],
      *([cache_kt] * P), *([cache_v] * P))


def _mld_kernel(pt_ref, q_ref, cn_ref, rn_ref, mask_ref, wuk_ref, wuva_ref, wuvb_ref, cache_c, cache_rt,
                o_ref, ql_scr, qr_scr, m_scr, l_scr, acc_scr, cbuf, rbuf, sem, *, pages, page, layer):
    b = pl.program_id(0)
    pg = pl.program_id(1)
    nsteps = pl.num_programs(1)
    last = nsteps - 1
    step = b * nsteps + pg
    slot = step % 2

    def page_copies(sl, page_id):
        for j in range(pages):
            pid = page_id(j)
            yield pltpu.make_async_copy(cache_c.at[layer, pid], cbuf.at[sl, j], sem.at[0, sl])
            yield pltpu.make_async_copy(cache_rt.at[layer, pid], rbuf.at[sl, j], sem.at[1, sl])

    @pl.when(step == 0)
    def _():
        for cp in page_copies(slot, lambda j: pt_ref[b, pg * pages + j]):
            cp.start()

    @pl.when(step + 1 < pl.num_programs(0) * nsteps)
    def _():
        wrap = pg == last
        nb_, ng_ = jnp.where(wrap, b + 1, b), jnp.where(wrap, 0, pg + 1)
        for cp in page_copies(1 - slot, lambda j: pt_ref[nb_, ng_ * pages + j]):
            cp.start()

    for cp in page_copies(slot, lambda j: 0):
        cp.wait()
    c_refs = [cbuf.at[slot, j] for j in range(pages)]
    r_refs = [rbuf.at[slot, j] for j in range(pages)]

    @pl.when(pg == 0)
    def _():
        for h in range(MLA_HEADS):
            sl = slice(h * HEAD_PAD, (h + 1) * HEAD_PAD)
            qh = q_ref[:, sl].astype(BF16)
            rows = slice(h * SAMPLE_PAD, (h + 1) * SAMPLE_PAD)
            ql_scr[rows, :] = lax.dot_general(qh, wuk_ref[:, sl], _NT, preferred_element_type=F32)
            qr_scr[rows, :] = q_ref[:, h * HEAD_PAD + MLA_NOPE:h * HEAD_PAD + MLA_NOPE + MLA_ROPE]
        m_scr[...] = jnp.full(m_scr.shape, NEG_INF, F32)
        l_scr[...] = jnp.zeros(l_scr.shape, F32)
        acc_scr[...] = jnp.zeros(acc_scr.shape, F32)

    ql = ql_scr[...].astype(BF16)
    qr = qr_scr[...].astype(BF16)

    per = pages // MLD_CHAINS
    states = []
    for g in range(MLD_CHAINS):
        cb = [c_refs[j][...].astype(BF16) for j in range(g * per, (g + 1) * per)]
        s = jnp.concatenate(
            [lax.dot_general(ql, cb[j], _NT, preferred_element_type=F32)
             + jnp.dot(qr, r_refs[g * per + j][...].astype(BF16), preferred_element_type=F32)
             for j in range(per)], axis=1)
        c_all = jnp.concatenate(cb, axis=0)
        st = _softmax_next((m_scr[g], l_scr[g], acc_scr[g]), s,
                           lambda p, c_all=c_all: jnp.dot(p, c_all, preferred_element_type=F32))
        m_scr[g], l_scr[g], acc_scr[g] = st
        states.append(st)

    @pl.when(pg == last)
    def _():
        m_all = functools.reduce(jnp.maximum, [st[0] for st in states])
        l_all = sum(st[1] * jnp.exp2(st[0] - m_all) for st in states)
        acc_all = sum(st[2] * jnp.exp2(st[0] - m_all) for st in states)
        cn = _pad_rows(cn_ref[...], page).astype(BF16)
        rn = _pad_rows(rn_ref[...], page).astype(BF16)
        s_new = (lax.dot_general(ql, cn, _NT, preferred_element_type=F32)
                 + lax.dot_general(qr, rn, _NT, preferred_element_type=F32) + mask_ref[...])
        m, l, acc = _softmax_next((m_all, l_all, acc_all), s_new,
                                  lambda p: jnp.dot(p, cn, preferred_element_type=F32))
        o = (acc / l).astype(BF16)
        for p in range(MLA_HEADS // 2):
            c = slice(p * 2 * MLA_V, (p + 1) * 2 * MLA_V)
            ra = slice(2 * p * SAMPLE_PAD, (2 * p + 1) * SAMPLE_PAD)
            rb = slice((2 * p + 1) * SAMPLE_PAD, (2 * p + 2) * SAMPLE_PAD)
            y = (jnp.dot(o[ra], wuva_ref[:, c], preferred_element_type=F32)
                 + jnp.dot(o[rb], wuvb_ref[:, c], preferred_element_type=F32))
            o_ref[:, c] = y


def _mld(layer, qm, ckv_new, kr_new, new_mask, cache_c, cache_rt, page_table, lw):
    nb, n_pages = page_table.shape
    page = cache_c.shape[2]
    P = math.gcd(MLD_PAGES, n_pages)
    assert P % MLD_CHAINS == 0
    nrow = MLA_HEADS * SAMPLE_PAD
    qw = MLA_HEADS * HEAD_PAD
    ow = MLA_HEADS * MLA_V

    const = lambda shape: pl.BlockSpec(shape, lambda b, g, pt: (0,) * len(shape))
    hbm = pl.BlockSpec(memory_space=pl.ANY)
    grid_spec = pltpu.PrefetchScalarGridSpec(
        num_scalar_prefetch=1,
        grid=(nb, n_pages // P),
        in_specs=[pl.BlockSpec((SAMPLE_PAD, qw), lambda b, g, pt: (b, 0)),
                  pl.BlockSpec((SAMPLE_PAD, MLA_KV_LORA), lambda b, g, pt: (b, 0)),
                  pl.BlockSpec((SAMPLE_PAD, MLA_ROPE), lambda b, g, pt: (b, 0)),
                  const((nrow, page)), const((MLA_KV_LORA, qw)), const((MLA_KV_LORA, ow)),
                  const((MLA_KV_LORA, ow)), hbm, hbm],
        out_specs=pl.BlockSpec((SAMPLE_PAD, ow), lambda b, g, pt: (b, 0)),
        scratch_shapes=[pltpu.VMEM((nrow, MLA_KV_LORA), F32), pltpu.VMEM((nrow, MLA_ROPE), F32),
                        pltpu.VMEM((MLD_CHAINS, nrow, 1), F32), pltpu.VMEM((MLD_CHAINS, nrow, 1), F32),
                        pltpu.VMEM((MLD_CHAINS, nrow, MLA_KV_LORA), F32),
                        pltpu.VMEM((2, P, page, MLA_KV_LORA), F32), pltpu.VMEM((2, P, MLA_ROPE, page), F32),
                        pltpu.SemaphoreType.DMA((2, 2))],
    )
    return pl.pallas_call(
        functools.partial(_mld_kernel, pages=P, page=page, layer=layer),
        grid_spec=grid_spec,
        out_shape=jax.ShapeDtypeStruct((nb * SAMPLE_PAD, ow), F32),
        compiler_params=_params(("arbitrary", "arbitrary")),
        name="mld",
    )(page_table, qm, ckv_new, kr_new, new_mask, lw['wuk'], lw['wuva'], lw['wuvb'], cache_c, cache_rt)


def _post_kernel(x_ref, ym_ref, yd_ref, yc_ref, gate_ref, g_ref, wbm, wbd, wbc, wout, wfi, wfo, o_ref):
    u = None
    for j, (y_ref, w_ref) in enumerate(((ym_ref, wbm), (yd_ref, wbd), (yc_ref, wbc))):
        t = gate_ref[:, j * D_MODEL:(j + 1) * D_MODEL].astype(F32) * jnp.dot(
            y_ref[...].astype(BF16), w_ref[...], preferred_element_type=F32)
        u = t if u is None else u + t
    x = x_ref[...] + _rms(jnp.dot(u.astype(BF16), wout[...], preferred_element_type=F32), g_ref[0:1])
    hf = _rms(x, g_ref[1:2]).astype(BF16)
    cw = D_FF // 2
    t = None
    for j in range(2):
        a = jnp.dot(hf, wfi[:, j * cw:(j + 1) * cw], preferred_element_type=F32)
        bb = jnp.dot(hf, wfi[:, D_FF + j * cw:D_FF + (j + 1) * cw], preferred_element_type=F32)
        s = (a * jax.nn.sigmoid(a) * bb).astype(BF16)
        d = jnp.dot(s, wfo[j * cw:(j + 1) * cw, :], preferred_element_type=F32)
        t = d if t is None else t + d
    o_ref[...] = x + _rms(t, g_ref[2:3])


def _post(x, ym, yd, yc, gates, lw):
    rows = x.shape[0]
    tm = min(ROW_TILE, rows)
    row = lambda width: pl.BlockSpec((tm, width), lambda i: (i, 0))
    weights = [lw['g123'], lw['w_br_ml'], lw['w_br_df'], lw['w_br_mla'], lw['w_out'], lw['w_ffn_in'], lw['w_ffn_out']]
    return pl.pallas_call(
        _post_kernel,
        grid=(rows // tm,),
        in_specs=[row(D_MODEL), row(512), row(512), row(512), row(N_BRANCH * D_MODEL)]
                 + [_const_spec(w.shape) for w in weights],
        out_specs=row(D_MODEL),
        out_shape=jax.ShapeDtypeStruct((rows, D_MODEL), F32),
        compiler_params=_params(("parallel",)),
        name="post",
    )(x, ym, yd, yc, gates, *weights)


def _rot_half_cols(w):
    half = w.shape[-1] // 2
    return jnp.concatenate([-w[..., half:], w[..., :half]], axis=-1)


def _head_pad(nope, rope):
    k, h = nope.shape[0], nope.shape[1]
    pad = jnp.zeros((k, h, HEAD_PAD - MLA_NOPE - MLA_ROPE), nope.dtype)
    return jnp.concatenate([nope, rope, pad], axis=-1).reshape(k, h * HEAD_PAD)


def _layer_weights(l, norm_gains, w_in, b_ml_gates, ml_head_norm, df_lambda, df_head_norm, mla_q_norm,
                   mla_kv_norm, w_uq, w_uk, w_uv, w_br_ml, w_br_df, w_br_mla, w_out, w_ffn_in, w_ffn_out):
    offs = np.cumsum((0,) + IN_SIZES)
    col = lambda i: w_in[l][:, offs[i]:offs[i + 1]]
    mq, mk, mv, mo, mi, mf, dq, dk, dv, cq, ckv, kr, gl = (col(i) for i in range(len(IN_SIZES)))
    zeros = lambda n: jnp.zeros((D_MODEL, n), F32)
    rope_grp = lambda w: jnp.concatenate([zeros(MLA_NOPE), w, zeros(HEAD_PAD - MLA_NOPE - MLA_ROPE)], axis=1)
    wmisc = jnp.concatenate([mi, mf, zeros(LANES - 2 * ML_HEADS), rope_grp(kr), rope_grp(_rot_half_cols(kr))], axis=1)
    uq = w_uq[l].reshape(MLA_Q_LORA, MLA_HEADS, MLA_NOPE + MLA_ROPE)
    uq_n, uq_r = uq[..., :MLA_NOPE], uq[..., MLA_NOPE:]
    uv = w_uv[l]
    zv = jnp.zeros_like(uv[:, 0::2])
    bf = lambda a: a.astype(BF16)
    return {
        'g0': norm_gains[l, 0:1], 'g123': norm_gains[l, 1:4],
        'wml': bf(jnp.concatenate([mq, mk, mv, mo], axis=1)), 'wdf': bf(jnp.concatenate([dq, dk, dv], axis=1)),
        'wcq': bf(cq), 'wckv': bf(ckv), 'wmisc': bf(wmisc), 'wgl': bf(gl),
        'q_norm': mla_q_norm[l][None], 'kv_norm': mla_kv_norm[l][None],
        'wuqa': bf(_head_pad(uq_n, uq_r)), 'wuqb': bf(_head_pad(jnp.zeros_like(uq_n), _rot_half_cols(uq_r))),
        'wuk': bf(_head_pad(w_uk[l], jnp.zeros((MLA_KV_LORA, MLA_HEADS, MLA_ROPE), F32))),
        'wdvt': bf(dv.T), 'wuvt': bf(uv.reshape(MLA_KV_LORA, MLA_HEADS * MLA_V).T),
        'wuva': bf(jnp.concatenate([uv[:, 0::2], zv], axis=-1).reshape(MLA_KV_LORA, MLA_HEADS * MLA_V)),
        'wuvb': bf(jnp.concatenate([zv, uv[:, 1::2]], axis=-1).reshape(MLA_KV_LORA, MLA_HEADS * MLA_V)),
        'gate_bias': jnp.concatenate([b_ml_gates[l], jnp.zeros((LANES - 2 * ML_HEADS,), F32)])[None],
        'ml_norm': ml_head_norm[l][None], 'df_lambda': df_lambda[l], 'df_norm': df_head_norm[l][None],
        'w_br_ml': bf(w_br_ml[l]), 'w_br_df': bf(w_br_df[l]), 'w_br_mla': bf(w_br_mla[l]),
        'w_out': bf(w_out[l]), 'w_ffn_in': bf(w_ffn_in[l]), 'w_ffn_out': bf(w_ffn_out[l]),
    }


def _rope_tables(pos):
    half = MLA_ROPE // 2
    freqs = ROPE_THETA ** (-jnp.arange(half, dtype=F32) / half)
    ang = pos.astype(F32)[:, None] * freqs[None, :]
    cos, sin = jnp.cos(ang), jnp.sin(ang)
    n = pos.shape[0]
    tail = jnp.zeros((n, HEAD_PAD - MLA_NOPE - MLA_ROPE), F32)
    cs = jnp.concatenate([jnp.ones((n, MLA_NOPE), F32), cos, cos, tail], axis=1)
    sn = jnp.concatenate([jnp.zeros((n, MLA_NOPE), F32), sin, sin, tail], axis=1)
    return cs, sn


def kernel(x_prompt, x_sample, state_mlstm_C, state_mlstm_n, state_mlstm_m, cache_diff_k, cache_diff_v,
           cache_mla_ckv, cache_mla_krope, page_table, norm_gains, w_in, b_ml_gates, ml_head_norm, df_lambda,
           df_head_norm, rel_bias, mla_q_norm, mla_kv_norm, w_uq, w_uk, w_uv, w_br_ml, w_br_df, w_br_mla, w_out,
           w_ffn_in, w_ffn_out):
    depth = w_in.shape[0]
    B, S, _ = x_prompt.shape
    DB, DS, _ = x_sample.shape
    n_pages = page_table.shape[1]
    n_pool, page = cache_diff_k.shape[1], cache_diff_k.shape[2]
    past_len = n_pages * page
    assert DS <= SAMPLE_PAD
    T = ATT_TILE

    cs_p, sn_p = _rope_tables(jnp.arange(S, dtype=jnp.int32))
    pos_s = past_len + jnp.arange(SAMPLE_PAD, dtype=jnp.int32)
    cs_s, sn_s = _rope_tables(jnp.tile(pos_s, min(DB, ROW_TILE // SAMPLE_PAD)))
    ii, jj = np.meshgrid(np.arange(T), np.arange(T), indexing='ij')
    diag, sub = _t5_bucket_np(jj - ii), _t5_bucket_np(T + jj - ii)
    bias_p = _bias_table(rel_bias, np.concatenate([diag, sub], axis=0)).reshape(DF_HEADS, 2, T, T)
    rr, kk = np.meshgrid(np.arange(SAMPLE_PAD), np.arange(page), indexing='ij')
    new_ok = (kk <= rr) & (kk < DS)
    idx_d = np.concatenate([np.full((SAMPLE_PAD, page), REL_BUCKETS - 1, np.int32),
                            _t5_bucket_np(page + rr - kk),
                            np.where(new_ok, _t5_bucket_np(rr - kk), -1)], axis=0)
    bias_d = _bias_table(rel_bias, idx_d).reshape(DF_HEADS, 3, 1, SAMPLE_PAD, page)
    bias_d = jnp.broadcast_to(bias_d, (DF_HEADS, 3, 2, SAMPLE_PAD, page))
    bias_d = jnp.transpose(bias_d, (1, 0, 2, 3, 4)).reshape(3, DF_HEADS * 2 * SAMPLE_PAD, page)
    mla_new_mask = jnp.asarray(np.tile(np.where(new_ok, 0.0, NEG_INF).astype(np.float32), (MLA_HEADS, 1)))

    ckt = jnp.transpose(cache_diff_k, (0, 1, 3, 4, 5, 2)).reshape(depth, n_pool, DF_HEADS * 2 * DF_DK, page)
    crt = jnp.transpose(cache_mla_krope, (0, 1, 3, 2))
    cv = cache_diff_v.reshape(depth, n_pool, page * DF_HEADS, DF_DV)

    xp = x_prompt.reshape(B * S, D_MODEL)
    xs = jnp.pad(x_sample, ((0, 0), (0, SAMPLE_PAD - DS), (0, 0))).reshape(DB * SAMPLE_PAD, D_MODEL)
    zero_c = jnp.zeros((B, ML_HEADS, ML_DK, ML_DV), F32)
    zero_n = jnp.zeros((B, ML_HEADS, ML_DK), F32)
    zero_m = jnp.zeros((B, ML_HEADS), F32)
    st_p, st_s = [], []
    for l in range(depth):
        lw = _layer_weights(l, norm_gains, w_in, b_ml_gates, ml_head_norm, df_lambda, df_head_norm, mla_q_norm,
                            mla_kv_norm, w_uq, w_uk, w_uv, w_br_ml, w_br_df, w_br_mla, w_out, w_ffn_in, w_ffn_out)
        lam_init = 0.8 - 0.6 * math.exp(-0.3 * l)

        (ml, gif, dq, dk, dv, qm, ckv, kr, gates, dkb, dvt, km, vmt) = _proj(xp, lw, cs_p, sn_p, True)
        ym, c_p, n_p, m_p = _mlstm(ml, gif, lw, zero_c, zero_n, zero_m, B, S, math.gcd(S, ML_CHUNK), S)
        yd = _dfp(dq, dkb, dvt, bias_p, rel_bias, lw, B, S, lam_init)
        yc = _mlp(qm, km, vmt, B, S)
        xp = _post(xp, ym, yd, yc, gates, lw)
        st_p.append((dk.reshape(B, S, DF_HEADS, 2, DF_DK), dv.reshape(B, S, DF_HEADS, DF_DV),
                     ckv.reshape(B, S, MLA_KV_LORA), kr.reshape(B, S, MLA_ROPE), c_p, n_p, m_p))

        (ml, gif, dq, dk, dv, qm, ckv, kr, gates) = _proj(xs, lw, cs_s, sn_s, False)
        ym, c_s, n_s, m_s = _mlstm(ml, gif, lw, state_mlstm_C[l], state_mlstm_n[l], state_mlstm_m[l],
                                   DB, SAMPLE_PAD, SAMPLE_PAD, DS)
        yd = _dfd(l, dq, dk, dv, bias_d, ckt, cv, page_table, lw, lam_init)
        yc = _mld(l, qm, ckv, kr, mla_new_mask, cache_mla_ckv, crt, page_table, lw)
        xs = _post(xs, ym, yd, yc, gates, lw)
        tok = lambda a, *tail: a.reshape((DB, SAMPLE_PAD) + tail)[:, :DS]
        st_s.append((tok(dk, DF_HEADS, 2, DF_DK), tok(dv, DF_HEADS, DF_DV), tok(ckv, MLA_KV_LORA),
                     tok(kr, MLA_ROPE), c_s, n_s, m_s))

    outs_p = [jnp.stack(a) for a in zip(*st_p)]
    outs_s = [jnp.stack(a) for a in zip(*st_s)]
    yp = xp.reshape(B, S, D_MODEL)
    ys = xs.reshape(DB, SAMPLE_PAD, D_MODEL)[:, :DS]
    return (yp, ys, *outs_p, *outs_s)
```

```python
import functools
import math

import numpy as np
import jax
import jax.numpy as jnp
from jax import lax
from jax.experimental import pallas as pl
from jax.experimental.pallas import tpu as pltpu

F32 = jnp.float32
BF16 = jnp.bfloat16

D_MODEL = 1024
ML_HEADS = 4
ML_DK = 128
ML_DV = 128
DF_HEADS = 4
DF_DK = 64
DF_DV = 2 * DF_DK
MLA_HEADS = 8
MLA_NOPE = 64
MLA_ROPE = 32
MLA_V = 64
MLA_Q_LORA = 384
MLA_KV_LORA = 256
ROPE_THETA = 10000.0
REL_BUCKETS = 32
REL_MAX_DIST = 128
N_BRANCH = 3
D_FF = 2816
EPS = 1e-6
NEG_INF = -1e30
LOG2E = math.log2(math.e)
IN_SIZES = (ML_HEADS * ML_DK, ML_HEADS * ML_DK, ML_HEADS * ML_DV, ML_HEADS * ML_DV, ML_HEADS, ML_HEADS,
            DF_HEADS * 2 * DF_DK, DF_HEADS * 2 * DF_DK, DF_HEADS * DF_DV,
            MLA_Q_LORA, MLA_KV_LORA, MLA_ROPE,
            N_BRANCH * D_MODEL)

LANES = 128
SUBLANES = 8
HEAD_PAD = 128
ROW_TILE = 256
ATT_TILE = 512
DFP_HEADS_PER_STEP = 2
MLP_HEADS_PER_STEP = 4
ML_CHUNK = 256
SAMPLE_PAD = 8
DFD_PAGES = 16
MLD_PAGES = 32
MLD_CHAINS = 2
VMEM_LIMIT = 56 * 1024 * 1024

_NT = (((1,), (1,)), ((), ()))
_TN = (((0,), (0,)), ((), ()))


def _params(sem):
    return pltpu.CompilerParams(dimension_semantics=sem, vmem_limit_bytes=VMEM_LIMIT)


def _const_spec(shape):
    nd = len(shape)
    return pl.BlockSpec(shape, lambda *_: (0,) * nd, pipeline_mode=pl.Buffered(1))


def _rms(x, g):
    return x * lax.rsqrt(jnp.mean(x * x, axis=-1, keepdims=True) + EPS) * g


def _t5_bucket_np(n):
    n = np.asarray(n, np.int64)
    exact = REL_BUCKETS // 2
    nf = np.maximum(n, 1).astype(np.float32)
    large = exact + (np.log(nf / np.float32(exact)) / np.float32(math.log(REL_MAX_DIST / exact))
                     * np.float32(REL_BUCKETS - exact)).astype(np.int32)
    large = np.minimum(large, REL_BUCKETS - 1)
    b = np.where(n < exact, n, large)
    return np.where(n < 0, -1, b).astype(np.int32)


def _bias_table_kernel(rb_ref, idx_ref, out_ref):
    idx = idx_ref[...]
    for h in range(DF_HEADS):
        acc = jnp.full(idx.shape, NEG_INF, F32)
        for b in range(REL_BUCKETS):
            acc = jnp.where(idx == b, rb_ref[b, h] * LOG2E, acc)
        out_ref[h] = acc


def _bias_table(rel_bias, idx_np):
    r, c = idx_np.shape
    return pl.pallas_call(
        _bias_table_kernel,
        out_shape=jax.ShapeDtypeStruct((DF_HEADS, r, c), F32),
        in_specs=[pl.BlockSpec(memory_space=pltpu.SMEM), pl.BlockSpec(memory_space=pltpu.VMEM)],
        out_specs=pl.BlockSpec(memory_space=pltpu.VMEM),
        name="bias_table",
    )(rel_bias, jnp.asarray(idx_np))


def _proj_kernel(x_ref, g_ref, cs_ref, sn_ref, wml, wdf, wcq, wckv, wmisc, wgl, qn_ref, kvn_ref,
                 wuqa, wuqb, wuk, wdvt, wuvt,
                 ml_o, gif_o, dq_o, dk_o, dv_o, qm_o, ckv_o, kr_o, gate_o, *prompt_outs, prompt):
    x = x_ref[...]
    h = _rms(x, g_ref[...]).astype(BF16)

    def mm(w_ref, lo, hi):
        return jnp.dot(h, w_ref[:, lo:hi], preferred_element_type=F32)

    w = ML_HEADS * ML_DK
    act = ml_o.dtype
    ml_o[:, 0:w] = mm(wml, 0, w).astype(act)
    ml_o[:, w:2 * w] = (mm(wml, w, 2 * w) * ML_DK ** -0.5).astype(act)
    ml_o[:, 2 * w:3 * w] = mm(wml, 2 * w, 3 * w).astype(act)
    ml_o[:, 3 * w:4 * w] = jax.nn.sigmoid(mm(wml, 3 * w, 4 * w)).astype(act)

    misc = mm(wmisc, 0, 3 * LANES)
    gif_o[...] = misc[:, 0:LANES]
    cs = cs_ref[...]
    sn = sn_ref[...]
    krp = misc[:, LANES:2 * LANES] * cs + misc[:, 2 * LANES:3 * LANES] * sn
    kr_o[...] = krp[:, MLA_NOPE:MLA_NOPE + MLA_ROPE]

    wd = DF_HEADS * 2 * DF_DK
    dq_o[...] = (mm(wdf, 0, wd) * (DF_DK ** -0.5 * LOG2E)).astype(act)
    dk = mm(wdf, wd, 2 * wd)
    dv = mm(wdf, 2 * wd, 3 * wd)
    dk_o[...] = dk
    dv_o[...] = dv

    c_q = _rms(mm(wcq, 0, MLA_Q_LORA), qn_ref[...]).astype(BF16)
    c_kv = _rms(mm(wckv, 0, MLA_KV_LORA), kvn_ref[...])
    ckv_o[...] = c_kv
    scale = (MLA_NOPE + MLA_ROPE) ** -0.5 * LOG2E
    for hh in range(MLA_HEADS):
        sl = slice(hh * HEAD_PAD, (hh + 1) * HEAD_PAD)
        qa = jnp.dot(c_q, wuqa[:, sl], preferred_element_type=F32)
        qb = jnp.dot(c_q, wuqb[:, sl], preferred_element_type=F32)
        qm_o[:, sl] = ((qa * cs + qb * sn) * scale).astype(act)

    gw = 512
    for j in range(N_BRANCH * D_MODEL // gw):
        gate_o[:, j * gw:(j + 1) * gw] = jax.nn.sigmoid(mm(wgl, j * gw, (j + 1) * gw)).astype(BF16)

    if prompt:
        dkb_o, dvt_o, km_o, vmt_o = prompt_outs
        dkb_o[...] = dk.astype(BF16)
        dvt_o[...] = lax.dot_general(wdvt[...], h, _NT, preferred_element_type=F32).astype(BF16)
        ckb = c_kv.astype(BF16)
        for hh in range(MLA_HEADS):
            sl = slice(hh * HEAD_PAD, (hh + 1) * HEAD_PAD)
            kn = jnp.dot(ckb, wuk[:, sl], preferred_element_type=F32)
            km_o[:, sl] = (kn + krp).astype(BF16)
        vmt_o[...] = lax.dot_general(wuvt[...], ckb, _NT, preferred_element_type=F32).astype(BF16)


def _proj(x, lw, cs, sn, prompt):
    rows = x.shape[0]
    tm = min(ROW_TILE, rows)
    assert rows % tm == 0
    row = lambda width: pl.BlockSpec((tm, width), lambda i: (i, 0))
    assert cs.shape[0] % tm == 0 and rows % cs.shape[0] == 0
    pos_tiles = cs.shape[0] // tm
    pos_row = pl.BlockSpec((tm, LANES), lambda i: (i % pos_tiles, 0))
    weights = [lw['wml'], lw['wdf'], lw['wcq'], lw['wckv'], lw['wmisc'], lw['wgl'], lw['q_norm'], lw['kv_norm'],
               lw['wuqa'], lw['wuqb'], lw['wuk'], lw['wdvt'], lw['wuvt']]
    act = BF16 if prompt else F32
    out_widths = [(4 * ML_HEADS * ML_DK, act), (LANES, F32), (512, act), (512, F32), (512, F32),
                  (MLA_HEADS * HEAD_PAD, act), (MLA_KV_LORA, F32), (MLA_ROPE, F32), (N_BRANCH * D_MODEL, BF16)]
    out_specs = [row(wd) for wd, _ in out_widths]
    out_shape = [jax.ShapeDtypeStruct((rows, wd), dt) for wd, dt in out_widths]
    if prompt:
        col = lambda height: pl.BlockSpec((height, tm), lambda i: (0, i))
        out_specs += [row(512), col(512), row(MLA_HEADS * HEAD_PAD), col(512)]
        out_shape += [jax.ShapeDtypeStruct((rows, 512), BF16), jax.ShapeDtypeStruct((512, rows), BF16),
                      jax.ShapeDtypeStruct((rows, MLA_HEADS * HEAD_PAD), BF16),
                      jax.ShapeDtypeStruct((512, rows), BF16)]
    return pl.pallas_call(
        functools.partial(_proj_kernel, prompt=prompt),
        grid=(rows // tm,),
        in_specs=[row(D_MODEL), _const_spec((1, D_MODEL)), pos_row, pos_row]
                 + [_const_spec(w.shape) for w in weights],
        out_specs=out_specs,
        out_shape=out_shape,
        compiler_params=_params(("parallel",)),
        name="proj_prompt" if prompt else "proj_sample",
    )(x, lw['g0'], cs, sn, *weights)


def _mlstm_kernel(m0_ref, ml_ref, gif_ref, gb_ref, hn_ref, c0_ref, n0_ref,
                  ym_o, c_o, n_o, m_o, cn_scr, m_scr, *, chunk, t_valid):
    b = pl.program_id(0)
    c = pl.program_id(1)
    L = chunk
    dk, dv = ML_DK, ML_DV
    lane = lax.broadcasted_iota(jnp.int32, (1, LANES), 1)

    @pl.when(c == 0)
    def _():
        for h in range(ML_HEADS):
            ncol = jnp.where(lax.broadcasted_iota(jnp.int32, (dk, LANES), 1) == 0, n0_ref[0, h], 0.0)
            cn_scr[h] = jnp.concatenate([c0_ref[0, h], ncol], axis=1)
            m_scr[h] = jnp.full((SUBLANES, LANES), m0_ref[b, h], F32)

    g = gif_ref[...] + gb_ref[...]
    g = jnp.where(lane < ML_HEADS, g, jnp.minimum(g, 0.0) - jnp.log(1.0 + jnp.exp(-jnp.abs(g))))
    row_i = lax.broadcasted_iota(jnp.int32, (L, L), 0)
    col_i = lax.broadcasted_iota(jnp.int32, (L, L), 1)
    if t_valid < L:
        tok = lax.broadcasted_iota(jnp.int32, (L, LANES), 0)
        g = jnp.where(tok < t_valid, g, jnp.where(lane < ML_HEADS, NEG_INF, 0.0))
    causal = col_i <= row_i
    eye = col_i == row_i

    for h in range(ML_HEADS):
        lic = g[:, h:h + 1]
        lfc = g[:, ML_HEADS + h:ML_HEADS + h + 1]
        br = jnp.sum(jnp.where(row_i <= col_i, lfc, 0.0), axis=0, keepdims=True)
        bc = jnp.sum(jnp.where(eye, br, 0.0), axis=1, keepdims=True)
        ir = jnp.sum(jnp.where(eye, lic, 0.0), axis=0, keepdims=True)
        b_last = jnp.sum(lfc, axis=0, keepdims=True)
        m_prev = m_scr[h][0:1, 0:1]

        q = ml_ref[:, h * dk:(h + 1) * dk].astype(BF16)
        k = ml_ref[:, (ML_HEADS + h) * dk:(ML_HEADS + h + 1) * dk].astype(BF16)
        v = ml_ref[:, (2 * ML_HEADS + h) * dk:(2 * ML_HEADS + h + 1) * dk].astype(BF16)
        og = ml_ref[:, (3 * ML_HEADS + h) * dk:(3 * ML_HEADS + h + 1) * dk]

        dmat = jnp.where(causal, bc - br + ir, NEG_INF)
        inter = bc + m_prev
        mt = jnp.maximum(inter, jnp.max(dmat, axis=1, keepdims=True))
        wgt = jnp.exp(dmat - mt)
        iw = jnp.exp(inter - mt)
        a = lax.dot_general(q, k, _NT, preferred_element_type=F32) * wgt
        cn = cn_scr[h]
        qc = jnp.dot(q, cn.astype(BF16), preferred_element_type=F32)
        num = jnp.dot(a.astype(BF16), v, preferred_element_type=F32) + iw * qc[:, 0:dv]
        den = jnp.sum(a, axis=1, keepdims=True) + iw * qc[:, dv:dv + 1]
        hh = num / jnp.maximum(jnp.abs(den), jnp.exp(-mt))
        y = _rms(hh, hn_ref[:, h * dv:(h + 1) * dv]) * og.astype(F32)
        ym_o[:, h * dv:(h + 1) * dv] = y.astype(ym_o.dtype)

        m_new = jnp.maximum(b_last + m_prev, jnp.max(b_last - br + ir, axis=1, keepdims=True))
        wl = jnp.exp(b_last - bc + lic - m_new)
        dec = jnp.exp(b_last + m_prev - m_new)
        ones_col = jnp.where(lax.broadcasted_iota(jnp.int32, (L, LANES), 1) == 0, wl, 0.0)
        wv = jnp.concatenate([wl * v.astype(F32), ones_col], axis=1).astype(BF16)
        cn_new = dec * cn + lax.dot_general(k, wv, _TN, preferred_element_type=F32)
        cn_scr[h] = cn_new
        m_scr[h] = jnp.broadcast_to(m_new, (SUBLANES, LANES))
        c_o[0, h] = cn_new[:, 0:dv]
        n_o[0, h] = cn_new[:, dv:dv + 1]
        m_o[0, h] = jnp.broadcast_to(m_new, (SUBLANES, LANES))


def _mlstm(ml, gif, lw, c0, n0, m0, batch, seq, chunk, t_valid):
    nc = seq // chunk
    assert seq % chunk == 0
    width = ML_HEADS * ML_DV
    grid_spec = pltpu.PrefetchScalarGridSpec(
        num_scalar_prefetch=1,
        grid=(batch, nc),
        in_specs=[
            pl.BlockSpec((chunk, 4 * width), lambda b, c, m: (b * nc + c, 0)),
            pl.BlockSpec((chunk, LANES), lambda b, c, m: (b * nc + c, 0)),
            pl.BlockSpec((1, LANES), lambda b, c, m: (0, 0)),
            pl.BlockSpec((1, width), lambda b, c, m: (0, 0)),
            pl.BlockSpec((1, ML_HEADS, ML_DK, ML_DV), lambda b, c, m: (b, 0, 0, 0)),
            pl.BlockSpec((1, ML_HEADS, ML_DK, 1), lambda b, c, m: (b, 0, 0, 0)),
        ],
        out_specs=[
            pl.BlockSpec((chunk, width), lambda b, c, m: (b * nc + c, 0)),
            pl.BlockSpec((1, ML_HEADS, ML_DK, ML_DV), lambda b, c, m: (b, 0, 0, 0)),
            pl.BlockSpec((1, ML_HEADS, ML_DK, 1), lambda b, c, m: (b, 0, 0, 0)),
            pl.BlockSpec((1, ML_HEADS, SUBLANES, LANES), lambda b, c, m: (b, 0, 0, 0)),
        ],
        scratch_shapes=[pltpu.VMEM((ML_HEADS, ML_DK, 2 * ML_DV), F32),
                        pltpu.VMEM((ML_HEADS, SUBLANES, LANES), F32)],
    )
    ym, c_new, n_new, m_new = pl.pallas_call(
        functools.partial(_mlstm_kernel, chunk=chunk, t_valid=t_valid),
        grid_spec=grid_spec,
        out_shape=[jax.ShapeDtypeStruct((batch * seq, width), ml.dtype),
                   jax.ShapeDtypeStruct((batch, ML_HEADS, ML_DK, ML_DV), F32),
                   jax.ShapeDtypeStruct((batch, ML_HEADS, ML_DK, 1), F32),
                   jax.ShapeDtypeStruct((batch, ML_HEADS, SUBLANES, LANES), F32)],
        compiler_params=_params(("parallel", "arbitrary")),
        name="mlstm",
    )(m0, ml, gif, lw['gate_bias'], lw['ml_norm'], c0, n0[..., None])
    return ym, c_new, n_new[..., 0], m_new[:, :, 0, 0]


def _softmax_first(s, pv, axis=1):
    m = jnp.max(s, axis=axis, keepdims=True)
    p = jnp.exp2(s - m)
    return m, jnp.sum(p, axis=axis, keepdims=True), pv(p.astype(BF16))


def _softmax_next(carry, s, pv, axis=1, shift=None):
    m, l, acc = carry
    top = jnp.max(s, axis=axis, keepdims=True)
    m_new = jnp.maximum(m, top if shift is None else top + shift)
    alpha = jnp.exp2(m - m_new)
    p = jnp.exp2(s - (m_new if shift is None else m_new - shift))
    return m_new, alpha * l + jnp.sum(p, axis=axis, keepdims=True), alpha * acc + pv(p.astype(BF16))


def _lambda(lv_ref, lam_init):
    lv = lv_ref[...]
    e1 = jnp.exp(jnp.sum(lv[0:1] * lv[1:2], axis=1, keepdims=True))
    e2 = jnp.exp(jnp.sum(lv[2:3] * lv[3:4], axis=1, keepdims=True))
    return e1 - e2 + lam_init


def _dfp_kernel(rb_ref, q_ref, k_ref, vt_ref, bias_ref, lv_ref, hn_ref, o_ref, *, tile, heads, lam_init):
    hg = pl.program_id(1)
    qi = pl.program_id(2)
    T = tile
    hw = 2 * DF_DK
    lane = lax.broadcasted_iota(jnp.int32, (T, hw), 1)
    qs = []
    for j in range(heads):
        q = q_ref[:, j * hw:(j + 1) * hw]
        zero = jnp.zeros_like(q)
        qs += [jnp.where(lane < DF_DK, q, zero), jnp.where(lane >= DF_DK, q, zero)]

    def tile_scores(start):
        s, pvs = [], []
        for c in range(2 * heads):
            j = c // 2
            kt = k_ref[pl.ds(start, T), j * hw:(j + 1) * hw]
            s.append(lax.dot_general(kt, qs[c], _NT, preferred_element_type=F32))
            pvs.append(lambda p, j=j: jnp.dot(vt_ref[j * DF_DV:(j + 1) * DF_DV, pl.ds(start, T)], p,
                                              preferred_element_type=F32))
        return s, pvs

    s, pvs = tile_scores(pl.multiple_of(qi * T, T))
    carry = tuple(_softmax_first(sm + bias_ref[c // 2, 0], pv, axis=0) for c, (sm, pv) in enumerate(zip(s, pvs)))
    sub = jnp.maximum(qi - 1, 0)
    s, pvs = tile_scores(pl.multiple_of(sub * T, T))
    gone = jnp.where(qi == 0, NEG_INF, 0.0)
    carry = tuple(_softmax_next(st, sm + (bias_ref[c // 2, 1] + gone), pv, axis=0)
                  for c, (st, sm, pv) in enumerate(zip(carry, s, pvs)))
    far_bias = [rb_ref[REL_BUCKETS - 1, hg * heads + j] * LOG2E for j in range(heads)]

    def far(ki, carry):
        s, pvs = tile_scores(pl.multiple_of(ki * T, T))
        return tuple(_softmax_next(st, sm, pv, axis=0, shift=far_bias[c // 2])
                     for c, (st, sm, pv) in enumerate(zip(carry, s, pvs)))

    carry = lax.fori_loop(0, sub, far, carry)
    lam = _lambda(lv_ref, lam_init)
    for j in range(heads):
        (_, l0, acc0), (_, l1, acc1) = carry[2 * j], carry[2 * j + 1]
        od = (acc0 / l0 - lam * (acc1 / l1)).T
        sl = slice(j * DF_DV, (j + 1) * DF_DV)
        o_ref[:, sl] = (_rms(od, hn_ref[:, sl]) * (1.0 - lam_init)).astype(BF16)


def _dfp(dq, dkb, dvt, bias_tiles, rel_bias, lw, batch, seq, lam_init):
    T = ATT_TILE
    nq = seq // T
    assert seq % T == 0 and T >= REL_MAX_DIST
    hs = DFP_HEADS_PER_STEP
    hw = hs * 2 * DF_DK
    grid_spec = pltpu.PrefetchScalarGridSpec(
        num_scalar_prefetch=0,
        grid=(batch, DF_HEADS // hs, nq),
        in_specs=[
            pl.BlockSpec(memory_space=pltpu.SMEM),
            pl.BlockSpec((T, hw), lambda b, h, i: (b * nq + i, h)),
            pl.BlockSpec((seq, hw), lambda b, h, i: (b, h)),
            pl.BlockSpec((hs * DF_DV, seq), lambda b, h, i: (h, b)),
            pl.BlockSpec((hs, 2, T, T), lambda b, h, i: (h, 0, 0, 0)),
            pl.BlockSpec((4, DF_DK), lambda b, h, i: (0, 0)),
            pl.BlockSpec((1, hw), lambda b, h, i: (0, h)),
        ],
        out_specs=pl.BlockSpec((T, hw), lambda b, h, i: (b * nq + i, h)),
    )
    return pl.pallas_call(
        functools.partial(_dfp_kernel, tile=T, heads=hs, lam_init=lam_init),
        grid_spec=grid_spec,
        out_shape=jax.ShapeDtypeStruct((batch * seq, DF_HEADS * 2 * DF_DK), BF16),
        compiler_params=_params(("parallel", "parallel", "arbitrary")),
        name="dfp",
    )(rel_bias, dq, dkb, dvt, bias_tiles, lw['df_lambda'], lw['df_norm'])


def _mlp_kernel(q_ref, k_ref, vt_ref, o_ref, *, tile, heads):
    qi = pl.program_id(2)
    T = tile
    qs = [q_ref[:, j * HEAD_PAD:(j + 1) * HEAD_PAD] for j in range(heads)]
    key_i = lax.broadcasted_iota(jnp.int32, (T, T), 0)
    qry_i = lax.broadcasted_iota(jnp.int32, (T, T), 1)
    mask = jnp.where(key_i <= qry_i, 0.0, NEG_INF)
    vw = 2 * MLA_V

    def tile_scores(start):
        kt = k_ref[pl.ds(start, T), :]
        s = [lax.dot_general(kt[:, j * HEAD_PAD:(j + 1) * HEAD_PAD], qs[j], _NT, preferred_element_type=F32)
             for j in range(heads)]
        pvs = [lambda p, j=j: jnp.dot(vt_ref[(j // 2) * vw:(j // 2 + 1) * vw, pl.ds(start, T)], p,
                                      preferred_element_type=F32) for j in range(heads)]
        return s, pvs

    s, pvs = tile_scores(pl.multiple_of(qi * T, T))
    carry = tuple(_softmax_first(sm + mask, pv, axis=0) for sm, pv in zip(s, pvs))

    def far(ki, carry):
        s, pvs = tile_scores(pl.multiple_of(ki * T, T))
        return tuple(_softmax_next(c, sm, pv, axis=0) for c, sm, pv in zip(carry, s, pvs))

    carry = lax.fori_loop(0, qi, far, carry)
    row = lax.broadcasted_iota(jnp.int32, (vw, T), 0)
    for p in range(heads // 2):
        (_, la, acca), (_, lb, accb) = carry[2 * p], carry[2 * p + 1]
        o_ref[:, p * vw:(p + 1) * vw] = jnp.where(row < MLA_V, acca / la, accb / lb).T.astype(BF16)


def _mlp(qm, km, vmt, batch, seq):
    T = ATT_TILE
    nq = seq // T
    hs = MLP_HEADS_PER_STEP
    pw = hs * HEAD_PAD
    vw = hs * MLA_V
    return pl.pallas_call(
        functools.partial(_mlp_kernel, tile=T, heads=hs),
        grid=(batch, MLA_HEADS // hs, nq),
        in_specs=[
            pl.BlockSpec((T, pw), lambda b, p, i: (b * nq + i, p)),
            pl.BlockSpec((seq, pw), lambda b, p, i: (b, p)),
            pl.BlockSpec((vw, seq), lambda b, p, i: (p, b)),
        ],
        out_specs=pl.BlockSpec((T, vw), lambda b, p, i: (b * nq + i, p)),
        out_shape=jax.ShapeDtypeStruct((batch * seq, MLA_HEADS * MLA_V), BF16),
        compiler_params=_params(("parallel", "parallel", "arbitrary")),
        name="mlp",
    )(qm, km, vmt)


def _pad_rows(x, rows):
    return jnp.concatenate([x, jnp.zeros((rows - x.shape[0], x.shape[1]), x.dtype)], axis=0)


def _fetch_pages(pt_ref, layer, pages, caches, bufs, sem):
    b = pl.program_id(0)
    pg = pl.program_id(1)
    nsteps = pl.num_programs(1)
    step = b * nsteps + pg
    slot = step % 2

    def copies(sl, page_id):
        for j in range(pages):
            pid = page_id(j)
            for a, (cache, buf) in enumerate(zip(caches, bufs)):
                yield pltpu.make_async_copy(cache.at[layer, pid], buf.at[sl, j], sem.at[a, sl])

    @pl.when(step == 0)
    def _():
        for cp in copies(slot, lambda j: pt_ref[b, pg * pages + j]):
            cp.start()

    @pl.when(step + 1 < pl.num_programs(0) * nsteps)
    def _():
        wrap = pg == nsteps - 1
        nb_, ng_ = jnp.where(wrap, b + 1, b), jnp.where(wrap, 0, pg + 1)
        for cp in copies(1 - slot, lambda j: pt_ref[nb_, ng_ * pages + j]):
            cp.start()

    for cp in copies(slot, lambda j: 0):
        cp.wait()
    return slot


def _dfd_kernel(pt_ref, q_ref, kn_ref, vn_ref, bias_ref, lv_ref, hn_ref, cache_kt, cache_v,
                o_ref, q_scr, m_scr, l_scr, acc_scr, kbuf, vbuf, sem, *, pages, page, layer, lam_init):
    slot = _fetch_pages(pt_ref, layer, pages, (cache_kt, cache_v), (kbuf, vbuf), sem)
    k_refs = [kbuf.at[slot, j] for j in range(pages)]
    v_refs = [vbuf.at[slot, j] for j in range(pages)]
    pg = pl.program_id(1)
    last = pl.num_programs(1) - 1
    nrow = DF_HEADS * 2 * SAMPLE_PAD
    width = DF_HEADS * 2 * DF_DK

    @pl.when(pg == 0)
    def _():
        q = q_ref[...].astype(F32)
        qt = jnp.concatenate([q] * (DF_HEADS * 2), axis=0)
        rblk = lax.broadcasted_iota(jnp.int32, (nrow, width), 0) // SAMPLE_PAD
        cblk = lax.broadcasted_iota(jnp.int32, (nrow, width), 1) // DF_DK
        q_scr[...] = jnp.where(rblk == cblk, qt, 0.0).astype(BF16)
        m_scr[...] = jnp.full(m_scr.shape, NEG_INF, F32)
        l_scr[...] = jnp.zeros(l_scr.shape, F32)
        acc_scr[...] = jnp.zeros(acc_scr.shape, F32)

    qbd = q_scr[...]
    hrows = 2 * SAMPLE_PAD

    def pv_heads(p, v_of_head):
        return jnp.concatenate(
            [jnp.dot(p[h * hrows:(h + 1) * hrows], v_of_head(h), preferred_element_type=F32)
             for h in range(DF_HEADS)], axis=0)

    far = bias_ref[0]
    near = jnp.where(pg == last, bias_ref[1], far)
    s = jnp.concatenate(
        [jnp.dot(qbd, k_refs[j][...].astype(BF16), preferred_element_type=F32)
         + (near if j == pages - 1 else far) for j in range(pages)], axis=1)
    v_all = [jnp.concatenate([v_refs[j][pl.ds(h, page, stride=DF_HEADS), :].astype(BF16) for j in range(pages)],
                             axis=0) for h in range(DF_HEADS)]
    carry = _softmax_next((m_scr[...], l_scr[...], acc_scr[...]), s, lambda p: pv_heads(p, lambda h: v_all[h]))
    m_scr[...], l_scr[...], acc_scr[...] = carry

    @pl.when(pg == last)
    def _():
        kn = _pad_rows(kn_ref[...], page).astype(BF16)
        vn = _pad_rows(vn_ref[...], page).astype(BF16)
        s_new = lax.dot_general(qbd, kn, _NT, preferred_element_type=F32) + bias_ref[2]
        m, l, acc = _softmax_next(
            carry, s_new, lambda p: pv_heads(p, lambda h: vn[:, h * DF_DV:(h + 1) * DF_DV]))
        o = acc / l
        lam = _lambda(lv_ref, lam_init)
        for h in range(DF_HEADS):
            r0 = h * hrows
            c = slice(h * DF_DV, (h + 1) * DF_DV)
            od = o[r0:r0 + SAMPLE_PAD] - lam * o[r0 + SAMPLE_PAD:r0 + hrows]
            o_ref[:, c] = _rms(od, hn_ref[:, c]) * (1.0 - lam_init)


def _dfd(layer, dq, dk_new, dv_new, dbias, cache_kt, cache_v, page_table, lw, lam_init):
    nb, n_pages = page_table.shape
    page = cache_kt.shape[3]
    P = math.gcd(DFD_PAGES, n_pages)
    width = DF_HEADS * 2 * DF_DK
    nrow = DF_HEADS * 2 * SAMPLE_PAD
    assert page == LANES

    hbm = pl.BlockSpec(memory_space=pl.ANY)
    row = pl.BlockSpec((SAMPLE_PAD, width), lambda b, g, pt: (b, 0))
    grid_spec = pltpu.PrefetchScalarGridSpec(
        num_scalar_prefetch=1,
        grid=(nb, n_pages // P),
        in_specs=[pl.BlockSpec((SAMPLE_PAD, width), lambda b, g, pt: (b, 0)),
                  row, row,
                  pl.BlockSpec((3, nrow, page), lambda b, g, pt: (0, 0, 0)),
                  pl.BlockSpec((4, DF_DK), lambda b, g, pt: (0, 0)),
                  pl.BlockSpec((1, width), lambda b, g, pt: (0, 0)), hbm, hbm],
        out_specs=pl.BlockSpec((SAMPLE_PAD, width), lambda b, g, pt: (b, 0)),
        scratch_shapes=[pltpu.VMEM((nrow, width), BF16), pltpu.VMEM((nrow, 1), F32),
                        pltpu.VMEM((nrow, 1), F32), pltpu.VMEM((nrow, DF_DV), F32),
                        pltpu.VMEM((2, P, width, page), F32), pltpu.VMEM((2, P, page * DF_HEADS, DF_DV), F32),
                        pltpu.SemaphoreType.DMA((2, 2))],
    )
    return pl.pallas_call(
        functools.partial(_dfd_kernel, pages=P, page=page, layer=layer, lam_init=lam_init),
        grid_spec=grid_spec,
        out_shape=jax.ShapeDtypeStruct((nb * SAMPLE_PAD, width), F32),
        compiler_params=_params(("arbitrary", "arbitrary")),
        name="dfd",
    )(page_table, dq, dk_new, dv_new, dbias, lw['df_lambda'], lw['df_norm'], cache_kt, cache_v)


def _mld_kernel(pt_ref, q_ref, cn_ref, rn_ref, mask_ref, wuk_ref, wuva_ref, wuvb_ref, cache_c, cache_rt,
                o_ref, ql_scr, qr_scr, m_scr, l_scr, acc_scr, cbuf, rbuf, sem, *, pages, page, layer):
    pg = pl.program_id(1)
    last = pl.num_programs(1) - 1
    slot = _fetch_pages(pt_ref, layer, pages, (cache_c, cache_rt), (cbuf, rbuf), sem)
    c_refs = [cbuf.at[slot, j] for j in range(pages)]
    r_refs = [rbuf.at[slot, j] for j in range(pages)]

    @pl.when(pg == 0)
    def _():
        for h in range(MLA_HEADS):
            sl = slice(h * HEAD_PAD, (h + 1) * HEAD_PAD)
            qh = q_ref[:, sl].astype(BF16)
            rows = slice(h * SAMPLE_PAD, (h + 1) * SAMPLE_PAD)
            ql_scr[rows, :] = lax.dot_general(qh, wuk_ref[:, sl], _NT, preferred_element_type=F32)
            qr_scr[rows, :] = q_ref[:, h * HEAD_PAD + MLA_NOPE:h * HEAD_PAD + MLA_NOPE + MLA_ROPE]
        m_scr[...] = jnp.full(m_scr.shape, NEG_INF, F32)
        l_scr[...] = jnp.zeros(l_scr.shape, F32)
        acc_scr[...] = jnp.zeros(acc_scr.shape, F32)

    ql = ql_scr[...].astype(BF16)
    qr = qr_scr[...].astype(BF16)

    per = pages // MLD_CHAINS
    states = []
    for g in range(MLD_CHAINS):
        cb = [c_refs[j][...].astype(BF16) for j in range(g * per, (g + 1) * per)]
        s = jnp.concatenate(
            [lax.dot_general(ql, cb[j], _NT, preferred_element_type=F32)
             + jnp.dot(qr, r_refs[g * per + j][...].astype(BF16), preferred_element_type=F32)
             for j in range(per)], axis=1)
        c_all = jnp.concatenate(cb, axis=0)
        st = _softmax_next((m_scr[g], l_scr[g], acc_scr[g]), s,
                           lambda p, c_all=c_all: jnp.dot(p, c_all, preferred_element_type=F32))
        m_scr[g], l_scr[g], acc_scr[g] = st
        states.append(st)

    @pl.when(pg == last)
    def _():
        m_all = functools.reduce(jnp.maximum, [st[0] for st in states])
        l_all = sum(st[1] * jnp.exp2(st[0] - m_all) for st in states)
        acc_all = sum(st[2] * jnp.exp2(st[0] - m_all) for st in states)
        cn = _pad_rows(cn_ref[...], page).astype(BF16)
        rn = _pad_rows(rn_ref[...], page).astype(BF16)
        s_new = (lax.dot_general(ql, cn, _NT, preferred_element_type=F32)
                 + lax.dot_general(qr, rn, _NT, preferred_element_type=F32) + mask_ref[...])
        m, l, acc = _softmax_next((m_all, l_all, acc_all), s_new,
                                  lambda p: jnp.dot(p, cn, preferred_element_type=F32))
        o = (acc / l).astype(BF16)
        for p in range(MLA_HEADS // 2):
            c = slice(p * 2 * MLA_V, (p + 1) * 2 * MLA_V)
            ra = slice(2 * p * SAMPLE_PAD, (2 * p + 1) * SAMPLE_PAD)
            rb = slice((2 * p + 1) * SAMPLE_PAD, (2 * p + 2) * SAMPLE_PAD)
            y = (jnp.dot(o[ra], wuva_ref[:, c], preferred_element_type=F32)
                 + jnp.dot(o[rb], wuvb_ref[:, c], preferred_element_type=F32))
            o_ref[:, c] = y


def _mld(layer, qm, ckv_new, kr_new, new_mask, cache_c, cache_rt, page_table, lw):
    nb, n_pages = page_table.shape
    page = cache_c.shape[2]
    P = math.gcd(MLD_PAGES, n_pages)
    assert P % MLD_CHAINS == 0
    nrow = MLA_HEADS * SAMPLE_PAD
    qw = MLA_HEADS * HEAD_PAD
    ow = MLA_HEADS * MLA_V

    const = lambda shape: pl.BlockSpec(shape, lambda b, g, pt: (0,) * len(shape))
    hbm = pl.BlockSpec(memory_space=pl.ANY)
    grid_spec = pltpu.PrefetchScalarGridSpec(
        num_scalar_prefetch=1,
        grid=(nb, n_pages // P),
        in_specs=[pl.BlockSpec((SAMPLE_PAD, qw), lambda b, g, pt: (b, 0)),
                  pl.BlockSpec((SAMPLE_PAD, MLA_KV_LORA), lambda b, g, pt: (b, 0)),
                  pl.BlockSpec((SAMPLE_PAD, MLA_ROPE), lambda b, g, pt: (b, 0)),
                  const((nrow, page)), const((MLA_KV_LORA, qw)), const((MLA_KV_LORA, ow)),
                  const((MLA_KV_LORA, ow)), hbm, hbm],
        out_specs=pl.BlockSpec((SAMPLE_PAD, ow), lambda b, g, pt: (b, 0)),
        scratch_shapes=[pltpu.VMEM((nrow, MLA_KV_LORA), F32), pltpu.VMEM((nrow, MLA_ROPE), F32),
                        pltpu.VMEM((MLD_CHAINS, nrow, 1), F32), pltpu.VMEM((MLD_CHAINS, nrow, 1), F32),
                        pltpu.VMEM((MLD_CHAINS, nrow, MLA_KV_LORA), F32),
                        pltpu.VMEM((2, P, page, MLA_KV_LORA), F32), pltpu.VMEM((2, P, MLA_ROPE, page), F32),
                        pltpu.SemaphoreType.DMA((2, 2))],
    )
    return pl.pallas_call(
        functools.partial(_mld_kernel, pages=P, page=page, layer=layer),
        grid_spec=grid_spec,
        out_shape=jax.ShapeDtypeStruct((nb * SAMPLE_PAD, ow), F32),
        compiler_params=_params(("arbitrary", "arbitrary")),
        name="mld",
    )(page_table, qm, ckv_new, kr_new, new_mask, lw['wuk'], lw['wuva'], lw['wuvb'], cache_c, cache_rt)


def _post_kernel(x_ref, ym_ref, yd_ref, yc_ref, gate_ref, g_ref, wbm, wbd, wbc, wout, wfi, wfo, o_ref):
    u = None
    for j, (y_ref, w_ref) in enumerate(((ym_ref, wbm), (yd_ref, wbd), (yc_ref, wbc))):
        t = gate_ref[:, j * D_MODEL:(j + 1) * D_MODEL].astype(F32) * jnp.dot(
            y_ref[...].astype(BF16), w_ref[...], preferred_element_type=F32)
        u = t if u is None else u + t
    x = x_ref[...] + _rms(jnp.dot(u.astype(BF16), wout[...], preferred_element_type=F32), g_ref[0:1])
    hf = _rms(x, g_ref[1:2]).astype(BF16)
    cw = D_FF // 2
    t = None
    for j in range(2):
        a = jnp.dot(hf, wfi[:, j * cw:(j + 1) * cw], preferred_element_type=F32)
        bb = jnp.dot(hf, wfi[:, D_FF + j * cw:D_FF + (j + 1) * cw], preferred_element_type=F32)
        s = (a * jax.nn.sigmoid(a) * bb).astype(BF16)
        d = jnp.dot(s, wfo[j * cw:(j + 1) * cw, :], preferred_element_type=F32)
        t = d if t is None else t + d
    o_ref[...] = x + _rms(t, g_ref[2:3])


def _post(x, ym, yd, yc, gates, lw):
    rows = x.shape[0]
    tm = min(ROW_TILE, rows)
    row = lambda width: pl.BlockSpec((tm, width), lambda i: (i, 0))
    weights = [lw['g123'], lw['w_br_ml'], lw['w_br_df'], lw['w_br_mla'], lw['w_out'], lw['w_ffn_in'], lw['w_ffn_out']]
    return pl.pallas_call(
        _post_kernel,
        grid=(rows // tm,),
        in_specs=[row(D_MODEL), row(512), row(512), row(512), row(N_BRANCH * D_MODEL)]
                 + [_const_spec(w.shape) for w in weights],
        out_specs=row(D_MODEL),
        out_shape=jax.ShapeDtypeStruct((rows, D_MODEL), F32),
        compiler_params=_params(("parallel",)),
        name="post",
    )(x, ym, yd, yc, gates, *weights)


def _rot_half_cols(w):
    half = w.shape[-1] // 2
    return jnp.concatenate([-w[..., half:], w[..., :half]], axis=-1)


def _head_pad(nope, rope):
    k, h = nope.shape[0], nope.shape[1]
    pad = jnp.zeros((k, h, HEAD_PAD - MLA_NOPE - MLA_ROPE), nope.dtype)
    return jnp.concatenate([nope, rope, pad], axis=-1).reshape(k, h * HEAD_PAD)


def _layer_weights(l, norm_gains, w_in, b_ml_gates, ml_head_norm, df_lambda, df_head_norm, mla_q_norm,
                   mla_kv_norm, w_uq, w_uk, w_uv, w_br_ml, w_br_df, w_br_mla, w_out, w_ffn_in, w_ffn_out):
    offs = np.cumsum((0,) + IN_SIZES)
    col = lambda i: w_in[l][:, offs[i]:offs[i + 1]]
    mq, mk, mv, mo, mi, mf, dq, dk, dv, cq, ckv, kr, gl = (col(i) for i in range(len(IN_SIZES)))
    zeros = lambda n: jnp.zeros((D_MODEL, n), F32)
    rope_grp = lambda w: jnp.concatenate([zeros(MLA_NOPE), w, zeros(HEAD_PAD - MLA_NOPE - MLA_ROPE)], axis=1)
    wmisc = jnp.concatenate([mi, mf, zeros(LANES - 2 * ML_HEADS), rope_grp(kr), rope_grp(_rot_half_cols(kr))], axis=1)
    uq = w_uq[l].reshape(MLA_Q_LORA, MLA_HEADS, MLA_NOPE + MLA_ROPE)
    uq_n, uq_r = uq[..., :MLA_NOPE], uq[..., MLA_NOPE:]
    uv = w_uv[l]
    zv = jnp.zeros_like(uv[:, 0::2])
    bf = lambda a: a.astype(BF16)
    return {
        'g0': norm_gains[l, 0:1], 'g123': norm_gains[l, 1:4],
        'wml': bf(jnp.concatenate([mq, mk, mv, mo], axis=1)), 'wdf': bf(jnp.concatenate([dq, dk, dv], axis=1)),
        'wcq': bf(cq), 'wckv': bf(ckv), 'wmisc': bf(wmisc), 'wgl': bf(gl),
        'q_norm': mla_q_norm[l][None], 'kv_norm': mla_kv_norm[l][None],
        'wuqa': bf(_head_pad(uq_n, uq_r)), 'wuqb': bf(_head_pad(jnp.zeros_like(uq_n), _rot_half_cols(uq_r))),
        'wuk': bf(_head_pad(w_uk[l], jnp.zeros((MLA_KV_LORA, MLA_HEADS, MLA_ROPE), F32))),
        'wdvt': bf(dv.T), 'wuvt': bf(uv.reshape(MLA_KV_LORA, MLA_HEADS * MLA_V).T),
        'wuva': bf(jnp.concatenate([uv[:, 0::2], zv], axis=-1).reshape(MLA_KV_LORA, MLA_HEADS * MLA_V)),
        'wuvb': bf(jnp.concatenate([zv, uv[:, 1::2]], axis=-1).reshape(MLA_KV_LORA, MLA_HEADS * MLA_V)),
        'gate_bias': jnp.concatenate([b_ml_gates[l], jnp.zeros((LANES - 2 * ML_HEADS,), F32)])[None],
        'ml_norm': ml_head_norm[l][None], 'df_lambda': df_lambda[l], 'df_norm': df_head_norm[l][None],
        'w_br_ml': bf(w_br_ml[l]), 'w_br_df': bf(w_br_df[l]), 'w_br_mla': bf(w_br_mla[l]),
        'w_out': bf(w_out[l]), 'w_ffn_in': bf(w_ffn_in[l]), 'w_ffn_out': bf(w_ffn_out[l]),
    }


def _rope_tables(pos):
    half = MLA_ROPE // 2
    freqs = ROPE_THETA ** (-jnp.arange(half, dtype=F32) / half)
    ang = pos.astype(F32)[:, None] * freqs[None, :]
    cos, sin = jnp.cos(ang), jnp.sin(ang)
    n = pos.shape[0]
    tail = jnp.zeros((n, HEAD_PAD - MLA_NOPE - MLA_ROPE), F32)
    cs = jnp.concatenate([jnp.ones((n, MLA_NOPE), F32), cos, cos, tail], axis=1)
    sn = jnp.concatenate([jnp.zeros((n, MLA_NOPE), F32), sin, sin, tail], axis=1)
    return cs, sn


def kernel(x_prompt, x_sample, state_mlstm_C, state_mlstm_n, state_mlstm_m, cache_diff_k, cache_diff_v,
           cache_mla_ckv, cache_mla_krope, page_table, norm_gains, w_in, b_ml_gates, ml_head_norm, df_lambda,
           df_head_norm, rel_bias, mla_q_norm, mla_kv_norm, w_uq, w_uk, w_uv, w_br_ml, w_br_df, w_br_mla, w_out,
           w_ffn_in, w_ffn_out):
    depth = w_in.shape[0]
    B, S, _ = x_prompt.shape
    DB, DS, _ = x_sample.shape
    n_pages = page_table.shape[1]
    n_pool, page = cache_diff_k.shape[1], cache_diff_k.shape[2]
    past_len = n_pages * page
    assert DS <= SAMPLE_PAD
    T = ATT_TILE

    cs_p, sn_p = _rope_tables(jnp.arange(S, dtype=jnp.int32))
    pos_s = past_len + jnp.arange(SAMPLE_PAD, dtype=jnp.int32)
    cs_s, sn_s = _rope_tables(jnp.tile(pos_s, min(DB, ROW_TILE // SAMPLE_PAD)))
    ii, jj = np.meshgrid(np.arange(T), np.arange(T), indexing='ij')
    diag, sub = _t5_bucket_np(jj - ii), _t5_bucket_np(T + jj - ii)
    bias_p = _bias_table(rel_bias, np.concatenate([diag, sub], axis=0)).reshape(DF_HEADS, 2, T, T)
    rr, kk = np.meshgrid(np.arange(SAMPLE_PAD), np.arange(page), indexing='ij')
    new_ok = (kk <= rr) & (kk < DS)
    idx_d = np.concatenate([np.full((SAMPLE_PAD, page), REL_BUCKETS - 1, np.int32),
                            _t5_bucket_np(page + rr - kk),
                            np.where(new_ok, _t5_bucket_np(rr - kk), -1)], axis=0)
    bias_d = _bias_table(rel_bias, idx_d).reshape(DF_HEADS, 3, 1, SAMPLE_PAD, page)
    bias_d = jnp.broadcast_to(bias_d, (DF_HEADS, 3, 2, SAMPLE_PAD, page))
    bias_d = jnp.transpose(bias_d, (1, 0, 2, 3, 4)).reshape(3, DF_HEADS * 2 * SAMPLE_PAD, page)
    mla_new_mask = jnp.asarray(np.tile(np.where(new_ok, 0.0, NEG_INF).astype(np.float32), (MLA_HEADS, 1)))

    ckt = jnp.transpose(cache_diff_k, (0, 1, 3, 4, 5, 2)).reshape(depth, n_pool, DF_HEADS * 2 * DF_DK, page)
    crt = jnp.transpose(cache_mla_krope, (0, 1, 3, 2))
    cv = cache_diff_v.reshape(depth, n_pool, page * DF_HEADS, DF_DV)

    xp = x_prompt.reshape(B * S, D_MODEL)
    xs = jnp.pad(x_sample, ((0, 0), (0, SAMPLE_PAD - DS), (0, 0))).reshape(DB * SAMPLE_PAD, D_MODEL)
    zero_c = jnp.zeros((B, ML_HEADS, ML_DK, ML_DV), F32)
    zero_n = jnp.zeros((B, ML_HEADS, ML_DK), F32)
    zero_m = jnp.zeros((B, ML_HEADS), F32)
    st_p, st_s = [], []
    for l in range(depth):
        lw = _layer_weights(l, norm_gains, w_in, b_ml_gates, ml_head_norm, df_lambda, df_head_norm, mla_q_norm,
                            mla_kv_norm, w_uq, w_uk, w_uv, w_br_ml, w_br_df, w_br_mla, w_out, w_ffn_in, w_ffn_out)
        lam_init = 0.8 - 0.6 * math.exp(-0.3 * l)

        (ml, gif, dq, dk, dv, qm, ckv, kr, gates, dkb, dvt, km, vmt) = _proj(xp, lw, cs_p, sn_p, True)
        ym, c_p, n_p, m_p = _mlstm(ml, gif, lw, zero_c, zero_n, zero_m, B, S, math.gcd(S, ML_CHUNK), S)
        yd = _dfp(dq, dkb, dvt, bias_p, rel_bias, lw, B, S, lam_init)
        yc = _mlp(qm, km, vmt, B, S)
        xp = _post(xp, ym, yd, yc, gates, lw)
        st_p.append((dk.reshape(B, S, DF_HEADS, 2, DF_DK), dv.reshape(B, S, DF_HEADS, DF_DV),
                     ckv.reshape(B, S, MLA_KV_LORA), kr.reshape(B, S, MLA_ROPE), c_p, n_p, m_p))

        (ml, gif, dq, dk, dv, qm, ckv, kr, gates) = _proj(xs, lw, cs_s, sn_s, False)
        ym, c_s, n_s, m_s = _mlstm(ml, gif, lw, state_mlstm_C[l], state_mlstm_n[l], state_mlstm_m[l],
                                   DB, SAMPLE_PAD, SAMPLE_PAD, DS)
        yd = _dfd(l, dq, dk, dv, bias_d, ckt, cv, page_table, lw, lam_init)
        yc = _mld(l, qm, ckv, kr, mla_new_mask, cache_mla_ckv, crt, page_table, lw)
        xs = _post(xs, ym, yd, yc, gates, lw)
        tok = lambda a, *tail: a.reshape((DB, SAMPLE_PAD) + tail)[:, :DS]
        st_s.append((tok(dk, DF_HEADS, 2, DF_DK), tok(dv, DF_HEADS, DF_DV), tok(ckv, MLA_KV_LORA),
                     tok(kr, MLA_ROPE), c_s, n_s, m_s))

    outs_p = [jnp.stack(a) for a in zip(*st_p)]
    outs_s = [jnp.stack(a) for a in zip(*st_s)]
    yp = xp.reshape(B, S, D_MODEL)
    ys = xs.reshape(DB, SAMPLE_PAD, D_MODEL)[:, :DS]
    return (yp, ys, *outs_p, *outs_s)
```

```python
import functools
import math

import numpy as np
import jax
import jax.numpy as jnp
from jax import lax
from jax.experimental import pallas as pl
from jax.experimental.pallas import tpu as pltpu

F32 = jnp.float32
BF16 = jnp.bfloat16

D_MODEL = 1024
ML_HEADS = 4
ML_DK = 128
ML_DV = 128
DF_HEADS = 4
DF_DK = 64
DF_DV = 2 * DF_DK
MLA_HEADS = 8
MLA_NOPE = 64
MLA_ROPE = 32
MLA_V = 64
MLA_Q_LORA = 384
MLA_KV_LORA = 256
ROPE_THETA = 10000.0
REL_BUCKETS = 32
REL_MAX_DIST = 128
N_BRANCH = 3
D_FF = 2816
EPS = 1e-6
NEG_INF = -1e30
LOG2E = math.log2(math.e)
IN_SIZES = (ML_HEADS * ML_DK, ML_HEADS * ML_DK, ML_HEADS * ML_DV, ML_HEADS * ML_DV, ML_HEADS, ML_HEADS,
            DF_HEADS * 2 * DF_DK, DF_HEADS * 2 * DF_DK, DF_HEADS * DF_DV,
            MLA_Q_LORA, MLA_KV_LORA, MLA_ROPE,
            N_BRANCH * D_MODEL)

LANES = 128
SUBLANES = 8
HEAD_PAD = 128
ROW_TILE = 256
ATT_TILE = 512
DFP_HEADS_PER_STEP = 2
MLP_HEADS_PER_STEP = 4
ML_CHUNK = 256
SAMPLE_PAD = 8
DFD_PAGES = 32
MLD_PAGES = 64
MLD_CHAINS = 2
VMEM_LIMIT = 56 * 1024 * 1024

_NT = (((1,), (1,)), ((), ()))
_TN = (((0,), (0,)), ((), ()))


def _params(sem):
    return pltpu.CompilerParams(dimension_semantics=sem, vmem_limit_bytes=VMEM_LIMIT)


def _const_spec(shape):
    nd = len(shape)
    return pl.BlockSpec(shape, lambda *_: (0,) * nd, pipeline_mode=pl.Buffered(1))


def _rms(x, g):
    return x * lax.rsqrt(jnp.mean(x * x, axis=-1, keepdims=True) + EPS) * g


def _t5_bucket_np(n):
    n = np.asarray(n, np.int64)
    exact = REL_BUCKETS // 2
    nf = np.maximum(n, 1).astype(np.float32)
    large = exact + (np.log(nf / np.float32(exact)) / np.float32(math.log(REL_MAX_DIST / exact))
                     * np.float32(REL_BUCKETS - exact)).astype(np.int32)
    large = np.minimum(large, REL_BUCKETS - 1)
    b = np.where(n < exact, n, large)
    return np.where(n < 0, -1, b).astype(np.int32)


def _bias_table_kernel(rb_ref, idx_ref, out_ref):
    idx = idx_ref[...]
    for h in range(DF_HEADS):
        acc = jnp.full(idx.shape, NEG_INF, F32)
        for b in range(REL_BUCKETS):
            acc = jnp.where(idx == b, rb_ref[b, h] * LOG2E, acc)
        out_ref[h] = acc


def _bias_table(rel_bias, idx_np):
    r, c = idx_np.shape
    return pl.pallas_call(
        _bias_table_kernel,
        out_shape=jax.ShapeDtypeStruct((DF_HEADS, r, c), F32),
        in_specs=[pl.BlockSpec(memory_space=pltpu.SMEM), pl.BlockSpec(memory_space=pltpu.VMEM)],
        out_specs=pl.BlockSpec(memory_space=pltpu.VMEM),
        name="bias_table",
    )(rel_bias, jnp.asarray(idx_np))


def _proj_kernel(x_ref, g_ref, cs_ref, sn_ref, wml, wdf, wcq, wckv, wmisc, wgl, qn_ref, kvn_ref,
                 wuqa, wuqb, wuk, wdvt, wuvt, wdkt, *rest, prompt, n_aliased):
    ml_o, gif_o, dq_o, dk_o, dv_o, qm_o, ckv_o, kr_o, gate_o, *prompt_outs = rest[n_aliased:]
    x = x_ref[...]
    h = _rms(x, g_ref[...]).astype(BF16)

    def mm(w_ref, lo, hi):
        return jnp.dot(h, w_ref[:, lo:hi], preferred_element_type=F32)

    w = ML_HEADS * ML_DK
    act = ml_o.dtype
    ml_o[:, 0:w] = mm(wml, 0, w).astype(act)
    ml_o[:, w:2 * w] = (mm(wml, w, 2 * w) * ML_DK ** -0.5).astype(act)
    ml_o[:, 2 * w:3 * w] = mm(wml, 2 * w, 3 * w).astype(act)
    ml_o[:, 3 * w:4 * w] = jax.nn.sigmoid(mm(wml, 3 * w, 4 * w)).astype(act)

    misc = mm(wmisc, 0, 3 * LANES)
    gif_o[...] = misc[:, 0:LANES]
    cs = cs_ref[...]
    sn = sn_ref[...]
    krp = misc[:, LANES:2 * LANES] * cs + misc[:, 2 * LANES:3 * LANES] * sn
    kr_o[...] = krp[:, MLA_NOPE:MLA_NOPE + MLA_ROPE]

    wd = DF_HEADS * 2 * DF_DK
    dq_o[...] = (mm(wdf, 0, wd) * (DF_DK ** -0.5 * LOG2E)).astype(act)
    dk = mm(wdf, wd, 2 * wd)
    dv = mm(wdf, 2 * wd, 3 * wd)
    if prompt:
        dk_o[...] = lax.dot_general(wdkt[...], h, _NT, preferred_element_type=F32)
        for hh in range(DF_HEADS):
            dv_o[pl.ds(hh, x.shape[0], stride=DF_HEADS), :] = dv[:, hh * DF_DV:(hh + 1) * DF_DV]
    else:
        dk_o[...] = dk
        dv_o[...] = dv

    c_q = _rms(mm(wcq, 0, MLA_Q_LORA), qn_ref[...]).astype(BF16)
    c_kv = _rms(mm(wckv, 0, MLA_KV_LORA), kvn_ref[...])
    ckv_o[...] = c_kv
    scale = (MLA_NOPE + MLA_ROPE) ** -0.5 * LOG2E
    for hh in range(MLA_HEADS):
        sl = slice(hh * HEAD_PAD, (hh + 1) * HEAD_PAD)
        qa = jnp.dot(c_q, wuqa[:, sl], preferred_element_type=F32)
        qb = jnp.dot(c_q, wuqb[:, sl], preferred_element_type=F32)
        qm_o[:, sl] = ((qa * cs + qb * sn) * scale).astype(act)

    gw = 512
    for j in range(N_BRANCH * D_MODEL // gw):
        gate_o[:, j * gw:(j + 1) * gw] = jax.nn.sigmoid(mm(wgl, j * gw, (j + 1) * gw)).astype(BF16)

    if prompt:
        dkb_o, dvt_o, km_o, vmt_o = prompt_outs
        dkb_o[...] = dk.astype(BF16)
        dvt_o[...] = lax.dot_general(wdvt[...], h, _NT, preferred_element_type=F32).astype(BF16)
        ckb = c_kv.astype(BF16)
        for hh in range(MLA_HEADS):
            sl = slice(hh * HEAD_PAD, (hh + 1) * HEAD_PAD)
            kn = jnp.dot(ckb, wuk[:, sl], preferred_element_type=F32)
            km_o[:, sl] = (kn + krp).astype(BF16)
        vmt_o[...] = lax.dot_general(wuvt[...], ckb, _NT, preferred_element_type=F32).astype(BF16)


_PROJ_LEAVES = (3, 4, 6, 7)


def _proj(x, lw, cs, sn, prompt, layer=0, depth=0, stacked=None):
    rows = x.shape[0]
    tm = min(ROW_TILE, rows)
    assert rows % tm == 0
    row = lambda width: pl.BlockSpec((tm, width), lambda i: (i, 0))
    assert cs.shape[0] % tm == 0 and rows % cs.shape[0] == 0
    pos_tiles = cs.shape[0] // tm
    pos_row = pl.BlockSpec((tm, LANES), lambda i: (i % pos_tiles, 0))
    weights = [lw['wml'], lw['wdf'], lw['wcq'], lw['wckv'], lw['wmisc'], lw['wgl'], lw['q_norm'], lw['kv_norm'],
               lw['wuqa'], lw['wuqb'], lw['wuk'], lw['wdvt'], lw['wuvt'], lw['wdkt']]
    act = BF16 if prompt else F32
    out_widths = [(4 * ML_HEADS * ML_DK, act), (LANES, F32), (512, act), (512, F32), (512, F32),
                  (MLA_HEADS * HEAD_PAD, act), (MLA_KV_LORA, F32), (MLA_ROPE, F32), (N_BRANCH * D_MODEL, BF16)]
    out_specs = [row(wd) for wd, _ in out_widths]
    out_shape = [jax.ShapeDtypeStruct((rows, wd), dt) for wd, dt in out_widths]
    aliased = list(stacked) if stacked is not None else []
    if depth:
        assert stacked is not None
        for k in _PROJ_LEAVES:
            wd, dt = out_widths[k]
            out_specs[k] = pl.BlockSpec((None, tm, wd), lambda i: (layer, i, 0))
            out_shape[k] = jax.ShapeDtypeStruct((depth, rows, wd), dt)
        seq, wd = cs.shape[0], out_widths[_PROJ_LEAVES[0]][0]
        out_specs[_PROJ_LEAVES[0]] = pl.BlockSpec((None, None, wd, tm),
                                                  lambda i: (layer, i // pos_tiles, 0, i % pos_tiles))
        out_shape[_PROJ_LEAVES[0]] = jax.ShapeDtypeStruct((depth, rows // seq, wd, seq), F32)
        out_specs[_PROJ_LEAVES[1]] = pl.BlockSpec((None, tm * DF_HEADS, DF_DV), lambda i: (layer, i, 0))
        out_shape[_PROJ_LEAVES[1]] = jax.ShapeDtypeStruct((depth, rows * DF_HEADS, DF_DV), F32)
    n_in = 4 + len(weights)
    if prompt:
        col = lambda height: pl.BlockSpec((height, tm), lambda i: (0, i))
        out_specs += [row(512), col(512), row(MLA_HEADS * HEAD_PAD), col(512)]
        out_shape += [jax.ShapeDtypeStruct((rows, 512), BF16), jax.ShapeDtypeStruct((512, rows), BF16),
                      jax.ShapeDtypeStruct((rows, MLA_HEADS * HEAD_PAD), BF16),
                      jax.ShapeDtypeStruct((512, rows), BF16)]
    return pl.pallas_call(
        functools.partial(_proj_kernel, prompt=prompt, n_aliased=len(aliased)),
        grid=(rows // tm,),
        in_specs=[row(D_MODEL), _const_spec((1, D_MODEL)), pos_row, pos_row]
                 + [_const_spec(w.shape) for w in weights]
                 + [pl.BlockSpec(memory_space=pl.ANY)] * len(aliased),
        out_specs=out_specs,
        out_shape=out_shape,
        input_output_aliases={n_in + j: k for j, k in enumerate(_PROJ_LEAVES[:len(aliased)])},
        compiler_params=_params(("parallel",)),
        name="proj_prompt" if prompt else "proj_sample",
    )(x, lw['g0'], cs, sn, *weights, *aliased)


def _mlstm_kernel(m0_ref, ml_ref, gif_ref, gb_ref, hn_ref, c0_ref, n0_ref,
                  ym_o, c_o, n_o, m_o, cn_scr, m_scr, *, chunk, t_valid):
    b = pl.program_id(0)
    c = pl.program_id(1)
    L = chunk
    dk, dv = ML_DK, ML_DV
    lane = lax.broadcasted_iota(jnp.int32, (1, LANES), 1)

    @pl.when(c == 0)
    def _():
        for h in range(ML_HEADS):
            ncol = jnp.where(lax.broadcasted_iota(jnp.int32, (dk, LANES), 1) == 0, n0_ref[0, h], 0.0)
            cn_scr[h] = jnp.concatenate([c0_ref[0, h], ncol], axis=1)
            m_scr[h] = jnp.full((SUBLANES, LANES), m0_ref[b, h], F32)

    g = gif_ref[...] + gb_ref[...]
    g = jnp.where(lane < ML_HEADS, g, jnp.minimum(g, 0.0) - jnp.log(1.0 + jnp.exp(-jnp.abs(g))))
    row_i = lax.broadcasted_iota(jnp.int32, (L, L), 0)
    col_i = lax.broadcasted_iota(jnp.int32, (L, L), 1)
    if t_valid < L:
        tok = lax.broadcasted_iota(jnp.int32, (L, LANES), 0)
        g = jnp.where(tok < t_valid, g, jnp.where(lane < ML_HEADS, NEG_INF, 0.0))
    causal = col_i <= row_i
    eye = col_i == row_i

    for h in range(ML_HEADS):
        lic = g[:, h:h + 1]
        lfc = g[:, ML_HEADS + h:ML_HEADS + h + 1]
        br = jnp.sum(jnp.where(row_i <= col_i, lfc, 0.0), axis=0, keepdims=True)
        bc = jnp.sum(jnp.where(eye, br, 0.0), axis=1, keepdims=True)
        ir = jnp.sum(jnp.where(eye, lic, 0.0), axis=0, keepdims=True)
        b_last = jnp.sum(lfc, axis=0, keepdims=True)
        m_prev = m_scr[h][0:1, 0:1]

        q = ml_ref[:, h * dk:(h + 1) * dk].astype(BF16)
        k = ml_ref[:, (ML_HEADS + h) * dk:(ML_HEADS + h + 1) * dk].astype(BF16)
        v = ml_ref[:, (2 * ML_HEADS + h) * dk:(2 * ML_HEADS + h + 1) * dk].astype(BF16)
        og = ml_ref[:, (3 * ML_HEADS + h) * dk:(3 * ML_HEADS + h + 1) * dk]

        dmat = jnp.where(causal, bc - br + ir, NEG_INF)
        inter = bc + m_prev
        mt = jnp.maximum(inter, jnp.max(dmat, axis=1, keepdims=True))
        wgt = jnp.exp(dmat - mt)
        iw = jnp.exp(inter - mt)
        a = lax.dot_general(q, k, _NT, preferred_element_type=F32) * wgt
        cn = cn_scr[h]
        qc = jnp.dot(q, cn.astype(BF16), preferred_element_type=F32)
        num = jnp.dot(a.astype(BF16), v, preferred_element_type=F32) + iw * qc[:, 0:dv]
        den = jnp.sum(a, axis=1, keepdims=True) + iw * qc[:, dv:dv + 1]
        hh = num / jnp.maximum(jnp.abs(den), jnp.exp(-mt))
        y = _rms(hh, hn_ref[:, h * dv:(h + 1) * dv]) * og.astype(F32)
        ym_o[:, h * dv:(h + 1) * dv] = y.astype(ym_o.dtype)

        m_new = jnp.maximum(b_last + m_prev, jnp.max(b_last - br + ir, axis=1, keepdims=True))
        wl = jnp.exp(b_last - bc + lic - m_new)
        dec = jnp.exp(b_last + m_prev - m_new)
        ones_col = jnp.where(lax.broadcasted_iota(jnp.int32, (L, LANES), 1) == 0, wl, 0.0)
        wv = jnp.concatenate([wl * v.astype(F32), ones_col], axis=1).astype(BF16)
        cn_new = dec * cn + lax.dot_general(k, wv, _TN, preferred_element_type=F32)
        cn_scr[h] = cn_new
        m_scr[h] = jnp.broadcast_to(m_new, (SUBLANES, LANES))
        c_o[0, h] = cn_new[:, 0:dv]
        n_o[0, h] = cn_new[:, dv:dv + 1]
        m_o[0, h] = jnp.broadcast_to(m_new, (SUBLANES, LANES))


def _mlstm(ml, gif, lw, c0, n0, m0, batch, seq, chunk, t_valid):
    nc = seq // chunk
    assert seq % chunk == 0
    width = ML_HEADS * ML_DV
    grid_spec = pltpu.PrefetchScalarGridSpec(
        num_scalar_prefetch=1,
        grid=(batch, nc),
        in_specs=[
            pl.BlockSpec((chunk, 4 * width), lambda b, c, m: (b * nc + c, 0)),
            pl.BlockSpec((chunk, LANES), lambda b, c, m: (b * nc + c, 0)),
            pl.BlockSpec((1, LANES), lambda b, c, m: (0, 0)),
            pl.BlockSpec((1, width), lambda b, c, m: (0, 0)),
            pl.BlockSpec((1, ML_HEADS, ML_DK, ML_DV), lambda b, c, m: (b, 0, 0, 0)),
            pl.BlockSpec((1, ML_HEADS, ML_DK, 1), lambda b, c, m: (b, 0, 0, 0)),
        ],
        out_specs=[
            pl.BlockSpec((chunk, width), lambda b, c, m: (b * nc + c, 0)),
            pl.BlockSpec((1, ML_HEADS, ML_DK, ML_DV), lambda b, c, m: (b, 0, 0, 0)),
            pl.BlockSpec((1, ML_HEADS, ML_DK, 1), lambda b, c, m: (b, 0, 0, 0)),
            pl.BlockSpec((1, ML_HEADS, SUBLANES, LANES), lambda b, c, m: (b, 0, 0, 0)),
        ],
        scratch_shapes=[pltpu.VMEM((ML_HEADS, ML_DK, 2 * ML_DV), F32),
                        pltpu.VMEM((ML_HEADS, SUBLANES, LANES), F32)],
    )
    ym, c_new, n_new, m_new = pl.pallas_call(
        functools.partial(_mlstm_kernel, chunk=chunk, t_valid=t_valid),
        grid_spec=grid_spec,
        out_shape=[jax.ShapeDtypeStruct((batch * seq, width), ml.dtype),
                   jax.ShapeDtypeStruct((batch, ML_HEADS, ML_DK, ML_DV), F32),
                   jax.ShapeDtypeStruct((batch, ML_HEADS, ML_DK, 1), F32),
                   jax.ShapeDtypeStruct((batch, ML_HEADS, SUBLANES, LANES), F32)],
        compiler_params=_params(("parallel", "arbitrary")),
        name="mlstm",
    )(m0, ml, gif, lw['gate_bias'], lw['ml_norm'], c0, n0[..., None])
    return ym, c_new, n_new[..., 0], m_new[:, :, 0, 0]


def _softmax_first(s, pv, axis=1):
    m = jnp.max(s, axis=axis, keepdims=True)
    p = jnp.exp2(s - m)
    return m, jnp.sum(p, axis=axis, keepdims=True), pv(p.astype(BF16))


def _softmax_next(carry, s, pv, axis=1, shift=None):
    m, l, acc = carry
    top = jnp.max(s, axis=axis, keepdims=True)
    m_new = jnp.maximum(m, top if shift is None else top + shift)
    alpha = jnp.exp2(m - m_new)
    p = jnp.exp2(s - (m_new if shift is None else m_new - shift))
    return m_new, alpha * l + jnp.sum(p, axis=axis, keepdims=True), alpha * acc + pv(p.astype(BF16))


def _lambda(lv_ref, lam_init):
    lv = lv_ref[...]
    e1 = jnp.exp(jnp.sum(lv[0:1] * lv[1:2], axis=1, keepdims=True))
    e2 = jnp.exp(jnp.sum(lv[2:3] * lv[3:4], axis=1, keepdims=True))
    return e1 - e2 + lam_init


def _dfp_kernel(rb_ref, q_ref, k_ref, vt_ref, bias_ref, lv_ref, hn_ref, o_ref, *, tile, heads, lam_init):
    hg = pl.program_id(1)
    qi = pl.program_id(2)
    T = tile
    hw = 2 * DF_DK
    lane = lax.broadcasted_iota(jnp.int32, (T, hw), 1)
    qs = []
    for j in range(heads):
        q = q_ref[:, j * hw:(j + 1) * hw]
        zero = jnp.zeros_like(q)
        qs += [jnp.where(lane < DF_DK, q, zero), jnp.where(lane >= DF_DK, q, zero)]

    def tile_scores(start):
        s, pvs = [], []
        for c in range(2 * heads):
            j = c // 2
            kt = k_ref[pl.ds(start, T), j * hw:(j + 1) * hw]
            s.append(lax.dot_general(kt, qs[c], _NT, preferred_element_type=F32))
            pvs.append(lambda p, j=j: jnp.dot(vt_ref[j * DF_DV:(j + 1) * DF_DV, pl.ds(start, T)], p,
                                              preferred_element_type=F32))
        return s, pvs

    s, pvs = tile_scores(pl.multiple_of(qi * T, T))
    carry = tuple(_softmax_first(sm + bias_ref[c // 2, 0], pv, axis=0) for c, (sm, pv) in enumerate(zip(s, pvs)))
    sub = jnp.maximum(qi - 1, 0)
    s, pvs = tile_scores(pl.multiple_of(sub * T, T))
    gone = jnp.where(qi == 0, NEG_INF, 0.0)
    carry = tuple(_softmax_next(st, sm + (bias_ref[c // 2, 1] + gone), pv, axis=0)
                  for c, (st, sm, pv) in enumerate(zip(carry, s, pvs)))
    far_bias = [rb_ref[REL_BUCKETS - 1, hg * heads + j] * LOG2E for j in range(heads)]

    def far(ki, carry):
        s, pvs = tile_scores(pl.multiple_of(ki * T, T))
        return tuple(_softmax_next(st, sm, pv, axis=0, shift=far_bias[c // 2])
                     for c, (st, sm, pv) in enumerate(zip(carry, s, pvs)))

    carry = lax.fori_loop(0, sub, far, carry)
    lam = _lambda(lv_ref, lam_init)
    for j in range(heads):
        (_, l0, acc0), (_, l1, acc1) = carry[2 * j], carry[2 * j + 1]
        od = (acc0 / l0 - lam * (acc1 / l1)).T
        sl = slice(j * DF_DV, (j + 1) * DF_DV)
        o_ref[:, sl] = (_rms(od, hn_ref[:, sl]) * (1.0 - lam_init)).astype(BF16)


def _dfp(dq, dkb, dvt, bias_tiles, rel_bias, lw, batch, seq, lam_init):
    T = ATT_TILE
    nq = seq // T
    assert seq % T == 0 and T >= REL_MAX_DIST
    hs = DFP_HEADS_PER_STEP
    hw = hs * 2 * DF_DK
    grid_spec = pltpu.PrefetchScalarGridSpec(
        num_scalar_prefetch=0,
        grid=(batch, DF_HEADS // hs, nq),
        in_specs=[
            pl.BlockSpec(memory_space=pltpu.SMEM),
            pl.BlockSpec((T, hw), lambda b, h, i: (b * nq + i, h)),
            pl.BlockSpec((seq, hw), lambda b, h, i: (b, h)),
            pl.BlockSpec((hs * DF_DV, seq), lambda b, h, i: (h, b)),
            pl.BlockSpec((hs, 2, T, T), lambda b, h, i: (h, 0, 0, 0)),
            pl.BlockSpec((4, DF_DK), lambda b, h, i: (0, 0)),
            pl.BlockSpec((1, hw), lambda b, h, i: (0, h)),
        ],
        out_specs=pl.BlockSpec((T, hw), lambda b, h, i: (b * nq + i, h)),
    )
    return pl.pallas_call(
        functools.partial(_dfp_kernel, tile=T, heads=hs, lam_init=lam_init),
        grid_spec=grid_spec,
        out_shape=jax.ShapeDtypeStruct((batch * seq, DF_HEADS * 2 * DF_DK), BF16),
        compiler_params=_params(("parallel", "parallel", "arbitrary")),
        name="dfp",
    )(rel_bias, dq, dkb, dvt, bias_tiles, lw['df_lambda'], lw['df_norm'])


def _mlp_kernel(q_ref, k_ref, vt_ref, o_ref, *, tile, heads):
    qi = pl.program_id(2)
    T = tile
    qs = [q_ref[:, j * HEAD_PAD:(j + 1) * HEAD_PAD] for j in range(heads)]
    key_i = lax.broadcasted_iota(jnp.int32, (T, T), 0)
    qry_i = lax.broadcasted_iota(jnp.int32, (T, T), 1)
    mask = jnp.where(key_i <= qry_i, 0.0, NEG_INF)
    vw = 2 * MLA_V

    def tile_scores(start):
        kt = k_ref[pl.ds(start, T), :]
        s = [lax.dot_general(kt[:, j * HEAD_PAD:(j + 1) * HEAD_PAD], qs[j], _NT, preferred_element_type=F32)
             for j in range(heads)]
        pvs = [lambda p, j=j: jnp.dot(vt_ref[(j // 2) * vw:(j // 2 + 1) * vw, pl.ds(start, T)], p,
                                      preferred_element_type=F32) for j in range(heads)]
        return s, pvs

    s, pvs = tile_scores(pl.multiple_of(qi * T, T))
    carry = tuple(_softmax_first(sm + mask, pv, axis=0) for sm, pv in zip(s, pvs))

    def far(ki, carry):
        s, pvs = tile_scores(pl.multiple_of(ki * T, T))
        return tuple(_softmax_next(c, sm, pv, axis=0) for c, sm, pv in zip(carry, s, pvs))

    carry = lax.fori_loop(0, qi, far, carry)
    row = lax.broadcasted_iota(jnp.int32, (vw, T), 0)
    for p in range(heads // 2):
        (_, la, acca), (_, lb, accb) = carry[2 * p], carry[2 * p + 1]
        o_ref[:, p * vw:(p + 1) * vw] = jnp.where(row < MLA_V, acca / la, accb / lb).T.astype(BF16)


def _mlp(qm, km, vmt, batch, seq):
    T = ATT_TILE
    nq = seq // T
    hs = MLP_HEADS_PER_STEP
    pw = hs * HEAD_PAD
    vw = hs * MLA_V
    return pl.pallas_call(
        functools.partial(_mlp_kernel, tile=T, heads=hs),
        grid=(batch, MLA_HEADS // hs, nq),
        in_specs=[
            pl.BlockSpec((T, pw), lambda b, p, i: (b * nq + i, p)),
            pl.BlockSpec((seq, pw), lambda b, p, i: (b, p)),
            pl.BlockSpec((vw, seq), lambda b, p, i: (p, b)),
        ],
        out_specs=pl.BlockSpec((T, vw), lambda b, p, i: (b * nq + i, p)),
        out_shape=jax.ShapeDtypeStruct((batch * seq, MLA_HEADS * MLA_V), BF16),
        compiler_params=_params(("parallel", "parallel", "arbitrary")),
        name="mlp",
    )(qm, km, vmt)


def _pad_rows(x, rows):
    return jnp.concatenate([x, jnp.zeros((rows - x.shape[0], x.shape[1]), x.dtype)], axis=0)


def _fetch_pages(pt_ref, layer, pages, caches, bufs, sem):
    b = pl.program_id(0)
    pg = pl.program_id(1)
    nsteps = pl.num_programs(1)
    step = b * nsteps + pg
    slot = step % 2

    def copies(sl, page_id):
        for j in range(pages):
            pid = page_id(j)
            for a, (cache, buf) in enumerate(zip(caches, bufs)):
                yield pltpu.make_async_copy(cache.at[layer, pid], buf.at[sl, j], sem.at[a, sl])

    @pl.when(step == 0)
    def _():
        for cp in copies(slot, lambda j: pt_ref[b, pg * pages + j]):
            cp.start()

    @pl.when(step + 1 < pl.num_programs(0) * nsteps)
    def _():
        wrap = pg == nsteps - 1
        nb_, ng_ = jnp.where(wrap, b + 1, b), jnp.where(wrap, 0, pg + 1)
        for cp in copies(1 - slot, lambda j: pt_ref[nb_, ng_ * pages + j]):
            cp.start()

    for cp in copies(slot, lambda j: 0):
        cp.wait()
    return slot


def _dfd_kernel(pt_ref, q_ref, kn_ref, vn_ref, bias_ref, lv_ref, hn_ref, cache_kt, cache_v,
                o_ref, q_scr, m_scr, l_scr, acc_scr, kbuf, vbuf, sem, *, pages, page, layer, lam_init):
    slot = _fetch_pages(pt_ref, layer, pages, (cache_kt, cache_v), (kbuf, vbuf), sem)
    k_refs = [kbuf.at[slot, j] for j in range(pages)]
    v_refs = [vbuf.at[slot, j] for j in range(pages)]
    pg = pl.program_id(1)
    last = pl.num_programs(1) - 1
    nrow = DF_HEADS * 2 * SAMPLE_PAD
    width = DF_HEADS * 2 * DF_DK

    @pl.when(pg == 0)
    def _():
        q = q_ref[...].astype(F32)
        qt = jnp.concatenate([q] * (DF_HEADS * 2), axis=0)
        rblk = lax.broadcasted_iota(jnp.int32, (nrow, width), 0) // SAMPLE_PAD
        cblk = lax.broadcasted_iota(jnp.int32, (nrow, width), 1) // DF_DK
        q_scr[...] = jnp.where(rblk == cblk, qt, 0.0).astype(BF16)
        m_scr[...] = jnp.full(m_scr.shape, NEG_INF, F32)
        l_scr[...] = jnp.zeros(l_scr.shape, F32)
        acc_scr[...] = jnp.zeros(acc_scr.shape, F32)

    qbd = q_scr[...]
    hrows = 2 * SAMPLE_PAD

    def pv_heads(p, v_of_head):
        return jnp.concatenate(
            [jnp.dot(p[h * hrows:(h + 1) * hrows], v_of_head(h), preferred_element_type=F32)
             for h in range(DF_HEADS)], axis=0)

    far = bias_ref[0]
    near = jnp.where(pg == last, bias_ref[1], far)
    s = jnp.concatenate(
        [jnp.dot(qbd, k_refs[j][...].astype(BF16), preferred_element_type=F32)
         + (near if j == pages - 1 else far) for j in range(pages)], axis=1)
    v_all = [jnp.concatenate([v_refs[j][pl.ds(h, page, stride=DF_HEADS), :].astype(BF16) for j in range(pages)],
                             axis=0) for h in range(DF_HEADS)]
    carry = _softmax_next((m_scr[...], l_scr[...], acc_scr[...]), s, lambda p: pv_heads(p, lambda h: v_all[h]))
    m_scr[...], l_scr[...], acc_scr[...] = carry

    @pl.when(pg == last)
    def _():
        kn = _pad_rows(kn_ref[...], page).astype(BF16)
        vn = _pad_rows(vn_ref[...], page).astype(BF16)
        s_new = lax.dot_general(qbd, kn, _NT, preferred_element_type=F32) + bias_ref[2]
        m, l, acc = _softmax_next(
            carry, s_new, lambda p: pv_heads(p, lambda h: vn[:, h * DF_DV:(h + 1) * DF_DV]))
        o = acc / l
        lam = _lambda(lv_ref, lam_init)
        for h in range(DF_HEADS):
            r0 = h * hrows
            c = slice(h * DF_DV, (h + 1) * DF_DV)
            od = o[r0:r0 + SAMPLE_PAD] - lam * o[r0 + SAMPLE_PAD:r0 + hrows]
            o_ref[:, c] = _rms(od, hn_ref[:, c]) * (1.0 - lam_init)


def _dfd(layer, dq, dk_new, dv_new, dbias, cache_kt, cache_v, page_table, lw, lam_init):
    nb, n_pages = page_table.shape
    page = cache_kt.shape[3]
    P = math.gcd(DFD_PAGES, n_pages)
    width = DF_HEADS * 2 * DF_DK
    nrow = DF_HEADS * 2 * SAMPLE_PAD
    assert page == LANES

    hbm = pl.BlockSpec(memory_space=pl.ANY)
    row = pl.BlockSpec((SAMPLE_PAD, width), lambda b, g, pt: (b, 0))
    grid_spec = pltpu.PrefetchScalarGridSpec(
        num_scalar_prefetch=1,
        grid=(nb, n_pages // P),
        in_specs=[pl.BlockSpec((SAMPLE_PAD, width), lambda b, g, pt: (b, 0)),
                  row, row,
                  pl.BlockSpec((3, nrow, page), lambda b, g, pt: (0, 0, 0)),
                  pl.BlockSpec((4, DF_DK), lambda b, g, pt: (0, 0)),
                  pl.BlockSpec((1, width), lambda b, g, pt: (0, 0)), hbm, hbm],
        out_specs=pl.BlockSpec((SAMPLE_PAD, width), lambda b, g, pt: (b, 0)),
        scratch_shapes=[pltpu.VMEM((nrow, width), BF16), pltpu.VMEM((nrow, 1), F32),
                        pltpu.VMEM((nrow, 1), F32), pltpu.VMEM((nrow, DF_DV), F32),
                        pltpu.VMEM((2, P, width, page), F32), pltpu.VMEM((2, P, page * DF_HEADS, DF_DV), F32),
                        pltpu.SemaphoreType.DMA((2, 2))],
    )
    return pl.pallas_call(
        functools.partial(_dfd_kernel, pages=P, page=page, layer=layer, lam_init=lam_init),
        grid_spec=grid_spec,
        out_shape=jax.ShapeDtypeStruct((nb * SAMPLE_PAD, width), F32),
        compiler_params=_params(("arbitrary", "arbitrary")),
        name="dfd",
    )(page_table, dq, dk_new, dv_new, dbias, lw['df_lambda'], lw['df_norm'], cache_kt, cache_v)


def _mld_kernel(pt_ref, q_ref, cn_ref, rn_ref, mask_ref, wuk_ref, wuva_ref, wuvb_ref, cache_c, cache_rt,
                o_ref, ql_scr, qr_scr, m_scr, l_scr, acc_scr, cbuf, rbuf, sem, *, pages, page, layer):
    pg = pl.program_id(1)
    last = pl.num_programs(1) - 1
    slot = _fetch_pages(pt_ref, layer, pages, (cache_c, cache_rt), (cbuf, rbuf), sem)
    c_refs = [cbuf.at[slot, j] for j in range(pages)]
    r_refs = [rbuf.at[slot, j] for j in range(pages)]

    @pl.when(pg == 0)
    def _():
        for h in range(MLA_HEADS):
            sl = slice(h * HEAD_PAD, (h + 1) * HEAD_PAD)
            qh = q_ref[:, sl].astype(BF16)
            rows = slice(h * SAMPLE_PAD, (h + 1) * SAMPLE_PAD)
            ql_scr[rows, :] = lax.dot_general(qh, wuk_ref[:, sl], _NT, preferred_element_type=F32)
            qr_scr[rows, :] = q_ref[:, h * HEAD_PAD + MLA_NOPE:h * HEAD_PAD + MLA_NOPE + MLA_ROPE]
        m_scr[...] = jnp.full(m_scr.shape, NEG_INF, F32)
        l_scr[...] = jnp.zeros(l_scr.shape, F32)
        acc_scr[...] = jnp.zeros(acc_scr.shape, F32)

    ql = ql_scr[...].astype(BF16)
    qr = qr_scr[...].astype(BF16)

    per = pages // MLD_CHAINS
    states = []
    for g in range(MLD_CHAINS):
        cb = [c_refs[j][...].astype(BF16) for j in range(g * per, (g + 1) * per)]
        s = jnp.concatenate(
            [lax.dot_general(ql, cb[j], _NT, preferred_element_type=F32)
             + jnp.dot(qr, r_refs[g * per + j][...].astype(BF16), preferred_element_type=F32)
             for j in range(per)], axis=1)
        c_all = jnp.concatenate(cb, axis=0)
        st = _softmax_next((m_scr[g], l_scr[g], acc_scr[g]), s,
                           lambda p, c_all=c_all: jnp.dot(p, c_all, preferred_element_type=F32))
        m_scr[g], l_scr[g], acc_scr[g] = st
        states.append(st)

    @pl.when(pg == last)
    def _():
        m_all = functools.reduce(jnp.maximum, [st[0] for st in states])
        l_all = sum(st[1] * jnp.exp2(st[0] - m_all) for st in states)
        acc_all = sum(st[2] * jnp.exp2(st[0] - m_all) for st in states)
        cn = _pad_rows(cn_ref[...], page).astype(BF16)
        rn = _pad_rows(rn_ref[...], page).astype(BF16)
        s_new = (lax.dot_general(ql, cn, _NT, preferred_element_type=F32)
                 + lax.dot_general(qr, rn, _NT, preferred_element_type=F32) + mask_ref[...])
        m, l, acc = _softmax_next((m_all, l_all, acc_all), s_new,
                                  lambda p: jnp.dot(p, cn, preferred_element_type=F32))
        o = (acc / l).astype(BF16)
        for p in range(MLA_HEADS // 2):
            c = slice(p * 2 * MLA_V, (p + 1) * 2 * MLA_V)
            ra = slice(2 * p * SAMPLE_PAD, (2 * p + 1) * SAMPLE_PAD)
            rb = slice((2 * p + 1) * SAMPLE_PAD, (2 * p + 2) * SAMPLE_PAD)
            y = (jnp.dot(o[ra], wuva_ref[:, c], preferred_element_type=F32)
                 + jnp.dot(o[rb], wuvb_ref[:, c], preferred_element_type=F32))
            o_ref[:, c] = y


def _mld(layer, qm, ckv_new, kr_new, new_mask, cache_c, cache_rt, page_table, lw):
    nb, n_pages = page_table.shape
    page = cache_c.shape[2]
    P = math.gcd(MLD_PAGES, n_pages)
    assert P % MLD_CHAINS == 0
    nrow = MLA_HEADS * SAMPLE_PAD
    qw = MLA_HEADS * HEAD_PAD
    ow = MLA_HEADS * MLA_V

    const = lambda shape: pl.BlockSpec(shape, lambda b, g, pt: (0,) * len(shape))
    hbm = pl.BlockSpec(memory_space=pl.ANY)
    grid_spec = pltpu.PrefetchScalarGridSpec(
        num_scalar_prefetch=1,
        grid=(nb, n_pages // P),
        in_specs=[pl.BlockSpec((SAMPLE_PAD, qw), lambda b, g, pt: (b, 0)),
                  pl.BlockSpec((SAMPLE_PAD, MLA_KV_LORA), lambda b, g, pt: (b, 0)),
                  pl.BlockSpec((SAMPLE_PAD, MLA_ROPE), lambda b, g, pt: (b, 0)),
                  const((nrow, page)), const((MLA_KV_LORA, qw)), const((MLA_KV_LORA, ow)),
                  const((MLA_KV_LORA, ow)), hbm, hbm],
        out_specs=pl.BlockSpec((SAMPLE_PAD, ow), lambda b, g, pt: (b, 0)),
        scratch_shapes=[pltpu.VMEM((nrow, MLA_KV_LORA), F32), pltpu.VMEM((nrow, MLA_ROPE), F32),
                        pltpu.VMEM((MLD_CHAINS, nrow, 1), F32), pltpu.VMEM((MLD_CHAINS, nrow, 1), F32),
                        pltpu.VMEM((MLD_CHAINS, nrow, MLA_KV_LORA), F32),
                        pltpu.VMEM((2, P, page, MLA_KV_LORA), F32), pltpu.VMEM((2, P, MLA_ROPE, page), F32),
                        pltpu.SemaphoreType.DMA((2, 2))],
    )
    return pl.pallas_call(
        functools.partial(_mld_kernel, pages=P, page=page, layer=layer),
        grid_spec=grid_spec,
        out_shape=jax.ShapeDtypeStruct((nb * SAMPLE_PAD, ow), F32),
        compiler_params=_params(("arbitrary", "arbitrary")),
        name="mld",
    )(page_table, qm, ckv_new, kr_new, new_mask, lw['wuk'], lw['wuva'], lw['wuvb'], cache_c, cache_rt)


def _post_kernel(x_ref, ym_ref, yd_ref, yc_ref, gate_ref, g_ref, wbm, wbd, wbc, wout, wfi, wfo, o_ref):
    u = None
    for j, (y_ref, w_ref) in enumerate(((ym_ref, wbm), (yd_ref, wbd), (yc_ref, wbc))):
        t = gate_ref[:, j * D_MODEL:(j + 1) * D_MODEL].astype(F32) * jnp.dot(
            y_ref[...].astype(BF16), w_ref[...], preferred_element_type=F32)
        u = t if u is None else u + t
    x = x_ref[...] + _rms(jnp.dot(u.astype(BF16), wout[...], preferred_element_type=F32), g_ref[0:1])
    hf = _rms(x, g_ref[1:2]).astype(BF16)
    cw = D_FF // 2
    t = None
    for j in range(2):
        a = jnp.dot(hf, wfi[:, j * cw:(j + 1) * cw], preferred_element_type=F32)
        bb = jnp.dot(hf, wfi[:, D_FF + j * cw:D_FF + (j + 1) * cw], preferred_element_type=F32)
        s = (a * jax.nn.sigmoid(a) * bb).astype(BF16)
        d = jnp.dot(s, wfo[j * cw:(j + 1) * cw, :], preferred_element_type=F32)
        t = d if t is None else t + d
    o_ref[...] = x + _rms(t, g_ref[2:3])


def _post(x, ym, yd, yc, gates, lw):
    rows = x.shape[0]
    tm = min(ROW_TILE, rows)
    row = lambda width: pl.BlockSpec((tm, width), lambda i: (i, 0))
    weights = [lw['g123'], lw['w_br_ml'], lw['w_br_df'], lw['w_br_mla'], lw['w_out'], lw['w_ffn_in'], lw['w_ffn_out']]
    return pl.pallas_call(
        _post_kernel,
        grid=(rows // tm,),
        in_specs=[row(D_MODEL), row(512), row(512), row(512), row(N_BRANCH * D_MODEL)]
                 + [_const_spec(w.shape) for w in weights],
        out_specs=row(D_MODEL),
        out_shape=jax.ShapeDtypeStruct((rows, D_MODEL), F32),
        compiler_params=_params(("parallel",)),
        name="post",
    )(x, ym, yd, yc, gates, *weights)


def _rot_half_cols(w):
    half = w.shape[-1] // 2
    return jnp.concatenate([-w[..., half:], w[..., :half]], axis=-1)


def _head_pad(nope, rope):
    k, h = nope.shape[0], nope.shape[1]
    pad = jnp.zeros((k, h, HEAD_PAD - MLA_NOPE - MLA_ROPE), nope.dtype)
    return jnp.concatenate([nope, rope, pad], axis=-1).reshape(k, h * HEAD_PAD)


def _layer_weights(l, norm_gains, w_in, b_ml_gates, ml_head_norm, df_lambda, df_head_norm, mla_q_norm,
                   mla_kv_norm, w_uq, w_uk, w_uv, w_br_ml, w_br_df, w_br_mla, w_out, w_ffn_in, w_ffn_out):
    offs = np.cumsum((0,) + IN_SIZES)
    col = lambda i: w_in[l][:, offs[i]:offs[i + 1]]
    mq, mk, mv, mo, mi, mf, dq, dk, dv, cq, ckv, kr, gl = (col(i) for i in range(len(IN_SIZES)))
    zeros = lambda n: jnp.zeros((D_MODEL, n), F32)
    rope_grp = lambda w: jnp.concatenate([zeros(MLA_NOPE), w, zeros(HEAD_PAD - MLA_NOPE - MLA_ROPE)], axis=1)
    wmisc = jnp.concatenate([mi, mf, zeros(LANES - 2 * ML_HEADS), rope_grp(kr), rope_grp(_rot_half_cols(kr))], axis=1)
    uq = w_uq[l].reshape(MLA_Q_LORA, MLA_HEADS, MLA_NOPE + MLA_ROPE)
    uq_n, uq_r = uq[..., :MLA_NOPE], uq[..., MLA_NOPE:]
    uv = w_uv[l]
    zv = jnp.zeros_like(uv[:, 0::2])
    bf = lambda a: a.astype(BF16)
    return {
        'g0': norm_gains[l, 0:1], 'g123': norm_gains[l, 1:4],
        'wml': bf(jnp.concatenate([mq, mk, mv, mo], axis=1)), 'wdf': bf(jnp.concatenate([dq, dk, dv], axis=1)),
        'wcq': bf(cq), 'wckv': bf(ckv), 'wmisc': bf(wmisc), 'wgl': bf(gl),
        'q_norm': mla_q_norm[l][None], 'kv_norm': mla_kv_norm[l][None],
        'wuqa': bf(_head_pad(uq_n, uq_r)), 'wuqb': bf(_head_pad(jnp.zeros_like(uq_n), _rot_half_cols(uq_r))),
        'wuk': bf(_head_pad(w_uk[l], jnp.zeros((MLA_KV_LORA, MLA_HEADS, MLA_ROPE), F32))),
        'wdvt': bf(dv.T), 'wdkt': bf(dk.T), 'wuvt': bf(uv.reshape(MLA_KV_LORA, MLA_HEADS * MLA_V).T),
        'wuva': bf(jnp.concatenate([uv[:, 0::2], zv], axis=-1).reshape(MLA_KV_LORA, MLA_HEADS * MLA_V)),
        'wuvb': bf(jnp.concatenate([zv, uv[:, 1::2]], axis=-1).reshape(MLA_KV_LORA, MLA_HEADS * MLA_V)),
        'gate_bias': jnp.concatenate([b_ml_gates[l], jnp.zeros((LANES - 2 * ML_HEADS,), F32)])[None],
        'ml_norm': ml_head_norm[l][None], 'df_lambda': df_lambda[l], 'df_norm': df_head_norm[l][None],
        'w_br_ml': bf(w_br_ml[l]), 'w_br_df': bf(w_br_df[l]), 'w_br_mla': bf(w_br_mla[l]),
        'w_out': bf(w_out[l]), 'w_ffn_in': bf(w_ffn_in[l]), 'w_ffn_out': bf(w_ffn_out[l]),
    }


def _rope_tables(pos):
    half = MLA_ROPE // 2
    freqs = ROPE_THETA ** (-jnp.arange(half, dtype=F32) / half)
    ang = pos.astype(F32)[:, None] * freqs[None, :]
    cos, sin = jnp.cos(ang), jnp.sin(ang)
    n = pos.shape[0]
    tail = jnp.zeros((n, HEAD_PAD - MLA_NOPE - MLA_ROPE), F32)
    cs = jnp.concatenate([jnp.ones((n, MLA_NOPE), F32), cos, cos, tail], axis=1)
    sn = jnp.concatenate([jnp.zeros((n, MLA_NOPE), F32), sin, sin, tail], axis=1)
    return cs, sn


def kernel(x_prompt, x_sample, state_mlstm_C, state_mlstm_n, state_mlstm_m, cache_diff_k, cache_diff_v,
           cache_mla_ckv, cache_mla_krope, page_table, norm_gains, w_in, b_ml_gates, ml_head_norm, df_lambda,
           df_head_norm, rel_bias, mla_q_norm, mla_kv_norm, w_uq, w_uk, w_uv, w_br_ml, w_br_df, w_br_mla, w_out,
           w_ffn_in, w_ffn_out):
    depth = w_in.shape[0]
    B, S, _ = x_prompt.shape
    DB, DS, _ = x_sample.shape
    n_pages = page_table.shape[1]
    n_pool, page = cache_diff_k.shape[1], cache_diff_k.shape[2]
    past_len = n_pages * page
    assert DS <= SAMPLE_PAD
    T = ATT_TILE

    cs_p, sn_p = _rope_tables(jnp.arange(S, dtype=jnp.int32))
    pos_s = past_len + jnp.arange(SAMPLE_PAD, dtype=jnp.int32)
    cs_s, sn_s = _rope_tables(jnp.tile(pos_s, min(DB, ROW_TILE // SAMPLE_PAD)))
    ii, jj = np.meshgrid(np.arange(T), np.arange(T), indexing='ij')
    diag, sub = _t5_bucket_np(jj - ii), _t5_bucket_np(T + jj - ii)
    bias_p = _bias_table(rel_bias, np.concatenate([diag, sub], axis=0)).reshape(DF_HEADS, 2, T, T)
    rr, kk = np.meshgrid(np.arange(SAMPLE_PAD), np.arange(page), indexing='ij')
    new_ok = (kk <= rr) & (kk < DS)
    idx_d = np.concatenate([np.full((SAMPLE_PAD, page), REL_BUCKETS - 1, np.int32),
                            _t5_bucket_np(page + rr - kk),
                            np.where(new_ok, _t5_bucket_np(rr - kk), -1)], axis=0)
    bias_d = _bias_table(rel_bias, idx_d).reshape(DF_HEADS, 3, 1, SAMPLE_PAD, page)
    bias_d = jnp.broadcast_to(bias_d, (DF_HEADS, 3, 2, SAMPLE_PAD, page))
    bias_d = jnp.transpose(bias_d, (1, 0, 2, 3, 4)).reshape(3, DF_HEADS * 2 * SAMPLE_PAD, page)
    mla_new_mask = jnp.asarray(np.tile(np.where(new_ok, 0.0, NEG_INF).astype(np.float32), (MLA_HEADS, 1)))

    ckt = jnp.transpose(cache_diff_k, (0, 1, 3, 4, 5, 2)).reshape(depth, n_pool, DF_HEADS * 2 * DF_DK, page)
    crt = jnp.transpose(cache_mla_krope, (0, 1, 3, 2))
    cv = cache_diff_v.reshape(depth, n_pool, page * DF_HEADS, DF_DV)

    xp = x_prompt.reshape(B * S, D_MODEL)
    xs = jnp.pad(x_sample, ((0, 0), (0, SAMPLE_PAD - DS), (0, 0))).reshape(DB * SAMPLE_PAD, D_MODEL)
    zero_c = jnp.zeros((B, ML_HEADS, ML_DK, ML_DV), F32)
    zero_n = jnp.zeros((B, ML_HEADS, ML_DK), F32)
    zero_m = jnp.zeros((B, ML_HEADS), F32)
    st_p, st_s = [], []
    rows_p = (jnp.zeros((depth, B, DF_HEADS * 2 * DF_DK, S), F32), jnp.zeros((depth, B * S * DF_HEADS, DF_DV), F32),
              jnp.zeros((depth, B * S, MLA_KV_LORA), F32), jnp.zeros((depth, B * S, MLA_ROPE), F32))
    for l in range(depth):
        lw = _layer_weights(l, norm_gains, w_in, b_ml_gates, ml_head_norm, df_lambda, df_head_norm, mla_q_norm,
                            mla_kv_norm, w_uq, w_uk, w_uv, w_br_ml, w_br_df, w_br_mla, w_out, w_ffn_in, w_ffn_out)
        lam_init = 0.8 - 0.6 * math.exp(-0.3 * l)

        (ml, gif, dq, dk, dv, qm, ckv, kr, gates, dkb, dvt, km, vmt) = _proj(
            xp, lw, cs_p, sn_p, True, layer=l, depth=depth, stacked=rows_p)
        rows_p = (dk, dv, ckv, kr)
        ym, c_p, n_p, m_p = _mlstm(ml, gif, lw, zero_c, zero_n, zero_m, B, S, math.gcd(S, ML_CHUNK), S)
        yd = _dfp(dq, dkb, dvt, bias_p, rel_bias, lw, B, S, lam_init)
        yc = _mlp(qm, km, vmt, B, S)
        xp = _post(xp, ym, yd, yc, gates, lw)
        st_p.append((c_p, n_p, m_p))

        (ml, gif, dq, dk, dv, qm, ckv, kr, gates) = _proj(xs, lw, cs_s, sn_s, False)
        ym, c_s, n_s, m_s = _mlstm(ml, gif, lw, state_mlstm_C[l], state_mlstm_n[l], state_mlstm_m[l],
                                   DB, SAMPLE_PAD, SAMPLE_PAD, DS)
        yd = _dfd(l, dq, dk, dv, bias_d, ckt, cv, page_table, lw, lam_init)
        yc = _mld(l, qm, ckv, kr, mla_new_mask, cache_mla_ckv, crt, page_table, lw)
        xs = _post(xs, ym, yd, yc, gates, lw)
        tok = lambda a, *tail: a.reshape((DB, SAMPLE_PAD) + tail)[:, :DS]
        st_s.append((tok(dk, DF_HEADS, 2, DF_DK), tok(dv, DF_HEADS, DF_DV), tok(ckv, MLA_KV_LORA),
                     tok(kr, MLA_ROPE), c_s, n_s, m_s))

    dk, dv, ckv, kr = rows_p
    dk = jnp.transpose(dk.reshape(depth, B, DF_HEADS, 2, DF_DK, S), (0, 1, 5, 2, 3, 4))
    outs_p = [dk, dv.reshape(depth, B, S, DF_HEADS, DF_DV),
              ckv.reshape(depth, B, S, MLA_KV_LORA), kr.reshape(depth, B, S, MLA_ROPE)]
    outs_p += [jnp.stack(a) for a in zip(*st_p)]
    outs_s = [jnp.stack(a) for a in zip(*st_s)]
    yp = xp.reshape(B, S, D_MODEL)
    ys = xs.reshape(DB, SAMPLE_PAD, D_MODEL)[:, :DS]
    return (yp, ys, *outs_p, *outs_s)
```

```python
import functools
import math

import numpy as np
import jax
import jax.numpy as jnp
from jax import lax
from jax.experimental import pallas as pl
from jax.experimental.pallas import tpu as pltpu

F32 = jnp.float32
BF16 = jnp.bfloat16

D_MODEL = 1024
ML_HEADS = 4
ML_DK = 128
ML_DV = 128
DF_HEADS = 4
DF_DK = 64
DF_DV = 2 * DF_DK
MLA_HEADS = 8
MLA_NOPE = 64
MLA_ROPE = 32
MLA_V = 64
MLA_Q_LORA = 384
MLA_KV_LORA = 256
ROPE_THETA = 10000.0
REL_BUCKETS = 32
REL_MAX_DIST = 128
N_BRANCH = 3
D_FF = 2816
EPS = 1e-6
NEG_INF = -1e30
LOG2E = math.log2(math.e)
IN_SIZES = (ML_HEADS * ML_DK, ML_HEADS * ML_DK, ML_HEADS * ML_DV, ML_HEADS * ML_DV, ML_HEADS, ML_HEADS,
            DF_HEADS * 2 * DF_DK, DF_HEADS * 2 * DF_DK, DF_HEADS * DF_DV,
            MLA_Q_LORA, MLA_KV_LORA, MLA_ROPE,
            N_BRANCH * D_MODEL)

LANES = 128
SUBLANES = 8
HEAD_PAD = 128
ROW_TILE = 256
ATT_TILE = 512
DFP_HEADS_PER_STEP = 2
MLP_HEADS_PER_STEP = 4
ML_CHUNK = 256
SAMPLE_PAD = 8
DFD_PAGES = 32
MLD_PAGES = 128
MLD_CHAINS = 2
VMEM_LIMIT = 56 * 1024 * 1024

_NT = (((1,), (1,)), ((), ()))
_TN = (((0,), (0,)), ((), ()))


def _params(sem):
    return pltpu.CompilerParams(dimension_semantics=sem, vmem_limit_bytes=VMEM_LIMIT)


def _const_spec(shape):
    nd = len(shape)
    return pl.BlockSpec(shape, lambda *_: (0,) * nd, pipeline_mode=pl.Buffered(1))


def _rms(x, g):
    return x * lax.rsqrt(jnp.mean(x * x, axis=-1, keepdims=True) + EPS) * g


def _t5_bucket_np(n):
    n = np.asarray(n, np.int64)
    exact = REL_BUCKETS // 2
    nf = np.maximum(n, 1).astype(np.float32)
    large = exact + (np.log(nf / np.float32(exact)) / np.float32(math.log(REL_MAX_DIST / exact))
                     * np.float32(REL_BUCKETS - exact)).astype(np.int32)
    large = np.minimum(large, REL_BUCKETS - 1)
    b = np.where(n < exact, n, large)
    return np.where(n < 0, -1, b).astype(np.int32)


def _bias_table_kernel(rb_ref, idx_ref, out_ref):
    idx = idx_ref[...]
    for h in range(DF_HEADS):
        acc = jnp.full(idx.shape, NEG_INF, F32)
        for b in range(REL_BUCKETS):
            acc = jnp.where(idx == b, rb_ref[b, h] * LOG2E, acc)
        out_ref[h] = acc


def _bias_table(rel_bias, idx_np):
    r, c = idx_np.shape
    return pl.pallas_call(
        _bias_table_kernel,
        out_shape=jax.ShapeDtypeStruct((DF_HEADS, r, c), F32),
        in_specs=[pl.BlockSpec(memory_space=pltpu.SMEM), pl.BlockSpec(memory_space=pltpu.VMEM)],
        out_specs=pl.BlockSpec(memory_space=pltpu.VMEM),
        name="bias_table",
    )(rel_bias, jnp.asarray(idx_np))


def _proj_kernel(x_ref, g_ref, cs_ref, sn_ref, wml, wdf, wcq, wckv, wmisc, wgl, qn_ref, kvn_ref,
                 wuqa, wuk, wdvt, wuvt, wdkt, *rest, prompt, n_aliased):
    ml_o, gif_o, dq_o, dk_o, dv_o, qm_o, ckv_o, kr_o, gate_o, *prompt_outs = rest[n_aliased:]
    x = x_ref[...]
    h = _rms(x, g_ref[...]).astype(BF16)

    def mm(w_ref, lo, hi):
        return jnp.dot(h, w_ref[:, lo:hi], preferred_element_type=F32)

    w = ML_HEADS * ML_DK
    act = ml_o.dtype
    ml_o[:, 0:w] = mm(wml, 0, w).astype(act)
    ml_o[:, w:2 * w] = (mm(wml, w, 2 * w) * ML_DK ** -0.5).astype(act)
    ml_o[:, 2 * w:3 * w] = mm(wml, 2 * w, 3 * w).astype(act)
    ml_o[:, 3 * w:4 * w] = jax.nn.sigmoid(mm(wml, 3 * w, 4 * w)).astype(act)

    misc = mm(wmisc, 0, 3 * LANES)
    gif_o[...] = misc[:, 0:LANES]
    cs = cs_ref[...]
    sn = sn_ref[...]
    krp = misc[:, LANES:2 * LANES] * cs + misc[:, 2 * LANES:3 * LANES] * sn
    kr_o[...] = krp[:, MLA_NOPE:MLA_NOPE + MLA_ROPE]

    wd = DF_HEADS * 2 * DF_DK
    dq_o[...] = (mm(wdf, 0, wd) * (DF_DK ** -0.5 * LOG2E)).astype(act)
    dk = mm(wdf, wd, 2 * wd)
    dv = mm(wdf, 2 * wd, 3 * wd)
    if prompt:
        dk_o[...] = lax.dot_general(wdkt[...], h, _NT, preferred_element_type=F32)
        for hh in range(DF_HEADS):
            dv_o[pl.ds(hh, x.shape[0], stride=DF_HEADS), :] = dv[:, hh * DF_DV:(hh + 1) * DF_DV]
    else:
        dk_o[...] = dk
        dv_o[...] = dv

    c_q = _rms(mm(wcq, 0, MLA_Q_LORA), qn_ref[...]).astype(BF16)
    c_kv = _rms(mm(wckv, 0, MLA_KV_LORA), kvn_ref[...])
    ckv_o[...] = c_kv
    scale = (MLA_NOPE + MLA_ROPE) ** -0.5 * LOG2E
    half = MLA_ROPE // 2
    first_half = lax.broadcasted_iota(jnp.int32, (x.shape[0], HEAD_PAD), 1) < MLA_NOPE + half
    for hh in range(MLA_HEADS):
        sl = slice(hh * HEAD_PAD, (hh + 1) * HEAD_PAD)
        qa = jnp.dot(c_q, wuqa[:, sl], preferred_element_type=F32)
        qb = jnp.where(first_half, -pltpu.roll(qa, HEAD_PAD - half, 1), pltpu.roll(qa, half, 1))
        qm_o[:, sl] = ((qa * cs + qb * sn) * scale).astype(act)

    gw = 512
    for j in range(N_BRANCH * D_MODEL // gw):
        gate_o[:, j * gw:(j + 1) * gw] = jax.nn.sigmoid(mm(wgl, j * gw, (j + 1) * gw)).astype(BF16)

    if prompt:
        dkb_o, dvt_o, km_o, vmt_o = prompt_outs
        dkb_o[...] = dk.astype(BF16)
        dvt_o[...] = lax.dot_general(wdvt[...], h, _NT, preferred_element_type=F32).astype(BF16)
        ckb = c_kv.astype(BF16)
        for hh in range(MLA_HEADS):
            sl = slice(hh * HEAD_PAD, (hh + 1) * HEAD_PAD)
            kn = jnp.dot(ckb, wuk[:, sl], preferred_element_type=F32)
            km_o[:, sl] = (kn + krp).astype(BF16)
        vmt_o[...] = lax.dot_general(wuvt[...], ckb, _NT, preferred_element_type=F32).astype(BF16)


_PROJ_LEAVES = (3, 4, 6, 7)


def _proj(x, lw, cs, sn, prompt, layer=0, depth=0, stacked=None):
    rows = x.shape[0]
    tm = min(ROW_TILE, rows)
    assert rows % tm == 0
    row = lambda width: pl.BlockSpec((tm, width), lambda i: (i, 0))
    assert cs.shape[0] % tm == 0 and rows % cs.shape[0] == 0
    pos_tiles = cs.shape[0] // tm
    pos_row = pl.BlockSpec((tm, LANES), lambda i: (i % pos_tiles, 0))
    weights = [lw['wml'], lw['wdf'], lw['wcq'], lw['wckv'], lw['wmisc'], lw['wgl'], lw['q_norm'], lw['kv_norm'],
               lw['wuqa'], lw['wuk'], lw['wdvt'], lw['wuvt'], lw['wdkt']]
    act = BF16 if prompt else F32
    out_widths = [(4 * ML_HEADS * ML_DK, act), (LANES, F32), (512, act), (512, F32), (512, F32),
                  (MLA_HEADS * HEAD_PAD, act), (MLA_KV_LORA, F32), (MLA_ROPE, F32), (N_BRANCH * D_MODEL, BF16)]
    out_specs = [row(wd) for wd, _ in out_widths]
    out_shape = [jax.ShapeDtypeStruct((rows, wd), dt) for wd, dt in out_widths]
    aliased = list(stacked) if stacked is not None else []
    if depth:
        assert stacked is not None
        for k in _PROJ_LEAVES:
            wd, dt = out_widths[k]
            out_specs[k] = pl.BlockSpec((None, tm, wd), lambda i: (layer, i, 0))
            out_shape[k] = jax.ShapeDtypeStruct((depth, rows, wd), dt)
        seq, wd = cs.shape[0], out_widths[_PROJ_LEAVES[0]][0]
        out_specs[_PROJ_LEAVES[0]] = pl.BlockSpec((None, None, wd, tm),
                                                  lambda i: (layer, i // pos_tiles, 0, i % pos_tiles))
        out_shape[_PROJ_LEAVES[0]] = jax.ShapeDtypeStruct((depth, rows // seq, wd, seq), F32)
        out_specs[_PROJ_LEAVES[1]] = pl.BlockSpec((None, tm * DF_HEADS, DF_DV), lambda i: (layer, i, 0))
        out_shape[_PROJ_LEAVES[1]] = jax.ShapeDtypeStruct((depth, rows * DF_HEADS, DF_DV), F32)
    n_in = 4 + len(weights)
    if prompt:
        col = lambda height: pl.BlockSpec((height, tm), lambda i: (0, i))
        out_specs += [row(512), col(512), row(MLA_HEADS * HEAD_PAD), col(512)]
        out_shape += [jax.ShapeDtypeStruct((rows, 512), BF16), jax.ShapeDtypeStruct((512, rows), BF16),
                      jax.ShapeDtypeStruct((rows, MLA_HEADS * HEAD_PAD), BF16),
                      jax.ShapeDtypeStruct((512, rows), BF16)]
    return pl.pallas_call(
        functools.partial(_proj_kernel, prompt=prompt, n_aliased=len(aliased)),
        grid=(rows // tm,),
        in_specs=[row(D_MODEL), _const_spec((1, D_MODEL)), pos_row, pos_row]
                 + [_const_spec(w.shape) for w in weights]
                 + [pl.BlockSpec(memory_space=pl.ANY)] * len(aliased),
        out_specs=out_specs,
        out_shape=out_shape,
        input_output_aliases={n_in + j: k for j, k in enumerate(_PROJ_LEAVES[:len(aliased)])},
        compiler_params=_params(("parallel",)),
        name="proj_prompt" if prompt else "proj_sample",
    )(x, lw['g0'], cs, sn, *weights, *aliased)


def _mlstm_kernel(m0_ref, ml_ref, gif_ref, gb_ref, hn_ref, c0_ref, n0_ref,
                  ym_o, c_o, n_o, m_o, cn_scr, m_scr, *, chunk, t_valid):
    b = pl.program_id(0)
    c = pl.program_id(1)
    L = chunk
    dk, dv = ML_DK, ML_DV
    lane = lax.broadcasted_iota(jnp.int32, (1, LANES), 1)

    @pl.when(c == 0)
    def _():
        for h in range(ML_HEADS):
            ncol = jnp.where(lax.broadcasted_iota(jnp.int32, (dk, LANES), 1) == 0, n0_ref[0, h], 0.0)
            cn_scr[h] = jnp.concatenate([c0_ref[0, h], ncol], axis=1)
            m_scr[h] = jnp.full((SUBLANES, LANES), m0_ref[b, h], F32)

    g = gif_ref[...] + gb_ref[...]
    g = jnp.where(lane < ML_HEADS, g, jnp.minimum(g, 0.0) - jnp.log(1.0 + jnp.exp(-jnp.abs(g))))
    row_i = lax.broadcasted_iota(jnp.int32, (L, L), 0)
    col_i = lax.broadcasted_iota(jnp.int32, (L, L), 1)
    if t_valid < L:
        tok = lax.broadcasted_iota(jnp.int32, (L, LANES), 0)
        g = jnp.where(tok < t_valid, g, jnp.where(lane < ML_HEADS, NEG_INF, 0.0))
    causal = col_i <= row_i
    eye = col_i == row_i

    for h in range(ML_HEADS):
        lic = g[:, h:h + 1]
        lfc = g[:, ML_HEADS + h:ML_HEADS + h + 1]
        br = jnp.sum(jnp.where(row_i <= col_i, lfc, 0.0), axis=0, keepdims=True)
        bc = jnp.sum(jnp.where(eye, br, 0.0), axis=1, keepdims=True)
        ir = jnp.sum(jnp.where(eye, lic, 0.0), axis=0, keepdims=True)
        b_last = jnp.sum(lfc, axis=0, keepdims=True)
        m_prev = m_scr[h][0:1, 0:1]

        q = ml_ref[:, h * dk:(h + 1) * dk].astype(BF16)
        k = ml_ref[:, (ML_HEADS + h) * dk:(ML_HEADS + h + 1) * dk].astype(BF16)
        v = ml_ref[:, (2 * ML_HEADS + h) * dk:(2 * ML_HEADS + h + 1) * dk].astype(BF16)
        og = ml_ref[:, (3 * ML_HEADS + h) * dk:(3 * ML_HEADS + h + 1) * dk]

        dmat = jnp.where(causal, bc - br + ir, NEG_INF)
        inter = bc + m_prev
        mt = jnp.maximum(inter, jnp.max(dmat, axis=1, keepdims=True))
        wgt = jnp.exp(dmat - mt)
        iw = jnp.exp(inter - mt)
        a = lax.dot_general(q, k, _NT, preferred_element_type=F32) * wgt
        cn = cn_scr[h]
        qc = jnp.dot(q, cn.astype(BF16), preferred_element_type=F32)
        num = jnp.dot(a.astype(BF16), v, preferred_element_type=F32) + iw * qc[:, 0:dv]
        den = jnp.sum(a, axis=1, keepdims=True) + iw * qc[:, dv:dv + 1]
        hh = num / jnp.maximum(jnp.abs(den), jnp.exp(-mt))
        y = _rms(hh, hn_ref[:, h * dv:(h + 1) * dv]) * og.astype(F32)
        ym_o[:, h * dv:(h + 1) * dv] = y.astype(ym_o.dtype)

        m_new = jnp.maximum(b_last + m_prev, jnp.max(b_last - br + ir, axis=1, keepdims=True))
        wl = jnp.exp(b_last - bc + lic - m_new)
        dec = jnp.exp(b_last + m_prev - m_new)
        ones_col = jnp.where(lax.broadcasted_iota(jnp.int32, (L, LANES), 1) == 0, wl, 0.0)
        wv = jnp.concatenate([wl * v.astype(F32), ones_col], axis=1).astype(BF16)
        cn_new = dec * cn + lax.dot_general(k, wv, _TN, preferred_element_type=F32)
        cn_scr[h] = cn_new
        m_scr[h] = jnp.broadcast_to(m_new, (SUBLANES, LANES))
        c_o[0, h] = cn_new[:, 0:dv]
        n_o[0, h] = cn_new[:, dv:dv + 1]
        m_o[0, h] = jnp.broadcast_to(m_new, (SUBLANES, LANES))


def _mlstm(ml, gif, lw, c0, n0, m0, batch, seq, chunk, t_valid):
    nc = seq // chunk
    assert seq % chunk == 0
    width = ML_HEADS * ML_DV
    grid_spec = pltpu.PrefetchScalarGridSpec(
        num_scalar_prefetch=1,
        grid=(batch, nc),
        in_specs=[
            pl.BlockSpec((chunk, 4 * width), lambda b, c, m: (b * nc + c, 0)),
            pl.BlockSpec((chunk, LANES), lambda b, c, m: (b * nc + c, 0)),
            pl.BlockSpec((1, LANES), lambda b, c, m: (0, 0)),
            pl.BlockSpec((1, width), lambda b, c, m: (0, 0)),
            pl.BlockSpec((1, ML_HEADS, ML_DK, ML_DV), lambda b, c, m: (b, 0, 0, 0)),
            pl.BlockSpec((1, ML_HEADS, ML_DK, 1), lambda b, c, m: (b, 0, 0, 0)),
        ],
        out_specs=[
            pl.BlockSpec((chunk, width), lambda b, c, m: (b * nc + c, 0)),
            pl.BlockSpec((1, ML_HEADS, ML_DK, ML_DV), lambda b, c, m: (b, 0, 0, 0)),
            pl.BlockSpec((1, ML_HEADS, ML_DK, 1), lambda b, c, m: (b, 0, 0, 0)),
            pl.BlockSpec((1, ML_HEADS, SUBLANES, LANES), lambda b, c, m: (b, 0, 0, 0)),
        ],
        scratch_shapes=[pltpu.VMEM((ML_HEADS, ML_DK, 2 * ML_DV), F32),
                        pltpu.VMEM((ML_HEADS, SUBLANES, LANES), F32)],
    )
    ym, c_new, n_new, m_new = pl.pallas_call(
        functools.partial(_mlstm_kernel, chunk=chunk, t_valid=t_valid),
        grid_spec=grid_spec,
        out_shape=[jax.ShapeDtypeStruct((batch * seq, width), ml.dtype),
                   jax.ShapeDtypeStruct((batch, ML_HEADS, ML_DK, ML_DV), F32),
                   jax.ShapeDtypeStruct((batch, ML_HEADS, ML_DK, 1), F32),
                   jax.ShapeDtypeStruct((batch, ML_HEADS, SUBLANES, LANES), F32)],
        compiler_params=_params(("parallel", "arbitrary")),
        name="mlstm",
    )(m0, ml, gif, lw['gate_bias'], lw['ml_norm'], c0, n0[..., None])
    return ym, c_new, n_new[..., 0], m_new[:, :, 0, 0]


def _softmax_first(s, pv, axis=1):
    m = jnp.max(s, axis=axis, keepdims=True)
    p = jnp.exp2(s - m)
    return m, jnp.sum(p, axis=axis, keepdims=True), pv(p.astype(BF16))


def _softmax_next(carry, s, pv, axis=1, shift=None):
    m, l, acc = carry
    top = jnp.max(s, axis=axis, keepdims=True)
    m_new = jnp.maximum(m, top if shift is None else top + shift)
    alpha = jnp.exp2(m - m_new)
    p = jnp.exp2(s - (m_new if shift is None else m_new - shift))
    return m_new, alpha * l + jnp.sum(p, axis=axis, keepdims=True), alpha * acc + pv(p.astype(BF16))


def _lambda(lv_ref, lam_init):
    lv = lv_ref[...]
    e1 = jnp.exp(jnp.sum(lv[0:1] * lv[1:2], axis=1, keepdims=True))
    e2 = jnp.exp(jnp.sum(lv[2:3] * lv[3:4], axis=1, keepdims=True))
    return e1 - e2 + lam_init


def _dfp_kernel(rb_ref, q_ref, k_ref, vt_ref, bias_ref, lv_ref, hn_ref, o_ref, *, tile, heads, lam_init):
    hg = pl.program_id(1)
    qi = pl.program_id(2)
    T = tile
    hw = 2 * DF_DK
    lane = lax.broadcasted_iota(jnp.int32, (T, hw), 1)
    qs = []
    for j in range(heads):
        q = q_ref[:, j * hw:(j + 1) * hw]
        zero = jnp.zeros_like(q)
        qs += [jnp.where(lane < DF_DK, q, zero), jnp.where(lane >= DF_DK, q, zero)]

    def tile_scores(start):
        s, pvs = [], []
        for c in range(2 * heads):
            j = c // 2
            kt = k_ref[pl.ds(start, T), j * hw:(j + 1) * hw]
            s.append(lax.dot_general(kt, qs[c], _NT, preferred_element_type=F32))
            pvs.append(lambda p, j=j: jnp.dot(vt_ref[j * DF_DV:(j + 1) * DF_DV, pl.ds(start, T)], p,
                                              preferred_element_type=F32))
        return s, pvs

    s, pvs = tile_scores(pl.multiple_of(qi * T, T))
    carry = tuple(_softmax_first(sm + bias_ref[c // 2, 0], pv, axis=0) for c, (sm, pv) in enumerate(zip(s, pvs)))
    sub = jnp.maximum(qi - 1, 0)
    s, pvs = tile_scores(pl.multiple_of(sub * T, T))
    gone = jnp.where(qi == 0, NEG_INF, 0.0)
    carry = tuple(_softmax_next(st, sm + (bias_ref[c // 2, 1] + gone), pv, axis=0)
                  for c, (st, sm, pv) in enumerate(zip(carry, s, pvs)))
    far_bias = [rb_ref[REL_BUCKETS - 1, hg * heads + j] * LOG2E for j in range(heads)]

    def far(ki, carry):
        s, pvs = tile_scores(pl.multiple_of(ki * T, T))
        return tuple(_softmax_next(st, sm, pv, axis=0, shift=far_bias[c // 2])
                     for c, (st, sm, pv) in enumerate(zip(carry, s, pvs)))

    carry = lax.fori_loop(0, sub, far, carry)
    lam = _lambda(lv_ref, lam_init)
    for j in range(heads):
        (_, l0, acc0), (_, l1, acc1) = carry[2 * j], carry[2 * j + 1]
        od = (acc0 / l0 - lam * (acc1 / l1)).T
        sl = slice(j * DF_DV, (j + 1) * DF_DV)
        o_ref[:, sl] = (_rms(od, hn_ref[:, sl]) * (1.0 - lam_init)).astype(BF16)


def _dfp(dq, dkb, dvt, bias_tiles, rel_bias, lw, batch, seq, lam_init):
    T = ATT_TILE
    nq = seq // T
    assert seq % T == 0 and T >= REL_MAX_DIST
    hs = DFP_HEADS_PER_STEP
    hw = hs * 2 * DF_DK
    grid_spec = pltpu.PrefetchScalarGridSpec(
        num_scalar_prefetch=0,
        grid=(batch, DF_HEADS // hs, nq),
        in_specs=[
            pl.BlockSpec(memory_space=pltpu.SMEM),
            pl.BlockSpec((T, hw), lambda b, h, i: (b * nq + i, h)),
            pl.BlockSpec((seq, hw), lambda b, h, i: (b, h)),
            pl.BlockSpec((hs * DF_DV, seq), lambda b, h, i: (h, b)),
            pl.BlockSpec((hs, 2, T, T), lambda b, h, i: (h, 0, 0, 0)),
            pl.BlockSpec((4, DF_DK), lambda b, h, i: (0, 0)),
            pl.BlockSpec((1, hw), lambda b, h, i: (0, h)),
        ],
        out_specs=pl.BlockSpec((T, hw), lambda b, h, i: (b * nq + i, h)),
    )
    return pl.pallas_call(
        functools.partial(_dfp_kernel, tile=T, heads=hs, lam_init=lam_init),
        grid_spec=grid_spec,
        out_shape=jax.ShapeDtypeStruct((batch * seq, DF_HEADS * 2 * DF_DK), BF16),
        compiler_params=_params(("parallel", "parallel", "arbitrary")),
        name="dfp",
    )(rel_bias, dq, dkb, dvt, bias_tiles, lw['df_lambda'], lw['df_norm'])


def _mlp_kernel(q_ref, k_ref, vt_ref, o_ref, *, tile, heads):
    qi = pl.program_id(2)
    T = tile
    qs = [q_ref[:, j * HEAD_PAD:(j + 1) * HEAD_PAD] for j in range(heads)]
    key_i = lax.broadcasted_iota(jnp.int32, (T, T), 0)
    qry_i = lax.broadcasted_iota(jnp.int32, (T, T), 1)
    mask = jnp.where(key_i <= qry_i, 0.0, NEG_INF)
    vw = 2 * MLA_V

    def tile_scores(start):
        kt = k_ref[pl.ds(start, T), :]
        s = [lax.dot_general(kt[:, j * HEAD_PAD:(j + 1) * HEAD_PAD], qs[j], _NT, preferred_element_type=F32)
             for j in range(heads)]
        pvs = [lambda p, j=j: jnp.dot(vt_ref[(j // 2) * vw:(j // 2 + 1) * vw, pl.ds(start, T)], p,
                                      preferred_element_type=F32) for j in range(heads)]
        return s, pvs

    s, pvs = tile_scores(pl.multiple_of(qi * T, T))
    carry = tuple(_softmax_first(sm + mask, pv, axis=0) for sm, pv in zip(s, pvs))

    def far(ki, carry):
        s, pvs = tile_scores(pl.multiple_of(ki * T, T))
        return tuple(_softmax_next(c, sm, pv, axis=0) for c, sm, pv in zip(carry, s, pvs))

    carry = lax.fori_loop(0, qi, far, carry)
    row = lax.broadcasted_iota(jnp.int32, (vw, T), 0)
    for p in range(heads // 2):
        (_, la, acca), (_, lb, accb) = carry[2 * p], carry[2 * p + 1]
        o_ref[:, p * vw:(p + 1) * vw] = jnp.where(row < MLA_V, acca / la, accb / lb).T.astype(BF16)


def _mlp(qm, km, vmt, batch, seq):
    T = ATT_TILE
    nq = seq // T
    hs = MLP_HEADS_PER_STEP
    pw = hs * HEAD_PAD
    vw = hs * MLA_V
    return pl.pallas_call(
        functools.partial(_mlp_kernel, tile=T, heads=hs),
        grid=(batch, MLA_HEADS // hs, nq),
        in_specs=[
            pl.BlockSpec((T, pw), lambda b, p, i: (b * nq + i, p)),
            pl.BlockSpec((seq, pw), lambda b, p, i: (b, p)),
            pl.BlockSpec((vw, seq), lambda b, p, i: (p, b)),
        ],
        out_specs=pl.BlockSpec((T, vw), lambda b, p, i: (b * nq + i, p)),
        out_shape=jax.ShapeDtypeStruct((batch * seq, MLA_HEADS * MLA_V), BF16),
        compiler_params=_params(("parallel", "parallel", "arbitrary")),
        name="mlp",
    )(qm, km, vmt)


def _pad_rows(x, rows):
    return jnp.concatenate([x, jnp.zeros((rows - x.shape[0], x.shape[1]), x.dtype)], axis=0)


def _fetch_pages(pt_ref, layer, pages, caches, bufs, sem):
    b = pl.program_id(0)
    pg = pl.program_id(1)
    nsteps = pl.num_programs(1)
    step = b * nsteps + pg
    slot = step % 2

    def copies(sl, page_id):
        for j in range(pages):
            pid = page_id(j)
            for a, (cache, buf) in enumerate(zip(caches, bufs)):
                yield pltpu.make_async_copy(cache.at[layer, pid], buf.at[sl, j], sem.at[a, sl])

    @pl.when(step == 0)
    def _():
        for cp in copies(slot, lambda j: pt_ref[b, pg * pages + j]):
            cp.start()

    @pl.when(step + 1 < pl.num_programs(0) * nsteps)
    def _():
        wrap = pg == nsteps - 1
        nb_, ng_ = jnp.where(wrap, b + 1, b), jnp.where(wrap, 0, pg + 1)
        for cp in copies(1 - slot, lambda j: pt_ref[nb_, ng_ * pages + j]):
            cp.start()

    for cp in copies(slot, lambda j: 0):
        cp.wait()
    return slot


def _dfd_kernel(pt_ref, q_ref, kn_ref, vn_ref, bias_ref, lv_ref, hn_ref, cache_kt, cache_v,
                o_ref, q_scr, m_scr, l_scr, acc_scr, kbuf, vbuf, sem, *, pages, page, layer, lam_init):
    slot = _fetch_pages(pt_ref, layer, pages, (cache_kt, cache_v), (kbuf, vbuf), sem)
    k_refs = [kbuf.at[slot, j] for j in range(pages)]
    v_refs = [vbuf.at[slot, j] for j in range(pages)]
    pg = pl.program_id(1)
    last = pl.num_programs(1) - 1
    nrow = DF_HEADS * 2 * SAMPLE_PAD
    width = DF_HEADS * 2 * DF_DK

    @pl.when(pg == 0)
    def _():
        q = q_ref[...].astype(F32)
        qt = jnp.concatenate([q] * (DF_HEADS * 2), axis=0)
        rblk = lax.broadcasted_iota(jnp.int32, (nrow, width), 0) // SAMPLE_PAD
        cblk = lax.broadcasted_iota(jnp.int32, (nrow, width), 1) // DF_DK
        q_scr[...] = jnp.where(rblk == cblk, qt, 0.0).astype(BF16)
        m_scr[...] = jnp.full(m_scr.shape, NEG_INF, F32)
        l_scr[...] = jnp.zeros(l_scr.shape, F32)
        acc_scr[...] = jnp.zeros(acc_scr.shape, F32)

    qbd = q_scr[...]
    hrows = 2 * SAMPLE_PAD

    def pv_heads(p, v_of_head):
        return jnp.concatenate(
            [jnp.dot(p[h * hrows:(h + 1) * hrows], v_of_head(h), preferred_element_type=F32)
             for h in range(DF_HEADS)], axis=0)

    far = bias_ref[0]
    near = jnp.where(pg == last, bias_ref[1], far)
    s = jnp.concatenate(
        [jnp.dot(qbd, k_refs[j][...].astype(BF16), preferred_element_type=F32)
         + (near if j == pages - 1 else far) for j in range(pages)], axis=1)
    v_all = [jnp.concatenate([v_refs[j][pl.ds(h, page, stride=DF_HEADS), :].astype(BF16) for j in range(pages)],
                             axis=0) for h in range(DF_HEADS)]
    carry = _softmax_next((m_scr[...], l_scr[...], acc_scr[...]), s, lambda p: pv_heads(p, lambda h: v_all[h]))
    m_scr[...], l_scr[...], acc_scr[...] = carry

    @pl.when(pg == last)
    def _():
        kn = _pad_rows(kn_ref[...], page).astype(BF16)
        vn = _pad_rows(vn_ref[...], page).astype(BF16)
        s_new = lax.dot_general(qbd, kn, _NT, preferred_element_type=F32) + bias_ref[2]
        m, l, acc = _softmax_next(
            carry, s_new, lambda p: pv_heads(p, lambda h: vn[:, h * DF_DV:(h + 1) * DF_DV]))
        o = acc / l
        lam = _lambda(lv_ref, lam_init)
        for h in range(DF_HEADS):
            r0 = h * hrows
            c = slice(h * DF_DV, (h + 1) * DF_DV)
            od = o[r0:r0 + SAMPLE_PAD] - lam * o[r0 + SAMPLE_PAD:r0 + hrows]
            o_ref[:, c] = _rms(od, hn_ref[:, c]) * (1.0 - lam_init)


def _dfd(layer, dq, dk_new, dv_new, dbias, cache_kt, cache_v, page_table, lw, lam_init):
    nb, n_pages = page_table.shape
    page = cache_kt.shape[3]
    P = math.gcd(DFD_PAGES, n_pages)
    width = DF_HEADS * 2 * DF_DK
    nrow = DF_HEADS * 2 * SAMPLE_PAD
    assert page == LANES

    hbm = pl.BlockSpec(memory_space=pl.ANY)
    row = pl.BlockSpec((SAMPLE_PAD, width), lambda b, g, pt: (b, 0))
    grid_spec = pltpu.PrefetchScalarGridSpec(
        num_scalar_prefetch=1,
        grid=(nb, n_pages // P),
        in_specs=[pl.BlockSpec((SAMPLE_PAD, width), lambda b, g, pt: (b, 0)),
                  row, row,
                  pl.BlockSpec((3, nrow, page), lambda b, g, pt: (0, 0, 0)),
                  pl.BlockSpec((4, DF_DK), lambda b, g, pt: (0, 0)),
                  pl.BlockSpec((1, width), lambda b, g, pt: (0, 0)), hbm, hbm],
        out_specs=pl.BlockSpec((SAMPLE_PAD, width), lambda b, g, pt: (b, 0)),
        scratch_shapes=[pltpu.VMEM((nrow, width), BF16), pltpu.VMEM((nrow, 1), F32),
                        pltpu.VMEM((nrow, 1), F32), pltpu.VMEM((nrow, DF_DV), F32),
                        pltpu.VMEM((2, P, width, page), F32), pltpu.VMEM((2, P, page * DF_HEADS, DF_DV), F32),
                        pltpu.SemaphoreType.DMA((2, 2))],
    )
    return pl.pallas_call(
        functools.partial(_dfd_kernel, pages=P, page=page, layer=layer, lam_init=lam_init),
        grid_spec=grid_spec,
        out_shape=jax.ShapeDtypeStruct((nb * SAMPLE_PAD, width), F32),
        compiler_params=_params(("arbitrary", "arbitrary")),
        name="dfd",
    )(page_table, dq, dk_new, dv_new, dbias, lw['df_lambda'], lw['df_norm'], cache_kt, cache_v)


def _mld_kernel(pt_ref, q_ref, cn_ref, rn_ref, mask_ref, wuk_ref, wuva_ref, wuvb_ref, cache_c, cache_rt,
                o_ref, ql_scr, qr_scr, m_scr, l_scr, acc_scr, cbuf, rbuf, sem, *, pages, page, layer):
    pg = pl.program_id(1)
    last = pl.num_programs(1) - 1
    slot = _fetch_pages(pt_ref, layer, pages, (cache_c, cache_rt), (cbuf, rbuf), sem)
    c_refs = [cbuf.at[slot, j] for j in range(pages)]
    r_refs = [rbuf.at[slot, j] for j in range(pages)]

    @pl.when(pg == 0)
    def _():
        for h in range(MLA_HEADS):
            sl = slice(h * HEAD_PAD, (h + 1) * HEAD_PAD)
            qh = q_ref[:, sl].astype(BF16)
            rows = slice(h * SAMPLE_PAD, (h + 1) * SAMPLE_PAD)
            ql_scr[rows, :] = lax.dot_general(qh, wuk_ref[:, sl], _NT, preferred_element_type=F32)
            qr_scr[rows, :] = q_ref[:, h * HEAD_PAD + MLA_NOPE:h * HEAD_PAD + MLA_NOPE + MLA_ROPE]
        m_scr[...] = jnp.full(m_scr.shape, NEG_INF, F32)
        l_scr[...] = jnp.zeros(l_scr.shape, F32)
        acc_scr[...] = jnp.zeros(acc_scr.shape, F32)

    ql = ql_scr[...].astype(BF16)
    qr = qr_scr[...].astype(BF16)

    per = pages // MLD_CHAINS
    states = []
    for g in range(MLD_CHAINS):
        cb = [c_refs[j][...].astype(BF16) for j in range(g * per, (g + 1) * per)]
        s = jnp.concatenate(
            [lax.dot_general(ql, cb[j], _NT, preferred_element_type=F32)
             + jnp.dot(qr, r_refs[g * per + j][...].astype(BF16), preferred_element_type=F32)
             for j in range(per)], axis=1)
        c_all = jnp.concatenate(cb, axis=0)
        st = _softmax_next((m_scr[g], l_scr[g], acc_scr[g]), s,
                           lambda p, c_all=c_all: jnp.dot(p, c_all, preferred_element_type=F32))
        m_scr[g], l_scr[g], acc_scr[g] = st
        states.append(st)

    @pl.when(pg == last)
    def _():
        m_all = functools.reduce(jnp.maximum, [st[0] for st in states])
        l_all = sum(st[1] * jnp.exp2(st[0] - m_all) for st in states)
        acc_all = sum(st[2] * jnp.exp2(st[0] - m_all) for st in states)
        cn = _pad_rows(cn_ref[...], page).astype(BF16)
        rn = _pad_rows(rn_ref[...], page).astype(BF16)
        s_new = (lax.dot_general(ql, cn, _NT, preferred_element_type=F32)
                 + lax.dot_general(qr, rn, _NT, preferred_element_type=F32) + mask_ref[...])
        m, l, acc = _softmax_next((m_all, l_all, acc_all), s_new,
                                  lambda p: jnp.dot(p, cn, preferred_element_type=F32))
        o = (acc / l).astype(BF16)
        for p in range(MLA_HEADS // 2):
            c = slice(p * 2 * MLA_V, (p + 1) * 2 * MLA_V)
            ra = slice(2 * p * SAMPLE_PAD, (2 * p + 1) * SAMPLE_PAD)
            rb = slice((2 * p + 1) * SAMPLE_PAD, (2 * p + 2) * SAMPLE_PAD)
            y = (jnp.dot(o[ra], wuva_ref[:, c], preferred_element_type=F32)
                 + jnp.dot(o[rb], wuvb_ref[:, c], preferred_element_type=F32))
            o_ref[:, c] = y


def _mld(layer, qm, ckv_new, kr_new, new_mask, cache_c, cache_rt, page_table, lw):
    nb, n_pages = page_table.shape
    page = cache_c.shape[2]
    P = math.gcd(MLD_PAGES, n_pages)
    assert P % MLD_CHAINS == 0
    nrow = MLA_HEADS * SAMPLE_PAD
    qw = MLA_HEADS * HEAD_PAD
    ow = MLA_HEADS * MLA_V

    const = lambda shape: pl.BlockSpec(shape, lambda b, g, pt: (0,) * len(shape))
    hbm = pl.BlockSpec(memory_space=pl.ANY)
    grid_spec = pltpu.PrefetchScalarGridSpec(
        num_scalar_prefetch=1,
        grid=(nb, n_pages // P),
        in_specs=[pl.BlockSpec((SAMPLE_PAD, qw), lambda b, g, pt: (b, 0)),
                  pl.BlockSpec((SAMPLE_PAD, MLA_KV_LORA), lambda b, g, pt: (b, 0)),
                  pl.BlockSpec((SAMPLE_PAD, MLA_ROPE), lambda b, g, pt: (b, 0)),
                  const((nrow, page)), const((MLA_KV_LORA, qw)), const((MLA_KV_LORA, ow)),
                  const((MLA_KV_LORA, ow)), hbm, hbm],
        out_specs=pl.BlockSpec((SAMPLE_PAD, ow), lambda b, g, pt: (b, 0)),
        scratch_shapes=[pltpu.VMEM((nrow, MLA_KV_LORA), F32), pltpu.VMEM((nrow, MLA_ROPE), F32),
                        pltpu.VMEM((MLD_CHAINS, nrow, 1), F32), pltpu.VMEM((MLD_CHAINS, nrow, 1), F32),
                        pltpu.VMEM((MLD_CHAINS, nrow, MLA_KV_LORA), F32),
                        pltpu.VMEM((2, P, page, MLA_KV_LORA), F32), pltpu.VMEM((2, P, MLA_ROPE, page), F32),
                        pltpu.SemaphoreType.DMA((2, 2))],
    )
    return pl.pallas_call(
        functools.partial(_mld_kernel, pages=P, page=page, layer=layer),
        grid_spec=grid_spec,
        out_shape=jax.ShapeDtypeStruct((nb * SAMPLE_PAD, ow), F32),
        compiler_params=_params(("arbitrary", "arbitrary")),
        name="mld",
    )(page_table, qm, ckv_new, kr_new, new_mask, lw['wuk'], lw['wuva'], lw['wuvb'], cache_c, cache_rt)


def _post_kernel(x_ref, ym_ref, yd_ref, yc_ref, gate_ref, g_ref, wbm, wbd, wbc, wout, wfi, wfo, o_ref):
    u = None
    for j, (y_ref, w_ref) in enumerate(((ym_ref, wbm), (yd_ref, wbd), (yc_ref, wbc))):
        t = gate_ref[:, j * D_MODEL:(j + 1) * D_MODEL].astype(F32) * jnp.dot(
            y_ref[...].astype(BF16), w_ref[...], preferred_element_type=F32)
        u = t if u is None else u + t
    x = x_ref[...] + _rms(jnp.dot(u.astype(BF16), wout[...], preferred_element_type=F32), g_ref[0:1])
    hf = _rms(x, g_ref[1:2]).astype(BF16)
    cw = D_FF // 2
    t = None
    for j in range(2):
        a = jnp.dot(hf, wfi[:, j * cw:(j + 1) * cw], preferred_element_type=F32)
        bb = jnp.dot(hf, wfi[:, D_FF + j * cw:D_FF + (j + 1) * cw], preferred_element_type=F32)
        s = (a * jax.nn.sigmoid(a) * bb).astype(BF16)
        d = jnp.dot(s, wfo[j * cw:(j + 1) * cw, :], preferred_element_type=F32)
        t = d if t is None else t + d
    o_ref[...] = x + _rms(t, g_ref[2:3])


def _post(x, ym, yd, yc, gates, lw):
    rows = x.shape[0]
    tm = min(ROW_TILE, rows)
    row = lambda width: pl.BlockSpec((tm, width), lambda i: (i, 0))
    weights = [lw['g123'], lw['w_br_ml'], lw['w_br_df'], lw['w_br_mla'], lw['w_out'], lw['w_ffn_in'], lw['w_ffn_out']]
    return pl.pallas_call(
        _post_kernel,
        grid=(rows // tm,),
        in_specs=[row(D_MODEL), row(512), row(512), row(512), row(N_BRANCH * D_MODEL)]
                 + [_const_spec(w.shape) for w in weights],
        out_specs=row(D_MODEL),
        out_shape=jax.ShapeDtypeStruct((rows, D_MODEL), F32),
        compiler_params=_params(("parallel",)),
        name="post",
    )(x, ym, yd, yc, gates, *weights)


def _rot_half_cols(w):
    half = w.shape[-1] // 2
    return jnp.concatenate([-w[..., half:], w[..., :half]], axis=-1)


def _head_pad(nope, rope):
    k, h = nope.shape[0], nope.shape[1]
    pad = jnp.zeros((k, h, HEAD_PAD - MLA_NOPE - MLA_ROPE), nope.dtype)
    return jnp.concatenate([nope, rope, pad], axis=-1).reshape(k, h * HEAD_PAD)


def _layer_weights(l, norm_gains, w_in, b_ml_gates, ml_head_norm, df_lambda, df_head_norm, mla_q_norm,
                   mla_kv_norm, w_uq, w_uk, w_uv, w_br_ml, w_br_df, w_br_mla, w_out, w_ffn_in, w_ffn_out):
    offs = np.cumsum((0,) + IN_SIZES)
    col = lambda i: w_in[l][:, offs[i]:offs[i + 1]]
    mq, mk, mv, mo, mi, mf, dq, dk, dv, cq, ckv, kr, gl = (col(i) for i in range(len(IN_SIZES)))
    zeros = lambda n: jnp.zeros((D_MODEL, n), F32)
    rope_grp = lambda w: jnp.concatenate([zeros(MLA_NOPE), w, zeros(HEAD_PAD - MLA_NOPE - MLA_ROPE)], axis=1)
    wmisc = jnp.concatenate([mi, mf, zeros(LANES - 2 * ML_HEADS), rope_grp(kr), rope_grp(_rot_half_cols(kr))], axis=1)
    uq = w_uq[l].reshape(MLA_Q_LORA, MLA_HEADS, MLA_NOPE + MLA_ROPE)
    uq_n, uq_r = uq[..., :MLA_NOPE], uq[..., MLA_NOPE:]
    uv = w_uv[l]
    zv = jnp.zeros_like(uv[:, 0::2])
    bf = lambda a: a.astype(BF16)
    return {
        'g0': norm_gains[l, 0:1], 'g123': norm_gains[l, 1:4],
        'wml': bf(jnp.concatenate([mq, mk, mv, mo], axis=1)), 'wdf': bf(jnp.concatenate([dq, dk, dv], axis=1)),
        'wcq': bf(cq), 'wckv': bf(ckv), 'wmisc': bf(wmisc), 'wgl': bf(gl),
        'q_norm': mla_q_norm[l][None], 'kv_norm': mla_kv_norm[l][None],
        'wuqa': bf(_head_pad(uq_n, uq_r)),
        'wuk': bf(_head_pad(w_uk[l], jnp.zeros((MLA_KV_LORA, MLA_HEADS, MLA_ROPE), F32))),
        'wdvt': bf(dv.T), 'wdkt': bf(dk.T), 'wuvt': bf(uv.reshape(MLA_KV_LORA, MLA_HEADS * MLA_V).T),
        'wuva': bf(jnp.concatenate([uv[:, 0::2], zv], axis=-1).reshape(MLA_KV_LORA, MLA_HEADS * MLA_V)),
        'wuvb': bf(jnp.concatenate([zv, uv[:, 1::2]], axis=-1).reshape(MLA_KV_LORA, MLA_HEADS * MLA_V)),
        'gate_bias': jnp.concatenate([b_ml_gates[l], jnp.zeros((LANES - 2 * ML_HEADS,), F32)])[None],
        'ml_norm': ml_head_norm[l][None], 'df_lambda': df_lambda[l], 'df_norm': df_head_norm[l][None],
        'w_br_ml': bf(w_br_ml[l]), 'w_br_df': bf(w_br_df[l]), 'w_br_mla': bf(w_br_mla[l]),
        'w_out': bf(w_out[l]), 'w_ffn_in': bf(w_ffn_in[l]), 'w_ffn_out': bf(w_ffn_out[l]),
    }


def _rope_tables(pos):
    half = MLA_ROPE // 2
    freqs = ROPE_THETA ** (-jnp.arange(half, dtype=F32) / half)
    ang = pos.astype(F32)[:, None] * freqs[None, :]
    cos, sin = jnp.cos(ang), jnp.sin(ang)
    n = pos.shape[0]
    tail = jnp.zeros((n, HEAD_PAD - MLA_NOPE - MLA_ROPE), F32)
    cs = jnp.concatenate([jnp.ones((n, MLA_NOPE), F32), cos, cos, tail], axis=1)
    sn = jnp.concatenate([jnp.zeros((n, MLA_NOPE), F32), sin, sin, tail], axis=1)
    return cs, sn


def kernel(x_prompt, x_sample, state_mlstm_C, state_mlstm_n, state_mlstm_m, cache_diff_k, cache_diff_v,
           cache_mla_ckv, cache_mla_krope, page_table, norm_gains, w_in, b_ml_gates, ml_head_norm, df_lambda,
           df_head_norm, rel_bias, mla_q_norm, mla_kv_norm, w_uq, w_uk, w_uv, w_br_ml, w_br_df, w_br_mla, w_out,
           w_ffn_in, w_ffn_out):
    depth = w_in.shape[0]
    B, S, _ = x_prompt.shape
    DB, DS, _ = x_sample.shape
    n_pages = page_table.shape[1]
    n_pool, page = cache_diff_k.shape[1], cache_diff_k.shape[2]
    past_len = n_pages * page
    assert DS <= SAMPLE_PAD
    T = ATT_TILE

    cs_p, sn_p = _rope_tables(jnp.arange(S, dtype=jnp.int32))
    pos_s = past_len + jnp.arange(SAMPLE_PAD, dtype=jnp.int32)
    cs_s, sn_s = _rope_tables(jnp.tile(pos_s, min(DB, ROW_TILE // SAMPLE_PAD)))
    ii, jj = np.meshgrid(np.arange(T), np.arange(T), indexing='ij')
    diag, sub = _t5_bucket_np(jj - ii), _t5_bucket_np(T + jj - ii)
    bias_p = _bias_table(rel_bias, np.concatenate([diag, sub], axis=0)).reshape(DF_HEADS, 2, T, T)
    rr, kk = np.meshgrid(np.arange(SAMPLE_PAD), np.arange(page), indexing='ij')
    new_ok = (kk <= rr) & (kk < DS)
    idx_d = np.concatenate([np.full((SAMPLE_PAD, page), REL_BUCKETS - 1, np.int32),
                            _t5_bucket_np(page + rr - kk),
                            np.where(new_ok, _t5_bucket_np(rr - kk), -1)], axis=0)
    bias_d = _bias_table(rel_bias, idx_d).reshape(DF_HEADS, 3, 1, SAMPLE_PAD, page)
    bias_d = jnp.broadcast_to(bias_d, (DF_HEADS, 3, 2, SAMPLE_PAD, page))
    bias_d = jnp.transpose(bias_d, (1, 0, 2, 3, 4)).reshape(3, DF_HEADS * 2 * SAMPLE_PAD, page)
    mla_new_mask = jnp.asarray(np.tile(np.where(new_ok, 0.0, NEG_INF).astype(np.float32), (MLA_HEADS, 1)))

    ckt = jnp.transpose(cache_diff_k, (0, 1, 3, 4, 5, 2)).reshape(depth, n_pool, DF_HEADS * 2 * DF_DK, page)
    crt = jnp.transpose(cache_mla_krope, (0, 1, 3, 2))
    cv = cache_diff_v.reshape(depth, n_pool, page * DF_HEADS, DF_DV)

    xp = x_prompt.reshape(B * S, D_MODEL)
    xs = jnp.pad(x_sample, ((0, 0), (0, SAMPLE_PAD - DS), (0, 0))).reshape(DB * SAMPLE_PAD, D_MODEL)
    zero_c = jnp.zeros((B, ML_HEADS, ML_DK, ML_DV), F32)
    zero_n = jnp.zeros((B, ML_HEADS, ML_DK), F32)
    zero_m = jnp.zeros((B, ML_HEADS), F32)
    st_p, st_s = [], []
    rows_p = (jnp.zeros((depth, B, DF_HEADS * 2 * DF_DK, S), F32), jnp.zeros((depth, B * S * DF_HEADS, DF_DV), F32),
              jnp.zeros((depth, B * S, MLA_KV_LORA), F32), jnp.zeros((depth, B * S, MLA_ROPE), F32))
    for l in range(depth):
        lw = _layer_weights(l, norm_gains, w_in, b_ml_gates, ml_head_norm, df_lambda, df_head_norm, mla_q_norm,
                            mla_kv_norm, w_uq, w_uk, w_uv, w_br_ml, w_br_df, w_br_mla, w_out, w_ffn_in, w_ffn_out)
        lam_init = 0.8 - 0.6 * math.exp(-0.3 * l)

        (ml, gif, dq, dk, dv, qm, ckv, kr, gates, dkb, dvt, km, vmt) = _proj(
            xp, lw, cs_p, sn_p, True, layer=l, depth=depth, stacked=rows_p)
        rows_p = (dk, dv, ckv, kr)
        ym, c_p, n_p, m_p = _mlstm(ml, gif, lw, zero_c, zero_n, zero_m, B, S, math.gcd(S, ML_CHUNK), S)
        yd = _dfp(dq, dkb, dvt, bias_p, rel_bias, lw, B, S, lam_init)
        yc = _mlp(qm, km, vmt, B, S)
        xp = _post(xp, ym, yd, yc, gates, lw)
        st_p.append((c_p, n_p, m_p))

        (ml, gif, dq, dk, dv, qm, ckv, kr, gates) = _proj(xs, lw, cs_s, sn_s, False)
        ym, c_s, n_s, m_s = _mlstm(ml, gif, lw, state_mlstm_C[l], state_mlstm_n[l], state_mlstm_m[l],
                                   DB, SAMPLE_PAD, SAMPLE_PAD, DS)
        yd = _dfd(l, dq, dk, dv, bias_d, ckt, cv, page_table, lw, lam_init)
        yc = _mld(l, qm, ckv, kr, mla_new_mask, cache_mla_ckv, crt, page_table, lw)
        xs = _post(xs, ym, yd, yc, gates, lw)
        tok = lambda a, *tail: a.reshape((DB, SAMPLE_PAD) + tail)[:, :DS]
        st_s.append((tok(dk, DF_HEADS, 2, DF_DK), tok(dv, DF_HEADS, DF_DV), tok(ckv, MLA_KV_LORA),
                     tok(kr, MLA_ROPE), c_s, n_s, m_s))

    dk, dv, ckv, kr = rows_p
    dk = jnp.transpose(dk.reshape(depth, B, DF_HEADS, 2, DF_DK, S), (0, 1, 5, 2, 3, 4))
    outs_p = [dk, dv.reshape(depth, B, S, DF_HEADS, DF_DV),
              ckv.reshape(depth, B, S, MLA_KV_LORA), kr.reshape(depth, B, S, MLA_ROPE)]
    outs_p += [jnp.stack(a) for a in zip(*st_p)]
    outs_s = [jnp.stack(a) for a in zip(*st_s)]
    yp = xp.reshape(B, S, D_MODEL)
    ys = xs.reshape(DB, SAMPLE_PAD, D_MODEL)[:, :DS]
    return (yp, ys, *outs_p, *outs_s)
```

```python
import functools
import math

import numpy as np
import jax
import jax.numpy as jnp
from jax import lax
from jax.experimental import pallas as pl
from jax.experimental.pallas import tpu as pltpu

F32 = jnp.float32
BF16 = jnp.bfloat16

D_MODEL = 1024
ML_HEADS = 4
ML_DK = 128
ML_DV = 128
DF_HEADS = 4
DF_DK = 64
DF_DV = 2 * DF_DK
MLA_HEADS = 8
MLA_NOPE = 64
MLA_ROPE = 32
MLA_V = 64
MLA_Q_LORA = 384
MLA_KV_LORA = 256
ROPE_THETA = 10000.0
REL_BUCKETS = 32
REL_MAX_DIST = 128
N_BRANCH = 3
D_FF = 2816
EPS = 1e-6
NEG_INF = -1e30
LOG2E = math.log2(math.e)
IN_SIZES = (ML_HEADS * ML_DK, ML_HEADS * ML_DK, ML_HEADS * ML_DV, ML_HEADS * ML_DV, ML_HEADS, ML_HEADS,
            DF_HEADS * 2 * DF_DK, DF_HEADS * 2 * DF_DK, DF_HEADS * DF_DV,
            MLA_Q_LORA, MLA_KV_LORA, MLA_ROPE,
            N_BRANCH * D_MODEL)

LANES = 128
SUBLANES = 8
HEAD_PAD = 128
ROW_TILE = 256
ATT_TILE = 512
DFP_HEADS_PER_STEP = 4
MLP_HEADS_PER_STEP = 4
ML_CHUNK = 256
SAMPLE_PAD = 8
DFD_PAGES = 32
MLD_PAGES = 128
MLD_CHAINS = 2
VMEM_LIMIT = 56 * 1024 * 1024

_NT = (((1,), (1,)), ((), ()))
_TN = (((0,), (0,)), ((), ()))


def _params(sem):
    return pltpu.CompilerParams(dimension_semantics=sem, vmem_limit_bytes=VMEM_LIMIT)


def _const_spec(shape):
    nd = len(shape)
    return pl.BlockSpec(shape, lambda *_: (0,) * nd, pipeline_mode=pl.Buffered(1))


def _rms(x, g):
    return x * lax.rsqrt(jnp.mean(x * x, axis=-1, keepdims=True) + EPS) * g


def _t5_bucket_np(n):
    n = np.asarray(n, np.int64)
    exact = REL_BUCKETS // 2
    nf = np.maximum(n, 1).astype(np.float32)
    large = exact + (np.log(nf / np.float32(exact)) / np.float32(math.log(REL_MAX_DIST / exact))
                     * np.float32(REL_BUCKETS - exact)).astype(np.int32)
    large = np.minimum(large, REL_BUCKETS - 1)
    b = np.where(n < exact, n, large)
    return np.where(n < 0, -1, b).astype(np.int32)


def _bias_table_kernel(rb_ref, idx_ref, out_ref):
    idx = idx_ref[...]
    for h in range(DF_HEADS):
        acc = jnp.full(idx.shape, NEG_INF, F32)
        for b in range(REL_BUCKETS):
            acc = jnp.where(idx == b, rb_ref[b, h] * LOG2E, acc)
        out_ref[h] = acc


def _bias_table(rel_bias, idx_np):
    r, c = idx_np.shape
    return pl.pallas_call(
        _bias_table_kernel,
        out_shape=jax.ShapeDtypeStruct((DF_HEADS, r, c), F32),
        in_specs=[pl.BlockSpec(memory_space=pltpu.SMEM), pl.BlockSpec(memory_space=pltpu.VMEM)],
        out_specs=pl.BlockSpec(memory_space=pltpu.VMEM),
        name="bias_table",
    )(rel_bias, jnp.asarray(idx_np))


def _proj_kernel(x_ref, g_ref, cs_ref, sn_ref, wml, wdf, wcq, wckv, wmisc, wgl, qn_ref, kvn_ref,
                 wuqa, wuk, wdvt, wuvt, wdkt, *rest, prompt, n_aliased):
    ml_o, gif_o, dq_o, dk_o, dv_o, qm_o, ckv_o, kr_o, gate_o, *prompt_outs = rest[n_aliased:]
    x = x_ref[...]
    h = _rms(x, g_ref[...]).astype(BF16)

    def mm(w_ref, lo, hi):
        return jnp.dot(h, w_ref[:, lo:hi], preferred_element_type=F32)

    w = ML_HEADS * ML_DK
    act = ml_o.dtype
    ml_o[:, 0:w] = mm(wml, 0, w).astype(act)
    ml_o[:, w:2 * w] = (mm(wml, w, 2 * w) * ML_DK ** -0.5).astype(act)
    ml_o[:, 2 * w:3 * w] = mm(wml, 2 * w, 3 * w).astype(act)
    ml_o[:, 3 * w:4 * w] = jax.nn.sigmoid(mm(wml, 3 * w, 4 * w)).astype(act)

    misc = mm(wmisc, 0, 3 * LANES)
    gif_o[...] = misc[:, 0:LANES]
    cs = cs_ref[...]
    sn = sn_ref[...]
    krp = misc[:, LANES:2 * LANES] * cs + misc[:, 2 * LANES:3 * LANES] * sn
    kr_o[...] = krp[:, MLA_NOPE:MLA_NOPE + MLA_ROPE]

    wd = DF_HEADS * 2 * DF_DK
    dq_o[...] = (mm(wdf, 0, wd) * (DF_DK ** -0.5 * LOG2E)).astype(act)
    dk = mm(wdf, wd, 2 * wd)
    dv = mm(wdf, 2 * wd, 3 * wd)
    if prompt:
        dk_o[...] = lax.dot_general(wdkt[...], h, _NT, preferred_element_type=F32)
        for hh in range(DF_HEADS):
            dv_o[pl.ds(hh, x.shape[0], stride=DF_HEADS), :] = dv[:, hh * DF_DV:(hh + 1) * DF_DV]
    else:
        dk_o[...] = dk
        dv_o[...] = dv

    c_q = _rms(mm(wcq, 0, MLA_Q_LORA), qn_ref[...]).astype(BF16)
    c_kv = _rms(mm(wckv, 0, MLA_KV_LORA), kvn_ref[...])
    ckv_o[...] = c_kv
    scale = (MLA_NOPE + MLA_ROPE) ** -0.5 * LOG2E
    half = MLA_ROPE // 2
    first_half = lax.broadcasted_iota(jnp.int32, (x.shape[0], HEAD_PAD), 1) < MLA_NOPE + half
    for hh in range(MLA_HEADS):
        sl = slice(hh * HEAD_PAD, (hh + 1) * HEAD_PAD)
        qa = jnp.dot(c_q, wuqa[:, sl], preferred_element_type=F32)
        qb = jnp.where(first_half, -pltpu.roll(qa, HEAD_PAD - half, 1), pltpu.roll(qa, half, 1))
        qm_o[:, sl] = ((qa * cs + qb * sn) * scale).astype(act)

    gw = 512
    for j in range(N_BRANCH * D_MODEL // gw):
        gate_o[:, j * gw:(j + 1) * gw] = jax.nn.sigmoid(mm(wgl, j * gw, (j + 1) * gw)).astype(BF16)

    if prompt:
        dkb_o, dvt_o, km_o, vmt_o = prompt_outs
        dkb_o[...] = dk.astype(BF16)
        dvt_o[...] = lax.dot_general(wdvt[...], h, _NT, preferred_element_type=F32).astype(BF16)
        ckb = c_kv.astype(BF16)
        for hh in range(MLA_HEADS):
            sl = slice(hh * HEAD_PAD, (hh + 1) * HEAD_PAD)
            kn = jnp.dot(ckb, wuk[:, sl], preferred_element_type=F32)
            km_o[:, sl] = (kn + krp).astype(BF16)
        vmt_o[...] = lax.dot_general(wuvt[...], ckb, _NT, preferred_element_type=F32).astype(BF16)


_PROJ_LEAVES = (3, 4, 6, 7)


def _proj(x, lw, cs, sn, prompt, layer=0, depth=0, stacked=None):
    rows = x.shape[0]
    tm = min(ROW_TILE, rows)
    assert rows % tm == 0
    row = lambda width: pl.BlockSpec((tm, width), lambda i: (i, 0))
    assert cs.shape[0] % tm == 0 and rows % cs.shape[0] == 0
    pos_tiles = cs.shape[0] // tm
    pos_row = pl.BlockSpec((tm, LANES), lambda i: (i % pos_tiles, 0))
    weights = [lw['wml'], lw['wdf'], lw['wcq'], lw['wckv'], lw['wmisc'], lw['wgl'], lw['q_norm'], lw['kv_norm'],
               lw['wuqa'], lw['wuk'], lw['wdvt'], lw['wuvt'], lw['wdkt']]
    act = BF16 if prompt else F32
    out_widths = [(4 * ML_HEADS * ML_DK, act), (LANES, F32), (512, act), (512, F32), (512, F32),
                  (MLA_HEADS * HEAD_PAD, act), (MLA_KV_LORA, F32), (MLA_ROPE, F32), (N_BRANCH * D_MODEL, BF16)]
    out_specs = [row(wd) for wd, _ in out_widths]
    out_shape = [jax.ShapeDtypeStruct((rows, wd), dt) for wd, dt in out_widths]
    aliased = list(stacked) if stacked is not None else []
    if depth:
        assert stacked is not None
        for k in _PROJ_LEAVES:
            wd, dt = out_widths[k]
            out_specs[k] = pl.BlockSpec((None, tm, wd), lambda i: (layer, i, 0))
            out_shape[k] = jax.ShapeDtypeStruct((depth, rows, wd), dt)
        seq, wd = cs.shape[0], out_widths[_PROJ_LEAVES[0]][0]
        out_specs[_PROJ_LEAVES[0]] = pl.BlockSpec((None, None, wd, tm),
                                                  lambda i: (layer, i // pos_tiles, 0, i % pos_tiles))
        out_shape[_PROJ_LEAVES[0]] = jax.ShapeDtypeStruct((depth, rows // seq, wd, seq), F32)
        out_specs[_PROJ_LEAVES[1]] = pl.BlockSpec((None, tm * DF_HEADS, DF_DV), lambda i: (layer, i, 0))
        out_shape[_PROJ_LEAVES[1]] = jax.ShapeDtypeStruct((depth, rows * DF_HEADS, DF_DV), F32)
    n_in = 4 + len(weights)
    if prompt:
        col = lambda height: pl.BlockSpec((height, tm), lambda i: (0, i))
        out_specs += [row(512), col(512), row(MLA_HEADS * HEAD_PAD), col(512)]
        out_shape += [jax.ShapeDtypeStruct((rows, 512), BF16), jax.ShapeDtypeStruct((512, rows), BF16),
                      jax.ShapeDtypeStruct((rows, MLA_HEADS * HEAD_PAD), BF16),
                      jax.ShapeDtypeStruct((512, rows), BF16)]
    return pl.pallas_call(
        functools.partial(_proj_kernel, prompt=prompt, n_aliased=len(aliased)),
        grid=(rows // tm,),
        in_specs=[row(D_MODEL), _const_spec((1, D_MODEL)), pos_row, pos_row]
                 + [_const_spec(w.shape) for w in weights]
                 + [pl.BlockSpec(memory_space=pl.ANY)] * len(aliased),
        out_specs=out_specs,
        out_shape=out_shape,
        input_output_aliases={n_in + j: k for j, k in enumerate(_PROJ_LEAVES[:len(aliased)])},
        compiler_params=_params(("parallel",)),
        name="proj_prompt" if prompt else "proj_sample",
    )(x, lw['g0'], cs, sn, *weights, *aliased)


def _mlstm_kernel(m0_ref, ml_ref, gif_ref, gb_ref, hn_ref, c0_ref, n0_ref,
                  ym_o, c_o, n_o, m_o, cn_scr, m_scr, *, chunk, t_valid):
    b = pl.program_id(0)
    c = pl.program_id(1)
    L = chunk
    dk, dv = ML_DK, ML_DV
    lane = lax.broadcasted_iota(jnp.int32, (1, LANES), 1)

    @pl.when(c == 0)
    def _():
        for h in range(ML_HEADS):
            ncol = jnp.where(lax.broadcasted_iota(jnp.int32, (dk, LANES), 1) == 0, n0_ref[0, h], 0.0)
            cn_scr[h] = jnp.concatenate([c0_ref[0, h], ncol], axis=1)
            m_scr[h] = jnp.full((SUBLANES, LANES), m0_ref[b, h], F32)

    g = gif_ref[...] + gb_ref[...]
    g = jnp.where(lane < ML_HEADS, g, jnp.minimum(g, 0.0) - jnp.log(1.0 + jnp.exp(-jnp.abs(g))))
    row_i = lax.broadcasted_iota(jnp.int32, (L, L), 0)
    col_i = lax.broadcasted_iota(jnp.int32, (L, L), 1)
    if t_valid < L:
        tok = lax.broadcasted_iota(jnp.int32, (L, LANES), 0)
        g = jnp.where(tok < t_valid, g, jnp.where(lane < ML_HEADS, NEG_INF, 0.0))
    causal = col_i <= row_i
    eye = col_i == row_i

    for h in range(ML_HEADS):
        lic = g[:, h:h + 1]
        lfc = g[:, ML_HEADS + h:ML_HEADS + h + 1]
        br = jnp.sum(jnp.where(row_i <= col_i, lfc, 0.0), axis=0, keepdims=True)
        bc = jnp.sum(jnp.where(eye, br, 0.0), axis=1, keepdims=True)
        ir = jnp.sum(jnp.where(eye, lic, 0.0), axis=0, keepdims=True)
        b_last = jnp.sum(lfc, axis=0, keepdims=True)
        m_prev = m_scr[h][0:1, 0:1]

        q = ml_ref[:, h * dk:(h + 1) * dk].astype(BF16)
        k = ml_ref[:, (ML_HEADS + h) * dk:(ML_HEADS + h + 1) * dk].astype(BF16)
        v = ml_ref[:, (2 * ML_HEADS + h) * dk:(2 * ML_HEADS + h + 1) * dk].astype(BF16)
        og = ml_ref[:, (3 * ML_HEADS + h) * dk:(3 * ML_HEADS + h + 1) * dk]

        dmat = jnp.where(causal, bc - br + ir, NEG_INF)
        inter = bc + m_prev
        mt = jnp.maximum(inter, jnp.max(dmat, axis=1, keepdims=True))
        wgt = jnp.exp(dmat - mt)
        iw = jnp.exp(inter - mt)
        a = lax.dot_general(q, k, _NT, preferred_element_type=F32) * wgt
        cn = cn_scr[h]
        qc = jnp.dot(q, cn.astype(BF16), preferred_element_type=F32)
        num = jnp.dot(a.astype(BF16), v, preferred_element_type=F32) + iw * qc[:, 0:dv]
        den = jnp.sum(a, axis=1, keepdims=True) + iw * qc[:, dv:dv + 1]
        hh = num / jnp.maximum(jnp.abs(den), jnp.exp(-mt))
        y = _rms(hh, hn_ref[:, h * dv:(h + 1) * dv]) * og.astype(F32)
        ym_o[:, h * dv:(h + 1) * dv] = y.astype(ym_o.dtype)

        m_new = jnp.maximum(b_last + m_prev, jnp.max(b_last - br + ir, axis=1, keepdims=True))
        wl = jnp.exp(b_last - bc + lic - m_new)
        dec = jnp.exp(b_last + m_prev - m_new)
        ones_col = jnp.where(lax.broadcasted_iota(jnp.int32, (L, LANES), 1) == 0, wl, 0.0)
        wv = jnp.concatenate([wl * v.astype(F32), ones_col], axis=1).astype(BF16)
        cn_new = dec * cn + lax.dot_general(k, wv, _TN, preferred_element_type=F32)
        cn_scr[h] = cn_new
        m_scr[h] = jnp.broadcast_to(m_new, (SUBLANES, LANES))
        c_o[0, h] = cn_new[:, 0:dv]
        n_o[0, h] = cn_new[:, dv:dv + 1]
        m_o[0, h] = jnp.broadcast_to(m_new, (SUBLANES, LANES))


def _mlstm(ml, gif, lw, c0, n0, m0, batch, seq, chunk, t_valid):
    nc = seq // chunk
    assert seq % chunk == 0
    width = ML_HEADS * ML_DV
    grid_spec = pltpu.PrefetchScalarGridSpec(
        num_scalar_prefetch=1,
        grid=(batch, nc),
        in_specs=[
            pl.BlockSpec((chunk, 4 * width), lambda b, c, m: (b * nc + c, 0)),
            pl.BlockSpec((chunk, LANES), lambda b, c, m: (b * nc + c, 0)),
            pl.BlockSpec((1, LANES), lambda b, c, m: (0, 0)),
            pl.BlockSpec((1, width), lambda b, c, m: (0, 0)),
            pl.BlockSpec((1, ML_HEADS, ML_DK, ML_DV), lambda b, c, m: (b, 0, 0, 0)),
            pl.BlockSpec((1, ML_HEADS, ML_DK, 1), lambda b, c, m: (b, 0, 0, 0)),
        ],
        out_specs=[
            pl.BlockSpec((chunk, width), lambda b, c, m: (b * nc + c, 0)),
            pl.BlockSpec((1, ML_HEADS, ML_DK, ML_DV), lambda b, c, m: (b, 0, 0, 0)),
            pl.BlockSpec((1, ML_HEADS, ML_DK, 1), lambda b, c, m: (b, 0, 0, 0)),
            pl.BlockSpec((1, ML_HEADS, SUBLANES, LANES), lambda b, c, m: (b, 0, 0, 0)),
        ],
        scratch_shapes=[pltpu.VMEM((ML_HEADS, ML_DK, 2 * ML_DV), F32),
                        pltpu.VMEM((ML_HEADS, SUBLANES, LANES), F32)],
    )
    ym, c_new, n_new, m_new = pl.pallas_call(
        functools.partial(_mlstm_kernel, chunk=chunk, t_valid=t_valid),
        grid_spec=grid_spec,
        out_shape=[jax.ShapeDtypeStruct((batch * seq, width), ml.dtype),
                   jax.ShapeDtypeStruct((batch, ML_HEADS, ML_DK, ML_DV), F32),
                   jax.ShapeDtypeStruct((batch, ML_HEADS, ML_DK, 1), F32),
                   jax.ShapeDtypeStruct((batch, ML_HEADS, SUBLANES, LANES), F32)],
        compiler_params=_params(("parallel", "arbitrary")),
        name="mlstm",
    )(m0, ml, gif, lw['gate_bias'], lw['ml_norm'], c0, n0[..., None])
    return ym, c_new, n_new[..., 0], m_new[:, :, 0, 0]


def _softmax_first(s, pv, axis=1):
    m = jnp.max(s, axis=axis, keepdims=True)
    p = jnp.exp2(s - m)
    return m, jnp.sum(p, axis=axis, keepdims=True), pv(p.astype(BF16))


def _softmax_next(carry, s, pv, axis=1, shift=None):
    m, l, acc = carry
    top = jnp.max(s, axis=axis, keepdims=True)
    m_new = jnp.maximum(m, top if shift is None else top + shift)
    alpha = jnp.exp2(m - m_new)
    p = jnp.exp2(s - (m_new if shift is None else m_new - shift))
    return m_new, alpha * l + jnp.sum(p, axis=axis, keepdims=True), alpha * acc + pv(p.astype(BF16))


def _lambda(lv_ref, lam_init):
    lv = lv_ref[...]
    e1 = jnp.exp(jnp.sum(lv[0:1] * lv[1:2], axis=1, keepdims=True))
    e2 = jnp.exp(jnp.sum(lv[2:3] * lv[3:4], axis=1, keepdims=True))
    return e1 - e2 + lam_init


def _dfp_kernel(rb_ref, q_ref, k_ref, vt_ref, bias_ref, lv_ref, hn_ref, o_ref, *, tile, heads, lam_init):
    hg = pl.program_id(1)
    qi = pl.program_id(2)
    T = tile
    hw = 2 * DF_DK
    lane = lax.broadcasted_iota(jnp.int32, (T, hw), 1)
    qs = []
    for j in range(heads):
        q = q_ref[:, j * hw:(j + 1) * hw]
        zero = jnp.zeros_like(q)
        qs += [jnp.where(lane < DF_DK, q, zero), jnp.where(lane >= DF_DK, q, zero)]

    def tile_scores(start):
        s, pvs = [], []
        for c in range(2 * heads):
            j = c // 2
            kt = k_ref[pl.ds(start, T), j * hw:(j + 1) * hw]
            s.append(lax.dot_general(kt, qs[c], _NT, preferred_element_type=F32))
            pvs.append(lambda p, j=j: jnp.dot(vt_ref[j * DF_DV:(j + 1) * DF_DV, pl.ds(start, T)], p,
                                              preferred_element_type=F32))
        return s, pvs

    s, pvs = tile_scores(pl.multiple_of(qi * T, T))
    carry = tuple(_softmax_first(sm + bias_ref[c // 2, 0], pv, axis=0) for c, (sm, pv) in enumerate(zip(s, pvs)))
    sub = jnp.maximum(qi - 1, 0)
    s, pvs = tile_scores(pl.multiple_of(sub * T, T))
    gone = jnp.where(qi == 0, NEG_INF, 0.0)
    carry = tuple(_softmax_next(st, sm + (bias_ref[c // 2, 1] + gone), pv, axis=0)
                  for c, (st, sm, pv) in enumerate(zip(carry, s, pvs)))
    far_bias = [rb_ref[REL_BUCKETS - 1, hg * heads + j] * LOG2E for j in range(heads)]

    def far(ki, carry):
        s, pvs = tile_scores(pl.multiple_of(ki * T, T))
        return tuple(_softmax_next(st, sm, pv, axis=0, shift=far_bias[c // 2])
                     for c, (st, sm, pv) in enumerate(zip(carry, s, pvs)))

    carry = lax.fori_loop(0, sub, far, carry)
    lam = _lambda(lv_ref, lam_init)
    for j in range(heads):
        (_, l0, acc0), (_, l1, acc1) = carry[2 * j], carry[2 * j + 1]
        od = (acc0 / l0 - lam * (acc1 / l1)).T
        sl = slice(j * DF_DV, (j + 1) * DF_DV)
        o_ref[:, sl] = (_rms(od, hn_ref[:, sl]) * (1.0 - lam_init)).astype(BF16)


def _dfp(dq, dkb, dvt, bias_tiles, rel_bias, lw, batch, seq, lam_init):
    T = ATT_TILE
    nq = seq // T
    assert seq % T == 0 and T >= REL_MAX_DIST
    hs = DFP_HEADS_PER_STEP
    hw = hs * 2 * DF_DK
    grid_spec = pltpu.PrefetchScalarGridSpec(
        num_scalar_prefetch=0,
        grid=(batch, DF_HEADS // hs, nq),
        in_specs=[
            pl.BlockSpec(memory_space=pltpu.SMEM),
            pl.BlockSpec((T, hw), lambda b, h, i: (b * nq + i, h)),
            pl.BlockSpec((seq, hw), lambda b, h, i: (b, h)),
            pl.BlockSpec((hs * DF_DV, seq), lambda b, h, i: (h, b)),
            pl.BlockSpec((hs, 2, T, T), lambda b, h, i: (h, 0, 0, 0)),
            pl.BlockSpec((4, DF_DK), lambda b, h, i: (0, 0)),
            pl.BlockSpec((1, hw), lambda b, h, i: (0, h)),
        ],
        out_specs=pl.BlockSpec((T, hw), lambda b, h, i: (b * nq + i, h)),
    )
    return pl.pallas_call(
        functools.partial(_dfp_kernel, tile=T, heads=hs, lam_init=lam_init),
        grid_spec=grid_spec,
        out_shape=jax.ShapeDtypeStruct((batch * seq, DF_HEADS * 2 * DF_DK), BF16),
        compiler_params=_params(("parallel", "parallel", "arbitrary")),
        name="dfp",
    )(rel_bias, dq, dkb, dvt, bias_tiles, lw['df_lambda'], lw['df_norm'])


def _mlp_kernel(q_ref, k_ref, vt_ref, o_ref, *, tile, heads):
    qi = pl.program_id(2)
    T = tile
    qs = [q_ref[:, j * HEAD_PAD:(j + 1) * HEAD_PAD] for j in range(heads)]
    key_i = lax.broadcasted_iota(jnp.int32, (T, T), 0)
    qry_i = lax.broadcasted_iota(jnp.int32, (T, T), 1)
    mask = jnp.where(key_i <= qry_i, 0.0, NEG_INF)
    vw = 2 * MLA_V

    def tile_scores(start):
        kt = k_ref[pl.ds(start, T), :]
        s = [lax.dot_general(kt[:, j * HEAD_PAD:(j + 1) * HEAD_PAD], qs[j], _NT, preferred_element_type=F32)
             for j in range(heads)]
        pvs = [lambda p, j=j: jnp.dot(vt_ref[(j // 2) * vw:(j // 2 + 1) * vw, pl.ds(start, T)], p,
                                      preferred_element_type=F32) for j in range(heads)]
        return s, pvs

    s, pvs = tile_scores(pl.multiple_of(qi * T, T))
    carry = tuple(_softmax_first(sm + mask, pv, axis=0) for sm, pv in zip(s, pvs))

    def far(ki, carry):
        s, pvs = tile_scores(pl.multiple_of(ki * T, T))
        return tuple(_softmax_next(c, sm, pv, axis=0) for c, sm, pv in zip(carry, s, pvs))

    carry = lax.fori_loop(0, qi, far, carry)
    row = lax.broadcasted_iota(jnp.int32, (vw, T), 0)
    for p in range(heads // 2):
        (_, la, acca), (_, lb, accb) = carry[2 * p], carry[2 * p + 1]
        o_ref[:, p * vw:(p + 1) * vw] = jnp.where(row < MLA_V, acca / la, accb / lb).T.astype(BF16)


def _mlp(qm, km, vmt, batch, seq):
    T = ATT_TILE
    nq = seq // T
    hs = MLP_HEADS_PER_STEP
    pw = hs * HEAD_PAD
    vw = hs * MLA_V
    return pl.pallas_call(
        functools.partial(_mlp_kernel, tile=T, heads=hs),
        grid=(batch, MLA_HEADS // hs, nq),
        in_specs=[
            pl.BlockSpec((T, pw), lambda b, p, i: (b * nq + i, p)),
            pl.BlockSpec((seq, pw), lambda b, p, i: (b, p)),
            pl.BlockSpec((vw, seq), lambda b, p, i: (p, b)),
        ],
        out_specs=pl.BlockSpec((T, vw), lambda b, p, i: (b * nq + i, p)),
        out_shape=jax.ShapeDtypeStruct((batch * seq, MLA_HEADS * MLA_V), BF16),
        compiler_params=_params(("parallel", "parallel", "arbitrary")),
        name="mlp",
    )(qm, km, vmt)


def _pad_rows(x, rows):
    return jnp.concatenate([x, jnp.zeros((rows - x.shape[0], x.shape[1]), x.dtype)], axis=0)


def _fetch_pages(pt_ref, layer, pages, caches, bufs, sem):
    b = pl.program_id(0)
    pg = pl.program_id(1)
    nsteps = pl.num_programs(1)
    step = b * nsteps + pg
    slot = step % 2

    def copies(sl, page_id):
        for j in range(pages):
            pid = page_id(j)
            for a, (cache, buf) in enumerate(zip(caches, bufs)):
                yield pltpu.make_async_copy(cache.at[layer, pid], buf.at[sl, j], sem.at[a, sl])

    @pl.when(step == 0)
    def _():
        for cp in copies(slot, lambda j: pt_ref[b, pg * pages + j]):
            cp.start()

    @pl.when(step + 1 < pl.num_programs(0) * nsteps)
    def _():
        wrap = pg == nsteps - 1
        nb_, ng_ = jnp.where(wrap, b + 1, b), jnp.where(wrap, 0, pg + 1)
        for cp in copies(1 - slot, lambda j: pt_ref[nb_, ng_ * pages + j]):
            cp.start()

    for cp in copies(slot, lambda j: 0):
        cp.wait()
    return slot


def _dfd_kernel(pt_ref, q_ref, kn_ref, vn_ref, bias_ref, lv_ref, hn_ref, cache_kt, cache_v,
                o_ref, q_scr, m_scr, l_scr, acc_scr, kbuf, vbuf, sem, *, pages, page, layer, lam_init):
    slot = _fetch_pages(pt_ref, layer, pages, (cache_kt, cache_v), (kbuf, vbuf), sem)
    k_refs = [kbuf.at[slot, j] for j in range(pages)]
    v_refs = [vbuf.at[slot, j] for j in range(pages)]
    pg = pl.program_id(1)
    last = pl.num_programs(1) - 1
    nrow = DF_HEADS * 2 * SAMPLE_PAD
    width = DF_HEADS * 2 * DF_DK

    @pl.when(pg == 0)
    def _():
        q = q_ref[...].astype(F32)
        qt = jnp.concatenate([q] * (DF_HEADS * 2), axis=0)
        rblk = lax.broadcasted_iota(jnp.int32, (nrow, width), 0) // SAMPLE_PAD
        cblk = lax.broadcasted_iota(jnp.int32, (nrow, width), 1) // DF_DK
        q_scr[...] = jnp.where(rblk == cblk, qt, 0.0).astype(BF16)
        m_scr[...] = jnp.full(m_scr.shape, NEG_INF, F32)
        l_scr[...] = jnp.zeros(l_scr.shape, F32)
        acc_scr[...] = jnp.zeros(acc_scr.shape, F32)

    qbd = q_scr[...]
    hrows = 2 * SAMPLE_PAD

    def pv_heads(p, v_of_head):
        return jnp.concatenate(
            [jnp.dot(p[h * hrows:(h + 1) * hrows], v_of_head(h), preferred_element_type=F32)
             for h in range(DF_HEADS)], axis=0)

    far = bias_ref[0]
    near = jnp.where(pg == last, bias_ref[1], far)
    s = jnp.concatenate(
        [jnp.dot(qbd, k_refs[j][...].astype(BF16), preferred_element_type=F32)
         + (near if j == pages - 1 else far) for j in range(pages)], axis=1)
    v_all = [jnp.concatenate([v_refs[j][pl.ds(h, page, stride=DF_HEADS), :].astype(BF16) for j in range(pages)],
                             axis=0) for h in range(DF_HEADS)]
    carry = _softmax_next((m_scr[...], l_scr[...], acc_scr[...]), s, lambda p: pv_heads(p, lambda h: v_all[h]))
    m_scr[...], l_scr[...], acc_scr[...] = carry

    @pl.when(pg == last)
    def _():
        kn = _pad_rows(kn_ref[...], page).astype(BF16)
        vn = _pad_rows(vn_ref[...], page).astype(BF16)
        s_new = lax.dot_general(qbd, kn, _NT, preferred_element_type=F32) + bias_ref[2]
        m, l, acc = _softmax_next(
            carry, s_new, lambda p: pv_heads(p, lambda h: vn[:, h * DF_DV:(h + 1) * DF_DV]))
        o = acc / l
        lam = _lambda(lv_ref, lam_init)
        for h in range(DF_HEADS):
            r0 = h * hrows
            c = slice(h * DF_DV, (h + 1) * DF_DV)
            od = o[r0:r0 + SAMPLE_PAD] - lam * o[r0 + SAMPLE_PAD:r0 + hrows]
            o_ref[:, c] = _rms(od, hn_ref[:, c]) * (1.0 - lam_init)


def _dfd(layer, dq, dk_new, dv_new, dbias, cache_kt, cache_v, page_table, lw, lam_init):
    nb, n_pages = page_table.shape
    page = cache_kt.shape[3]
    P = math.gcd(DFD_PAGES, n_pages)
    width = DF_HEADS * 2 * DF_DK
    nrow = DF_HEADS * 2 * SAMPLE_PAD
    assert page == LANES

    hbm = pl.BlockSpec(memory_space=pl.ANY)
    row = pl.BlockSpec((SAMPLE_PAD, width), lambda b, g, pt: (b, 0))
    grid_spec = pltpu.PrefetchScalarGridSpec(
        num_scalar_prefetch=1,
        grid=(nb, n_pages // P),
        in_specs=[pl.BlockSpec((SAMPLE_PAD, width), lambda b, g, pt: (b, 0)),
                  row, row,
                  pl.BlockSpec((3, nrow, page), lambda b, g, pt: (0, 0, 0)),
                  pl.BlockSpec((4, DF_DK), lambda b, g, pt: (0, 0)),
                  pl.BlockSpec((1, width), lambda b, g, pt: (0, 0)), hbm, hbm],
        out_specs=pl.BlockSpec((SAMPLE_PAD, width), lambda b, g, pt: (b, 0)),
        scratch_shapes=[pltpu.VMEM((nrow, width), BF16), pltpu.VMEM((nrow, 1), F32),
                        pltpu.VMEM((nrow, 1), F32), pltpu.VMEM((nrow, DF_DV), F32),
                        pltpu.VMEM((2, P, width, page), F32), pltpu.VMEM((2, P, page * DF_HEADS, DF_DV), F32),
                        pltpu.SemaphoreType.DMA((2, 2))],
    )
    return pl.pallas_call(
        functools.partial(_dfd_kernel, pages=P, page=page, layer=layer, lam_init=lam_init),
        grid_spec=grid_spec,
        out_shape=jax.ShapeDtypeStruct((nb * SAMPLE_PAD, width), F32),
        compiler_params=_params(("arbitrary", "arbitrary")),
        name="dfd",
    )(page_table, dq, dk_new, dv_new, dbias, lw['df_lambda'], lw['df_norm'], cache_kt, cache_v)


def _mld_kernel(pt_ref, q_ref, cn_ref, rn_ref, mask_ref, wuk_ref, wuva_ref, wuvb_ref, cache_c, cache_rt,
                o_ref, ql_scr, qr_scr, m_scr, l_scr, acc_scr, cbuf, rbuf, sem, *, pages, page, layer):
    pg = pl.program_id(1)
    last = pl.num_programs(1) - 1
    slot = _fetch_pages(pt_ref, layer, pages, (cache_c, cache_rt), (cbuf, rbuf), sem)
    c_refs = [cbuf.at[slot, j] for j in range(pages)]
    r_refs = [rbuf.at[slot, j] for j in range(pages)]

    @pl.when(pg == 0)
    def _():
        for h in range(MLA_HEADS):
            sl = slice(h * HEAD_PAD, (h + 1) * HEAD_PAD)
            qh = q_ref[:, sl].astype(BF16)
            rows = slice(h * SAMPLE_PAD, (h + 1) * SAMPLE_PAD)
            ql_scr[rows, :] = lax.dot_general(qh, wuk_ref[:, sl], _NT, preferred_element_type=F32)
            qr_scr[rows, :] = q_ref[:, h * HEAD_PAD + MLA_NOPE:h * HEAD_PAD + MLA_NOPE + MLA_ROPE]
        m_scr[...] = jnp.full(m_scr.shape, NEG_INF, F32)
        l_scr[...] = jnp.zeros(l_scr.shape, F32)
        acc_scr[...] = jnp.zeros(acc_scr.shape, F32)

    ql = ql_scr[...].astype(BF16)
    qr = qr_scr[...].astype(BF16)

    per = pages // MLD_CHAINS
    states = []
    for g in range(MLD_CHAINS):
        cb = [c_refs[j][...].astype(BF16) for j in range(g * per, (g + 1) * per)]
        s = jnp.concatenate(
            [lax.dot_general(ql, cb[j], _NT, preferred_element_type=F32)
             + jnp.dot(qr, r_refs[g * per + j][...].astype(BF16), preferred_element_type=F32)
             for j in range(per)], axis=1)
        c_all = jnp.concatenate(cb, axis=0)
        st = _softmax_next((m_scr[g], l_scr[g], acc_scr[g]), s,
                           lambda p, c_all=c_all: jnp.dot(p, c_all, preferred_element_type=F32))
        m_scr[g], l_scr[g], acc_scr[g] = st
        states.append(st)

    @pl.when(pg == last)
    def _():
        m_all = functools.reduce(jnp.maximum, [st[0] for st in states])
        l_all = sum(st[1] * jnp.exp2(st[0] - m_all) for st in states)
        acc_all = sum(st[2] * jnp.exp2(st[0] - m_all) for st in states)
        cn = _pad_rows(cn_ref[...], page).astype(BF16)
        rn = _pad_rows(rn_ref[...], page).astype(BF16)
        s_new = (lax.dot_general(ql, cn, _NT, preferred_element_type=F32)
                 + lax.dot_general(qr, rn, _NT, preferred_element_type=F32) + mask_ref[...])
        m, l, acc = _softmax_next((m_all, l_all, acc_all), s_new,
                                  lambda p: jnp.dot(p, cn, preferred_element_type=F32))
        o = (acc / l).astype(BF16)
        for p in range(MLA_HEADS // 2):
            c = slice(p * 2 * MLA_V, (p + 1) * 2 * MLA_V)
            ra = slice(2 * p * SAMPLE_PAD, (2 * p + 1) * SAMPLE_PAD)
            rb = slice((2 * p + 1) * SAMPLE_PAD, (2 * p + 2) * SAMPLE_PAD)
            y = (jnp.dot(o[ra], wuva_ref[:, c], preferred_element_type=F32)
                 + jnp.dot(o[rb], wuvb_ref[:, c], preferred_element_type=F32))
            o_ref[:, c] = y


def _mld(layer, qm, ckv_new, kr_new, new_mask, cache_c, cache_rt, page_table, lw):
    nb, n_pages = page_table.shape
    page = cache_c.shape[2]
    P = math.gcd(MLD_PAGES, n_pages)
    assert P % MLD_CHAINS == 0
    nrow = MLA_HEADS * SAMPLE_PAD
    qw = MLA_HEADS * HEAD_PAD
    ow = MLA_HEADS * MLA_V

    const = lambda shape: pl.BlockSpec(shape, lambda b, g, pt: (0,) * len(shape))
    hbm = pl.BlockSpec(memory_space=pl.ANY)
    grid_spec = pltpu.PrefetchScalarGridSpec(
        num_scalar_prefetch=1,
        grid=(nb, n_pages // P),
        in_specs=[pl.BlockSpec((SAMPLE_PAD, qw), lambda b, g, pt: (b, 0)),
                  pl.BlockSpec((SAMPLE_PAD, MLA_KV_LORA), lambda b, g, pt: (b, 0)),
                  pl.BlockSpec((SAMPLE_PAD, MLA_ROPE), lambda b, g, pt: (b, 0)),
                  const((nrow, page)), const((MLA_KV_LORA, qw)), const((MLA_KV_LORA, ow)),
                  const((MLA_KV_LORA, ow)), hbm, hbm],
        out_specs=pl.BlockSpec((SAMPLE_PAD, ow), lambda b, g, pt: (b, 0)),
        scratch_shapes=[pltpu.VMEM((nrow, MLA_KV_LORA), F32), pltpu.VMEM((nrow, MLA_ROPE), F32),
                        pltpu.VMEM((MLD_CHAINS, nrow, 1), F32), pltpu.VMEM((MLD_CHAINS, nrow, 1), F32),
                        pltpu.VMEM((MLD_CHAINS, nrow, MLA_KV_LORA), F32),
                        pltpu.VMEM((2, P, page, MLA_KV_LORA), F32), pltpu.VMEM((2, P, MLA_ROPE, page), F32),
                        pltpu.SemaphoreType.DMA((2, 2))],
    )
    return pl.pallas_call(
        functools.partial(_mld_kernel, pages=P, page=page, layer=layer),
        grid_spec=grid_spec,
        out_shape=jax.ShapeDtypeStruct((nb * SAMPLE_PAD, ow), F32),
        compiler_params=_params(("arbitrary", "arbitrary")),
        name="mld",
    )(page_table, qm, ckv_new, kr_new, new_mask, lw['wuk'], lw['wuva'], lw['wuvb'], cache_c, cache_rt)


def _post_kernel(x_ref, ym_ref, yd_ref, yc_ref, gate_ref, g_ref, wbm, wbd, wbc, wout, wfi, wfo, o_ref):
    u = None
    for j, (y_ref, w_ref) in enumerate(((ym_ref, wbm), (yd_ref, wbd), (yc_ref, wbc))):
        t = gate_ref[:, j * D_MODEL:(j + 1) * D_MODEL].astype(F32) * jnp.dot(
            y_ref[...].astype(BF16), w_ref[...], preferred_element_type=F32)
        u = t if u is None else u + t
    x = x_ref[...] + _rms(jnp.dot(u.astype(BF16), wout[...], preferred_element_type=F32), g_ref[0:1])
    hf = _rms(x, g_ref[1:2]).astype(BF16)
    cw = D_FF // 2
    t = None
    for j in range(2):
        a = jnp.dot(hf, wfi[:, j * cw:(j + 1) * cw], preferred_element_type=F32)
        bb = jnp.dot(hf, wfi[:, D_FF + j * cw:D_FF + (j + 1) * cw], preferred_element_type=F32)
        s = (a * jax.nn.sigmoid(a) * bb).astype(BF16)
        d = jnp.dot(s, wfo[j * cw:(j + 1) * cw, :], preferred_element_type=F32)
        t = d if t is None else t + d
    o_ref[...] = x + _rms(t, g_ref[2:3])


def _post(x, ym, yd, yc, gates, lw):
    rows = x.shape[0]
    tm = min(ROW_TILE, rows)
    row = lambda width: pl.BlockSpec((tm, width), lambda i: (i, 0))
    weights = [lw['g123'], lw['w_br_ml'], lw['w_br_df'], lw['w_br_mla'], lw['w_out'], lw['w_ffn_in'], lw['w_ffn_out']]
    return pl.pallas_call(
        _post_kernel,
        grid=(rows // tm,),
        in_specs=[row(D_MODEL), row(512), row(512), row(512), row(N_BRANCH * D_MODEL)]
                 + [_const_spec(w.shape) for w in weights],
        out_specs=row(D_MODEL),
        out_shape=jax.ShapeDtypeStruct((rows, D_MODEL), F32),
        compiler_params=_params(("parallel",)),
        name="post",
    )(x, ym, yd, yc, gates, *weights)


def _rot_half_cols(w):
    half = w.shape[-1] // 2
    return jnp.concatenate([-w[..., half:], w[..., :half]], axis=-1)


def _head_pad(nope, rope):
    k, h = nope.shape[0], nope.shape[1]
    pad = jnp.zeros((k, h, HEAD_PAD - MLA_NOPE - MLA_ROPE), nope.dtype)
    return jnp.concatenate([nope, rope, pad], axis=-1).reshape(k, h * HEAD_PAD)


def _layer_weights(l, norm_gains, w_in, b_ml_gates, ml_head_norm, df_lambda, df_head_norm, mla_q_norm,
                   mla_kv_norm, w_uq, w_uk, w_uv, w_br_ml, w_br_df, w_br_mla, w_out, w_ffn_in, w_ffn_out):
    offs = np.cumsum((0,) + IN_SIZES)
    col = lambda i: w_in[l][:, offs[i]:offs[i + 1]]
    mq, mk, mv, mo, mi, mf, dq, dk, dv, cq, ckv, kr, gl = (col(i) for i in range(len(IN_SIZES)))
    zeros = lambda n: jnp.zeros((D_MODEL, n), F32)
    rope_grp = lambda w: jnp.concatenate([zeros(MLA_NOPE), w, zeros(HEAD_PAD - MLA_NOPE - MLA_ROPE)], axis=1)
    wmisc = jnp.concatenate([mi, mf, zeros(LANES - 2 * ML_HEADS), rope_grp(kr), rope_grp(_rot_half_cols(kr))], axis=1)
    uq = w_uq[l].reshape(MLA_Q_LORA, MLA_HEADS, MLA_NOPE + MLA_ROPE)
    uq_n, uq_r = uq[..., :MLA_NOPE], uq[..., MLA_NOPE:]
    uv = w_uv[l]
    zv = jnp.zeros_like(uv[:, 0::2])
    bf = lambda a: a.astype(BF16)
    return {
        'g0': norm_gains[l, 0:1], 'g123': norm_gains[l, 1:4],
        'wml': bf(jnp.concatenate([mq, mk, mv, mo], axis=1)), 'wdf': bf(jnp.concatenate([dq, dk, dv], axis=1)),
        'wcq': bf(cq), 'wckv': bf(ckv), 'wmisc': bf(wmisc), 'wgl': bf(gl),
        'q_norm': mla_q_norm[l][None], 'kv_norm': mla_kv_norm[l][None],
        'wuqa': bf(_head_pad(uq_n, uq_r)),
        'wuk': bf(_head_pad(w_uk[l], jnp.zeros((MLA_KV_LORA, MLA_HEADS, MLA_ROPE), F32))),
        'wdvt': bf(dv.T), 'wdkt': bf(dk.T), 'wuvt': bf(uv.reshape(MLA_KV_LORA, MLA_HEADS * MLA_V).T),
        'wuva': bf(jnp.concatenate([uv[:, 0::2], zv], axis=-1).reshape(MLA_KV_LORA, MLA_HEADS * MLA_V)),
        'wuvb': bf(jnp.concatenate([zv, uv[:, 1::2]], axis=-1).reshape(MLA_KV_LORA, MLA_HEADS * MLA_V)),
        'gate_bias': jnp.concatenate([b_ml_gates[l], jnp.zeros((LANES - 2 * ML_HEADS,), F32)])[None],
        'ml_norm': ml_head_norm[l][None], 'df_lambda': df_lambda[l], 'df_norm': df_head_norm[l][None],
        'w_br_ml': bf(w_br_ml[l]), 'w_br_df': bf(w_br_df[l]), 'w_br_mla': bf(w_br_mla[l]),
        'w_out': bf(w_out[l]), 'w_ffn_in': bf(w_ffn_in[l]), 'w_ffn_out': bf(w_ffn_out[l]),
    }


def _rope_tables(pos):
    half = MLA_ROPE // 2
    freqs = ROPE_THETA ** (-jnp.arange(half, dtype=F32) / half)
    ang = pos.astype(F32)[:, None] * freqs[None, :]
    cos, sin = jnp.cos(ang), jnp.sin(ang)
    n = pos.shape[0]
    tail = jnp.zeros((n, HEAD_PAD - MLA_NOPE - MLA_ROPE), F32)
    cs = jnp.concatenate([jnp.ones((n, MLA_NOPE), F32), cos, cos, tail], axis=1)
    sn = jnp.concatenate([jnp.zeros((n, MLA_NOPE), F32), sin, sin, tail], axis=1)
    return cs, sn


def kernel(x_prompt, x_sample, state_mlstm_C, state_mlstm_n, state_mlstm_m, cache_diff_k, cache_diff_v,
           cache_mla_ckv, cache_mla_krope, page_table, norm_gains, w_in, b_ml_gates, ml_head_norm, df_lambda,
           df_head_norm, rel_bias, mla_q_norm, mla_kv_norm, w_uq, w_uk, w_uv, w_br_ml, w_br_df, w_br_mla, w_out,
           w_ffn_in, w_ffn_out):
    depth = w_in.shape[0]
    B, S, _ = x_prompt.shape
    DB, DS, _ = x_sample.shape
    n_pages = page_table.shape[1]
    n_pool, page = cache_diff_k.shape[1], cache_diff_k.shape[2]
    past_len = n_pages * page
    assert DS <= SAMPLE_PAD
    T = ATT_TILE

    cs_p, sn_p = _rope_tables(jnp.arange(S, dtype=jnp.int32))
    pos_s = past_len + jnp.arange(SAMPLE_PAD, dtype=jnp.int32)
    cs_s, sn_s = _rope_tables(jnp.tile(pos_s, min(DB, ROW_TILE // SAMPLE_PAD)))
    ii, jj = np.meshgrid(np.arange(T), np.arange(T), indexing='ij')
    diag, sub = _t5_bucket_np(jj - ii), _t5_bucket_np(T + jj - ii)
    bias_p = _bias_table(rel_bias, np.concatenate([diag, sub], axis=0)).reshape(DF_HEADS, 2, T, T)
    rr, kk = np.meshgrid(np.arange(SAMPLE_PAD), np.arange(page), indexing='ij')
    new_ok = (kk <= rr) & (kk < DS)
    idx_d = np.concatenate([np.full((SAMPLE_PAD, page), REL_BUCKETS - 1, np.int32),
                            _t5_bucket_np(page + rr - kk),
                            np.where(new_ok, _t5_bucket_np(rr - kk), -1)], axis=0)
    bias_d = _bias_table(rel_bias, idx_d).reshape(DF_HEADS, 3, 1, SAMPLE_PAD, page)
    bias_d = jnp.broadcast_to(bias_d, (DF_HEADS, 3, 2, SAMPLE_PAD, page))
    bias_d = jnp.transpose(bias_d, (1, 0, 2, 3, 4)).reshape(3, DF_HEADS * 2 * SAMPLE_PAD, page)
    mla_new_mask = jnp.asarray(np.tile(np.where(new_ok, 0.0, NEG_INF).astype(np.float32), (MLA_HEADS, 1)))

    ckt = jnp.transpose(cache_diff_k, (0, 1, 3, 4, 5, 2)).reshape(depth, n_pool, DF_HEADS * 2 * DF_DK, page)
    crt = jnp.transpose(cache_mla_krope, (0, 1, 3, 2))
    cv = cache_diff_v.reshape(depth, n_pool, page * DF_HEADS, DF_DV)

    xp = x_prompt.reshape(B * S, D_MODEL)
    xs = jnp.pad(x_sample, ((0, 0), (0, SAMPLE_PAD - DS), (0, 0))).reshape(DB * SAMPLE_PAD, D_MODEL)
    zero_c = jnp.zeros((B, ML_HEADS, ML_DK, ML_DV), F32)
    zero_n = jnp.zeros((B, ML_HEADS, ML_DK), F32)
    zero_m = jnp.zeros((B, ML_HEADS), F32)
    st_p, st_s = [], []
    rows_p = (jnp.zeros((depth, B, DF_HEADS * 2 * DF_DK, S), F32), jnp.zeros((depth, B * S * DF_HEADS, DF_DV), F32),
              jnp.zeros((depth, B * S, MLA_KV_LORA), F32), jnp.zeros((depth, B * S, MLA_ROPE), F32))
    for l in range(depth):
        lw = _layer_weights(l, norm_gains, w_in, b_ml_gates, ml_head_norm, df_lambda, df_head_norm, mla_q_norm,
                            mla_kv_norm, w_uq, w_uk, w_uv, w_br_ml, w_br_df, w_br_mla, w_out, w_ffn_in, w_ffn_out)
        lam_init = 0.8 - 0.6 * math.exp(-0.3 * l)

        (ml, gif, dq, dk, dv, qm, ckv, kr, gates, dkb, dvt, km, vmt) = _proj(
            xp, lw, cs_p, sn_p, True, layer=l, depth=depth, stacked=rows_p)
        rows_p = (dk, dv, ckv, kr)
        ym, c_p, n_p, m_p = _mlstm(ml, gif, lw, zero_c, zero_n, zero_m, B, S, math.gcd(S, ML_CHUNK), S)
        yd = _dfp(dq, dkb, dvt, bias_p, rel_bias, lw, B, S, lam_init)
        yc = _mlp(qm, km, vmt, B, S)
        xp = _post(xp, ym, yd, yc, gates, lw)
        st_p.append((c_p, n_p, m_p))

        (ml, gif, dq, dk, dv, qm, ckv, kr, gates) = _proj(xs, lw, cs_s, sn_s, False)
        ym, c_s, n_s, m_s = _mlstm(ml, gif, lw, state_mlstm_C[l], state_mlstm_n[l], state_mlstm_m[l],
                                   DB, SAMPLE_PAD, SAMPLE_PAD, DS)
        yd = _dfd(l, dq, dk, dv, bias_d, ckt, cv, page_table, lw, lam_init)
        yc = _mld(l, qm, ckv, kr, mla_new_mask, cache_mla_ckv, crt, page_table, lw)
        xs = _post(xs, ym, yd, yc, gates, lw)
        tok = lambda a, *tail: a.reshape((DB, SAMPLE_PAD) + tail)[:, :DS]
        st_s.append((tok(dk, DF_HEADS, 2, DF_DK), tok(dv, DF_HEADS, DF_DV), tok(ckv, MLA_KV_LORA),
                     tok(kr, MLA_ROPE), c_s, n_s, m_s))

    dk, dv, ckv, kr = rows_p
    dk = jnp.transpose(dk.reshape(depth, B, DF_HEADS, 2, DF_DK, S), (0, 1, 5, 2, 3, 4))
    outs_p = [dk, dv.reshape(depth, B, S, DF_HEADS, DF_DV),
              ckv.reshape(depth, B, S, MLA_KV_LORA), kr.reshape(depth, B, S, MLA_ROPE)]
    outs_p += [jnp.stack(a) for a in zip(*st_p)]
    outs_s = [jnp.stack(a) for a in zip(*st_s)]
    yp = xp.reshape(B, S, D_MODEL)
    ys = xs.reshape(DB, SAMPLE_PAD, D_MODEL)[:, :DS]
    return (yp, ys, *outs_p, *outs_s)
```
